```python
import jax, jax.numpy as jnp
from jax import lax
import numpy as np

D_MODEL = 2048
BATCH = 4
SEQ = 2048
DEPTH = 2
DEC_BATCH = 128
DEC_SEQ = 4
PAST_LEN = 16384
PAGE_SIZE = 128

D_CONV = D_MODEL // 2
CONV_A_WIDTH = 3
N_HEADS = 8
HEAD_DIM = 128
D_GDN = N_HEADS * HEAD_DIM
GDN_CONV_WIDTH = 4
CHUNK = 64
D_FF = 4 * D_MODEL
ALPHA = (2 * DEPTH) ** 0.25
BETA_INIT = (8 * DEPTH) ** -0.25
LN_EPS = 1e-5
NORM_EPS = 1e-6
SPLITS = (D_CONV, D_CONV, D_CONV, D_GDN, D_GDN, D_GDN, D_GDN, N_HEADS, N_HEADS, D_MODEL, D_MODEL)
D_PROJ = sum(SPLITS)

kernel_name = 'hybrid_shortconv_gdn_deepnorm_step'


def layer_norm(x, g, b):
    xf = x.astype(jnp.float32)
    mu = jnp.mean(xf, -1, keepdims=True)
    var = jnp.mean(jnp.square(xf - mu), -1, keepdims=True)
    return ((xf - mu) * lax.rsqrt(var + LN_EPS) * g.astype(jnp.float32) + b.astype(jnp.float32)).astype(x.dtype)


def l2norm(x):
    xf = x.astype(jnp.float32)
    return xf * lax.rsqrt(jnp.sum(xf * xf, -1, keepdims=True) + NORM_EPS)


def causal_depthwise_conv(u, buf, w):
    K = w.shape[0]
    L = u.shape[1]
    full = jnp.concatenate([buf.astype(u.dtype), u], axis=1)
    out = sum(full[:, j:j + L] * w[j] for j in range(K))
    return out, full[:, -(K - 1):]


def gated_delta_rule(q, k, v, g, beta, S0):
    Bsz, L, H, DK = q.shape
    DV = v.shape[-1]
    C = min(CHUNK, L)
    Lp = -(-L // C) * C
    pad = Lp - L
    if pad:
        padf = lambda a: jnp.pad(a, [(0, 0), (0, pad)] + [(0, 0)] * (a.ndim - 2))
        q, k, v, g, beta = padf(q), padf(k), padf(v), padf(g), padf(beta)
    N = Lp // C

    def chunks(a):
        return a.reshape(Bsz, N, C, H, -1).transpose(1, 0, 3, 2, 4)

    qc = chunks(q * (DK ** -0.5))
    kc = chunks(k)
    vc = chunks(v)
    gc = chunks(g[..., None])[..., 0]
    bc = chunks(beta[..., None])
    gcum = jnp.cumsum(gc, axis=-1)
    idx = jnp.arange(C)
    causal = idx[:, None] >= idx[None, :]
    strict = idx[:, None] > idx[None, :]
    diff = gcum[..., :, None] - gcum[..., None, :]
    decay = jnp.where(causal, jnp.exp(jnp.where(causal, diff, 0.0)), 0.0)
    kb = kc * bc
    M = jnp.where(strict, jnp.einsum('nbhid,nbhjd->nbhij', kb, kc) * decay, 0.0)
    eye = jnp.eye(C, dtype=jnp.float32)
    T = lax.linalg.triangular_solve(eye + M, jnp.broadcast_to(eye, M.shape),
                                    left_side=True, lower=True, unit_diagonal=True)
    u = T @ (vc * bc)
    w = T @ (kb * jnp.exp(gcum)[..., None])
    attn = jnp.einsum('nbhid,nbhjd->nbhij', qc, kc) * decay
    glast = gcum[..., -1:]
    kdec = kc * jnp.exp(glast - gcum)[..., None]
    qdec = qc * jnp.exp(gcum)[..., None]

    def step(S, xs):
        qd, kd, u_c, w_c, a_c, gl = xs
        v_new = u_c - w_c @ S
        o = qd @ S + a_c @ v_new
        S = S * jnp.exp(gl)[..., None] + jnp.einsum('bhcd,bhce->bhde', kd, v_new)
        return S, o

    S, o = lax.scan(step, S0, (qdec, kdec, u, w, attn, glast))
    o = o.transpose(1, 0, 3, 2, 4).reshape(Bsz, Lp, H, DV)[:, :L]
    return o, S


def hybrid_layer(x, conv_a_buf, gdn_buf, S0, w_in, conv_a_w, gdn_conv_w, a_log, dt_bias, gdn_norm_w,
                 w_a_out, w_b_out, w_o, ln1_g, ln1_b, w_up, w_down, ln2_g, ln2_b):
    Bsz, L, _ = x.shape
    proj = x @ w_in
    split_points = np.cumsum(SPLITS)[:-1].tolist()
    gB, gC, h, q, k, v, z, b_logit, a_logit, m_a, m_b = jnp.split(proj, split_points, axis=-1)

    conv_out, new_conv_a = causal_depthwise_conv(gC * h, conv_a_buf, conv_a_w)
    y_a = (gB * conv_out) @ w_a_out

    qkv, new_gdn_buf = causal_depthwise_conv(jnp.concatenate([q, k, v], -1), gdn_buf, gdn_conv_w)
    q, k, v = jnp.split(jax.nn.silu(qkv), 3, axis=-1)
    hs = (Bsz, L, N_HEADS, HEAD_DIM)
    qh = l2norm(q.reshape(hs))
    kh = l2norm(k.reshape(hs))
    vh = v.reshape(hs).astype(jnp.float32)
    beta = jax.nn.sigmoid(b_logit.astype(jnp.float32))
    g = -jnp.exp(a_log.astype(jnp.float32)) * jax.nn.softplus(a_logit.astype(jnp.float32) + dt_bias.astype(jnp.float32))
    o, S = gated_delta_rule(qh, kh, vh, g, beta, S0.astype(jnp.float32))
    o = o * lax.rsqrt(jnp.mean(o * o, -1, keepdims=True) + NORM_EPS) * gdn_norm_w.astype(jnp.float32)
    o = o * jax.nn.silu(z.reshape(hs).astype(jnp.float32))
    y_b = o.reshape(Bsz, L, D_GDN).astype(x.dtype) @ w_b_out

    merged = jax.nn.sigmoid(m_a) * y_a + jax.nn.sigmoid(m_b) * y_b
    x = layer_norm(ALPHA * x + merged @ w_o, ln1_g, ln1_b)
    hid = jnp.square(jax.nn.relu(x @ w_up))
    x = layer_norm(ALPHA * x + hid @ w_down, ln2_g, ln2_b)
    return x, new_conv_a, new_gdn_buf, S.astype(S0.dtype)


def run_trunk(x, st_conv_a, st_gdn_conv, st_gdn, w_in, conv_a_w, gdn_conv_w, a_log, dt_bias, gdn_norm_w,
              w_a_out, w_b_out, w_o, ln1_g, ln1_b, w_up, w_down, ln2_g, ln2_b):
    ca, cg, sg = [], [], []
    for l in range(DEPTH):
        x, c1, c2, s = hybrid_layer(x, st_conv_a[l], st_gdn_conv[l], st_gdn[l], w_in[l], conv_a_w[l], gdn_conv_w[l],
                                    a_log[l], dt_bias[l], gdn_norm_w[l], w_a_out[l], w_b_out[l], w_o[l],
                                    ln1_g[l], ln1_b[l], w_up[l], w_down[l], ln2_g[l], ln2_b[l])
        ca.append(c1)
        cg.append(c2)
        sg.append(s)
    return x, jnp.stack(ca), jnp.stack(cg), jnp.stack(sg)


def setup_inputs(seed: int = 0) -> dict:
    key = jax.random.key(seed)
    ks = jax.random.split(key, 24)
    f32 = jnp.float32
    nrm = lambda k, shape, s: jax.random.normal(k, shape, f32) * s
    col_scale = jnp.concatenate([
        jnp.ones((2 * D_CONV,), f32), jnp.full((D_CONV,), BETA_INIT, f32),
        jnp.ones((2 * D_GDN,), f32), jnp.full((D_GDN,), BETA_INIT, f32),
        jnp.ones((D_GDN + 2 * N_HEADS + 2 * D_MODEL,), f32)])
    dt = jnp.exp(jax.random.uniform(ks[5], (DEPTH, N_HEADS), f32, np.log(1e-3), np.log(1e-1)))
    return {
        'x_prompt': nrm(ks[0], (BATCH, SEQ, D_MODEL), 1.0),
        'x_sample': nrm(ks[1], (DEC_BATCH, DEC_SEQ, D_MODEL), 1.0),
        'state_conv_a': nrm(ks[2], (DEPTH, DEC_BATCH, CONV_A_WIDTH - 1, D_CONV), 1.0),
        'state_gdn_conv': nrm(ks[3], (DEPTH, DEC_BATCH, GDN_CONV_WIDTH - 1, 3 * D_GDN), 1.0),
        'state_gdn': nrm(ks[4], (DEPTH, DEC_BATCH, N_HEADS, HEAD_DIM, HEAD_DIM), 0.1),
        'w_in': nrm(ks[6], (DEPTH, D_MODEL, D_PROJ), D_MODEL ** -0.5) * col_scale,
        'conv_a_w': nrm(ks[7], (DEPTH, CONV_A_WIDTH, D_CONV), CONV_A_WIDTH ** -0.5),
        'gdn_conv_w': nrm(ks[8], (DEPTH, GDN_CONV_WIDTH, 3 * D_GDN), GDN_CONV_WIDTH ** -0.5),
        'a_log': jnp.log(jax.random.uniform(ks[9], (DEPTH, N_HEADS), f32, 1.0, 16.0)),
        'dt_bias': dt + jnp.log(-jnp.expm1(-dt)),
        'gdn_norm_w': 1.0 + nrm(ks[10], (DEPTH, HEAD_DIM), 0.02),
        'w_a_out': nrm(ks[11], (DEPTH, D_CONV, D_MODEL), BETA_INIT * D_CONV ** -0.5),
        'w_b_out': nrm(ks[12], (DEPTH, D_GDN, D_MODEL), BETA_INIT * D_GDN ** -0.5),
        'w_o': nrm(ks[13], (DEPTH, D_MODEL, D_MODEL), BETA_INIT * D_MODEL ** -0.5),
        'ln1_g': 1.0 + nrm(ks[14], (DEPTH, D_MODEL), 0.02),
        'ln1_b': nrm(ks[15], (DEPTH, D_MODEL), 0.02),
        'w_up': nrm(ks[16], (DEPTH, D_MODEL, D_FF), BETA_INIT * D_MODEL ** -0.5),
        'w_down': nrm(ks[17], (DEPTH, D_FF, D_MODEL), BETA_INIT * D_FF ** -0.5),
        'ln2_g': 1.0 + nrm(ks[18], (DEPTH, D_MODEL), 0.02),
        'ln2_b': nrm(ks[19], (DEPTH, D_MODEL), 0.02),
    }


def reference(x_prompt, x_sample, state_conv_a, state_gdn_conv, state_gdn, w_in, conv_a_w, gdn_conv_w, a_log,
              dt_bias, gdn_norm_w, w_a_out, w_b_out, w_o, ln1_g, ln1_b, w_up, w_down, ln2_g, ln2_b):
    bp = x_prompt.shape[0]
    dtp = x_prompt.dtype
    z_conv_a = jnp.zeros((DEPTH, bp, CONV_A_WIDTH - 1, D_CONV), dtp)
    z_gdn_conv = jnp.zeros((DEPTH, bp, GDN_CONV_WIDTH - 1, 3 * D_GDN), dtp)
    z_gdn = jnp.zeros((DEPTH, bp, N_HEADS, HEAD_DIM, HEAD_DIM), dtp)
    y_prompt, new_conv_a_prompt, new_gdn_conv_prompt, new_gdn_prompt = run_trunk(
        x_prompt, z_conv_a, z_gdn_conv, z_gdn, w_in, conv_a_w, gdn_conv_w, a_log, dt_bias, gdn_norm_w,
        w_a_out, w_b_out, w_o, ln1_g, ln1_b, w_up, w_down, ln2_g, ln2_b)
    y_sample, new_conv_a_sample, new_gdn_conv_sample, new_gdn_sample = run_trunk(
        x_sample, state_conv_a, state_gdn_conv, state_gdn, w_in, conv_a_w, gdn_conv_w, a_log, dt_bias, gdn_norm_w,
        w_a_out, w_b_out, w_o, ln1_g, ln1_b, w_up, w_down, ln2_g, ln2_b)
    return (y_prompt, y_sample, new_conv_a_prompt, new_gdn_conv_prompt, new_gdn_prompt,
            new_conv_a_sample, new_gdn_conv_sample, new_gdn_sample)
```

```python
import functools

import jax
import jax.numpy as jnp
from jax import lax
from jax.experimental import pallas as pl
from jax.experimental.pallas import tpu as pltpu

F32 = jnp.float32
BF16 = jnp.bfloat16

D_MODEL = 2048
DEPTH = 2
D_CONV = D_MODEL // 2
N_HEADS = 8
HEAD_DIM = 128
D_GDN = N_HEADS * HEAD_DIM
D_FF = 4 * D_MODEL
ALPHA = (2 * DEPTH) ** 0.25
LN_EPS = 1e-5
NORM_EPS = 1e-6

COL_GB, COL_GC, COL_H = 0, D_CONV, 2 * D_CONV
COL_QKV = 3 * D_CONV
COL_MA = COL_QKV + 3 * D_GDN
COL_MB = COL_MA + D_MODEL
COL_Z = COL_MB + D_MODEL
COL_LOGIT = COL_Z + D_GDN
LANE = 128
N_PROJ = COL_LOGIT + 2 * LANE
LOGIT_B, LOGIT_A = 0, N_HEADS

GROUP = 64
SAMPLE_LEN = 4
SEQ_PER_GROUP = GROUP // SAMPLE_LEN
INV_PASSES = 3

VMEM_LIMIT = 56 * 1024 * 1024


def _dot(a, b):
    return jnp.dot(a.astype(BF16), b.astype(BF16), preferred_element_type=F32)


def _dot_nt(a, b):
    return lax.dot_general(a.astype(BF16), b.astype(BF16), (((1,), (1,)), ((), ())),
                           preferred_element_type=F32)


def _split2(a):
    hi = a.astype(BF16)
    lo = (a - hi.astype(F32)).astype(BF16)
    return hi, lo


def _dot_multi(a, b, passes):
    if passes == 1:
        return _dot(a, b)
    ah, al = _split2(a)
    bh, bl = _split2(b)
    d = lambda x, y: jnp.dot(x, y, preferred_element_type=F32)
    return d(ah, bh) + (d(ah, bl) + d(al, bh))


def _dot_exact_lhs(m01, x):
    x1 = x.astype(BF16)
    r1 = x - x1.astype(F32)
    x2 = r1.astype(BF16)
    x3 = (r1 - x2.astype(F32)).astype(BF16)
    d = lambda y: jnp.dot(m01, y, preferred_element_type=F32)
    return d(x1) + (d(x2) + d(x3))


def _same_block(i, j, size):
    shift = size.bit_length() - 1
    assert size == 1 << shift
    return (i >> shift) == (j >> shift)


def _sigmoid(x):
    return 1.0 / (1.0 + jnp.exp(-x))


def _softplus(x):
    return jnp.maximum(x, 0.0) + jnp.log1p(jnp.exp(-jnp.abs(x)))


def _layer_norm(y, g, b):
    mu = jnp.mean(y, axis=-1, keepdims=True)
    yc = y - mu
    var = jnp.mean(yc * yc, axis=-1, keepdims=True)
    return yc * lax.rsqrt(var + LN_EPS) * g + b


def _in_proj_kernel(x_ref, w_ref, o_ref):
    o_ref[...] = jnp.dot(x_ref[...], w_ref[...], preferred_element_type=F32)


def _in_proj(x_bf, w_bf, tm, tn):
    t = x_bf.shape[0]
    return pl.pallas_call(
        _in_proj_kernel,
        grid=(t // tm, N_PROJ // tn),
        in_specs=[pl.BlockSpec((tm, D_MODEL), lambda i, j: (i, 0)),
                  pl.BlockSpec((D_MODEL, tn), lambda i, j: (0, j))],
        out_specs=pl.BlockSpec((tm, tn), lambda i, j: (i, j)),
        out_shape=jax.ShapeDtypeStruct((t, N_PROJ), F32),
        compiler_params=pltpu.CompilerParams(
            dimension_semantics=("arbitrary", "arbitrary"), vmem_limit_bytes=VMEM_LIMIT),
        name="in_proj",
    )(x_bf, w_bf)


def _causal_conv(u, w_ref, prev_fn):
    k = w_ref.shape[0]
    out = u * w_ref[k - 1:k, :]
    for j in range(1, k):
        out = out + prev_fn(j) * w_ref[k - 1 - j:k - j, :]
    return out


def _prompt_prev(u, tail8):
    row8 = lax.broadcasted_iota(jnp.int32, (8, u.shape[1]), 0)

    def prev(j):
        ru = pltpu.roll(u, j, 0)
        top = jnp.where(row8 >= j, ru[0:8], pltpu.roll(tail8, j, 0))
        return jnp.concatenate([top, ru[8:]], axis=0)
    return prev


def _sample_prev(u, st4):
    t = lax.broadcasted_iota(jnp.int32, u.shape, 0) & (SAMPLE_LEN - 1)

    def prev(j):
        return jnp.where(t >= j, pltpu.roll(u, j, 0), pltpu.roll(st4, GROUP - (SAMPLE_LEN - j), 0))
    return prev


def _tri_inverse(m, seq_len):
    row = lax.broadcasted_iota(jnp.int32, (GROUP, GROUP), 0)
    col = lax.broadcasted_iota(jnp.int32, (GROUP, GROUP), 1)
    eye = (row == col).astype(F32)
    base = min(8, seq_len)
    mm = functools.partial(_dot_multi, passes=INV_PASSES)
    same = lambda b: _same_block(row, col, b)
    d = jnp.where(same(base), m, 0.0)
    d2 = mm(d, d)
    x = eye - d
    inv = x + mm(x, d2)
    if base > 4:
        inv = inv + mm(inv, mm(d2, d2))
    b = base * 2
    while b <= seq_len:
        c = jnp.where(same(b) & jnp.logical_not(same(b // 2)), m, 0.0)
        inv = inv - mm(mm(inv, c), inv)
        b *= 2
    return inv


def _gdn_group(q, k, v, z, logit, a_row, dt_row, nw_row, seq_len, state_fn, og_ref):
    row = lax.broadcasted_iota(jnp.int32, (GROUP, GROUP), 0)
    col = lax.broadcasted_iota(jnp.int32, (GROUP, GROUP), 1)
    same_seq = _same_block(row, col, seq_len)
    causal = same_seq & (row >= col)
    strict = same_seq & (row > col)

    beta = _sigmoid(logit)
    g = -jnp.exp(a_row) * _softplus(logit + dt_row)
    lmat = jnp.concatenate([causal.astype(F32), same_seq.astype(F32)], axis=0).astype(BF16)
    gc2 = _dot_exact_lhs(lmat, g)
    gcum, gtot = gc2[:GROUP], gc2[GROUP:]
    gcum_t = gcum.T
    e_cum = jnp.exp(gcum)
    e_rem = jnp.exp(gtot - gcum)
    e_tot = jnp.exp(gtot)

    for h in range(N_HEADS):
        sl = slice(h * HEAD_DIM, (h + 1) * HEAD_DIM)
        la = LOGIT_A + h
        qh, kh, vh, zh = q[:, sl], k[:, sl], v[:, sl], z[:, sl]
        qn = qh * (lax.rsqrt(jnp.sum(qh * qh, axis=-1, keepdims=True) + NORM_EPS) * (HEAD_DIM ** -0.5))
        kn = kh * lax.rsqrt(jnp.sum(kh * kh, axis=-1, keepdims=True) + NORM_EPS)
        bcol = beta[:, LOGIT_B + h:LOGIT_B + h + 1]
        kb = kn * bcol
        a = _dot_nt(jnp.concatenate([kb, qn], axis=0), kn)
        diff = gcum[:, la:la + 1] - gcum_t[la:la + 1, :]
        dec = jnp.exp(jnp.where(causal, diff, 0.0))
        m = jnp.where(strict, a[:GROUP] * dec, 0.0)
        attn = jnp.where(causal, a[GROUP:] * dec, 0.0)
        t_inv = _tri_inverse(m, seq_len)
        ecol = e_cum[:, la:la + 1]
        uw = _dot(t_inv, jnp.concatenate([vh * bcol, kb * ecol], axis=1))
        u, w = uw[:, :HEAD_DIM], uw[:, HEAD_DIM:]
        kdec = kn * e_rem[:, la:la + 1]
        qdec = qn * ecol
        o = state_fn(h, u, w, qdec, kdec, attn, e_tot[:, la:la + 1])
        o = o * lax.rsqrt(jnp.mean(o * o, axis=-1, keepdims=True) + NORM_EPS) * nw_row
        og_ref[:, sl] = (o * (zh * _sigmoid(zh))).astype(og_ref.dtype)


def _mixer_common(abc_ref, qkv_ref, z_ref, logit_ref, caw_ref, gcw_ref, par_ref, prev_a, prev_g,
                  seq_len, state_fn, ya_ref, og_ref):
    gb = abc_ref[:, COL_GB:COL_GB + D_CONV]
    gc = abc_ref[:, COL_GC:COL_GC + D_CONV]
    hv = abc_ref[:, COL_H:COL_H + D_CONV]
    u = gc * hv
    conv_a = _causal_conv(u, caw_ref, prev_a(u))
    ya_ref[...] = (gb * conv_a).astype(ya_ref.dtype)

    raw = qkv_ref[...]
    cg = _causal_conv(raw, gcw_ref, prev_g(raw))
    act = cg * _sigmoid(cg)
    q, k, v = act[:, :D_GDN], act[:, D_GDN:2 * D_GDN], act[:, 2 * D_GDN:]
    _gdn_group(q, k, v, z_ref[...], logit_ref[...], par_ref[0:1, :], par_ref[1:2, :], par_ref[2:3, :],
               seq_len, state_fn, og_ref)
    return u, raw


def _prompt_mixer_kernel(abc_ref, qkv_ref, z_ref, logit_ref, caw_ref, gcw_ref, par_ref,
                         ya_ref, og_ref, utail_ref, gtail_ref, s_ref):
    @pl.when(pl.program_id(1) == 0)
    def _():
        utail_ref[...] = jnp.zeros_like(utail_ref)
        gtail_ref[...] = jnp.zeros_like(gtail_ref)
        s_ref[...] = jnp.zeros_like(s_ref)

    def state_fn(h, u, w, qdec, kdec, attn, e_tot_col):
        s = s_ref[0, h]
        ws = _dot(jnp.concatenate([w, qdec], axis=0), s)
        v_new = u - ws[:GROUP]
        o = ws[GROUP:] + _dot(attn, v_new)
        s_ref[0, h] = s * e_tot_col[0:1, :] + _dot(kdec.T, v_new)
        return o

    u, raw = _mixer_common(
        abc_ref, qkv_ref, z_ref, logit_ref, caw_ref, gcw_ref, par_ref,
        lambda u: _prompt_prev(u, utail_ref[0]), lambda r: _prompt_prev(r, gtail_ref[0]),
        GROUP, state_fn, ya_ref, og_ref)
    utail_ref[0] = u[GROUP - 8:]
    gtail_ref[0] = raw[GROUP - 8:]


def _sample_mixer_kernel(abc_ref, qkv_ref, z_ref, logit_ref, caw_ref, gcw_ref, par_ref,
                         sta_ref, stg_ref, s0_ref, ya_in, og_in,
                         ya_ref, og_ref, u_ref, s_ref):
    del ya_in, og_in
    row8 = lax.broadcasted_iota(jnp.int32, (8, HEAD_DIM), 0)
    first = row8 < SAMPLE_LEN
    rowg = lax.broadcasted_iota(jnp.int32, (GROUP, HEAD_DIM), 0)

    def state_fn(h, u, w, qdec, kdec, attn, e_tot_col):
        ws_w, ws_q = [], []
        for p in range(GROUP // 8):
            lhs = jnp.concatenate([w[8 * p:8 * p + 8], qdec[8 * p:8 * p + 8]], axis=0)
            r0 = _dot(lhs, s0_ref[2 * p, h])
            r1 = _dot(lhs, s0_ref[2 * p + 1, h])
            ws_w.append(jnp.where(first, r0[:8], r1[:8]))
            ws_q.append(jnp.where(first, r0[8:], r1[8:]))
        v_new = u - jnp.concatenate(ws_w, axis=0)
        o = jnp.concatenate(ws_q, axis=0) + _dot(attn, v_new)
        kdec_t = kdec.T
        for s in range(SEQ_PER_GROUP):
            in_seq = _same_block(rowg, SAMPLE_LEN * s, SAMPLE_LEN)
            kv = _dot(kdec_t, jnp.where(in_seq, v_new, 0.0))
            s_ref[s, h] = s0_ref[s, h] * e_tot_col[SAMPLE_LEN * s:SAMPLE_LEN * s + 1, :] + kv
        return o

    u, _ = _mixer_common(
        abc_ref, qkv_ref, z_ref, logit_ref, caw_ref, gcw_ref, par_ref,
        lambda u: _sample_prev(u, sta_ref[...]), lambda r: _sample_prev(r, stg_ref[...]),
        SAMPLE_LEN, state_fn, ya_ref, og_ref)
    u_ref[...] = u


def _mixer_in_specs(row_block):
    return [
        pl.BlockSpec((GROUP, 3 * D_CONV), lambda *i: (row_block(*i), 0)),
        pl.BlockSpec((GROUP, 3 * D_GDN), lambda *i: (row_block(*i), COL_QKV // (3 * D_GDN))),
        pl.BlockSpec((GROUP, D_GDN), lambda *i: (row_block(*i), COL_Z // D_GDN)),
        pl.BlockSpec((GROUP, LANE), lambda *i: (row_block(*i), COL_LOGIT // LANE)),
        pl.BlockSpec((3, D_CONV), lambda *i: (0, 0)),
        pl.BlockSpec((4, 3 * D_GDN), lambda *i: (0, 0)),
        pl.BlockSpec((8, LANE), lambda *i: (0, 0)),
    ]


def _prompt_mixer(proj, caw, gcw, par, n_tok, batch, seq):
    chunks = seq // GROUP
    rb = lambda b, c: b * chunks + c
    return pl.pallas_call(
        _prompt_mixer_kernel,
        grid=(batch, chunks),
        in_specs=_mixer_in_specs(rb),
        out_specs=[
            pl.BlockSpec((GROUP, D_CONV), lambda b, c: (rb(b, c), 0)),
            pl.BlockSpec((GROUP, D_GDN), lambda b, c: (rb(b, c), 0)),
            pl.BlockSpec((1, 8, D_CONV), lambda b, c: (b, 0, 0)),
            pl.BlockSpec((1, 8, 3 * D_GDN), lambda b, c: (b, 0, 0)),
            pl.BlockSpec((1, N_HEADS, HEAD_DIM, HEAD_DIM), lambda b, c: (b, 0, 0, 0)),
        ],
        out_shape=[
            jax.ShapeDtypeStruct((n_tok, D_CONV), BF16),
            jax.ShapeDtypeStruct((n_tok, D_GDN), BF16),
            jax.ShapeDtypeStruct((batch, 8, D_CONV), F32),
            jax.ShapeDtypeStruct((batch, 8, 3 * D_GDN), F32),
            jax.ShapeDtypeStruct((batch, N_HEADS, HEAD_DIM, HEAD_DIM), F32),
        ],
        compiler_params=pltpu.CompilerParams(
            dimension_semantics=("arbitrary", "arbitrary"), vmem_limit_bytes=VMEM_LIMIT),
        name="prompt_mixer",
    )(proj, proj, proj, proj, caw, gcw, par)


def _sample_mixer(proj, caw, gcw, par, sta, stg, s0, ya, og, first_block):
    n_seq = s0.shape[0]
    rb = lambda g: first_block + g
    any_spec = pl.BlockSpec(memory_space=pl.ANY)
    return pl.pallas_call(
        _sample_mixer_kernel,
        grid=(n_seq // SEQ_PER_GROUP,),
        in_specs=_mixer_in_specs(rb) + [
            pl.BlockSpec((GROUP, D_CONV), lambda g: (g, 0)),
            pl.BlockSpec((GROUP, 3 * D_GDN), lambda g: (g, 0)),
            pl.BlockSpec((SEQ_PER_GROUP, N_HEADS, HEAD_DIM, HEAD_DIM), lambda g: (g, 0, 0, 0)),
            any_spec, any_spec,
        ],
        out_specs=[
            pl.BlockSpec((GROUP, D_CONV), lambda g: (rb(g), 0)),
            pl.BlockSpec((GROUP, D_GDN), lambda g: (rb(g), 0)),
            pl.BlockSpec((GROUP, D_CONV), lambda g: (g, 0)),
            pl.BlockSpec((SEQ_PER_GROUP, N_HEADS, HEAD_DIM, HEAD_DIM), lambda g: (g, 0, 0, 0)),
        ],
        out_shape=[
            jax.ShapeDtypeStruct(ya.shape, ya.dtype),
            jax.ShapeDtypeStruct(og.shape, og.dtype),
            jax.ShapeDtypeStruct((n_seq * SAMPLE_LEN, D_CONV), F32),
            jax.ShapeDtypeStruct(s0.shape, F32),
        ],
        input_output_aliases={10: 0, 11: 1},
        compiler_params=pltpu.CompilerParams(
            dimension_semantics=("arbitrary",), vmem_limit_bytes=VMEM_LIMIT),
        name="sample_mixer",
    )(proj, proj, proj, proj, caw, gcw, par, sta, stg, s0, ya, og)


def _out_proj_kernel(ya_ref, og_ref, ma_ref, mb_ref, x_ref, wa_ref, wb_ref, wo_ref, g_ref, b_ref, o_ref):
    y_a = jnp.dot(ya_ref[...], wa_ref[...], preferred_element_type=F32)
    y_b = jnp.dot(og_ref[...], wb_ref[...], preferred_element_type=F32)
    merged = _sigmoid(ma_ref[...]) * y_a + _sigmoid(mb_ref[...]) * y_b
    y = ALPHA * x_ref[...] + jnp.dot(merged.astype(BF16), wo_ref[...], preferred_element_type=F32)
    o_ref[...] = _layer_norm(y, g_ref[...], b_ref[...])


def _out_proj(ya, og, proj, x, wa, wb, wo, g, b, tm):
    t = x.shape[0]
    const = lambda shape: pl.BlockSpec(shape, lambda i: (0, 0), pipeline_mode=pl.Buffered(1))
    return pl.pallas_call(
        _out_proj_kernel,
        grid=(t // tm,),
        in_specs=[
            pl.BlockSpec((tm, D_CONV), lambda i: (i, 0)),
            pl.BlockSpec((tm, D_GDN), lambda i: (i, 0)),
            pl.BlockSpec((tm, D_MODEL), lambda i: (i, COL_MA // D_MODEL)),
            pl.BlockSpec((tm, D_MODEL), lambda i: (i, COL_MB // D_MODEL)),
            pl.BlockSpec((tm, D_MODEL), lambda i: (i, 0)),
            const((D_CONV, D_MODEL)), const((D_GDN, D_MODEL)), const((D_MODEL, D_MODEL)),
            const((1, D_MODEL)), const((1, D_MODEL)),
        ],
        out_specs=pl.BlockSpec((tm, D_MODEL), lambda i: (i, 0)),
        out_shape=jax.ShapeDtypeStruct((t, D_MODEL), F32),
        compiler_params=pltpu.CompilerParams(
            dimension_semantics=("arbitrary",), vmem_limit_bytes=VMEM_LIMIT),
        name="out_proj",
    )(ya, og, proj, proj, x, wa, wb, wo, g, b)


def _mlp_kernel(x_ref, wu_ref, wd_ref, g_ref, b_ref, o_ref, obf_ref, xb_ref, acc_ref):
    j = pl.program_id(1)

    @pl.when(j == 0)
    def _():
        xb_ref[...] = x_ref[...].astype(BF16)

    hid = jnp.maximum(jnp.dot(xb_ref[...], wu_ref[...], preferred_element_type=F32), 0.0)
    part = jnp.dot((hid * hid).astype(BF16), wd_ref[...], preferred_element_type=F32)

    @pl.when(j == 0)
    def _():
        acc_ref[...] = part

    @pl.when(j > 0)
    def _():
        acc_ref[...] += part

    @pl.when(j == pl.num_programs(1) - 1)
    def _():
        y = _layer_norm(ALPHA * x_ref[...] + acc_ref[...], g_ref[...], b_ref[...])
        o_ref[...] = y
        obf_ref[...] = y.astype(BF16)


def _mlp(x, wu, wd, g, b, tm, tf):
    t = x.shape[0]
    return pl.pallas_call(
        _mlp_kernel,
        grid=(t // tm, D_FF // tf),
        in_specs=[
            pl.BlockSpec((tm, D_MODEL), lambda i, j: (i, 0)),
            pl.BlockSpec((D_MODEL, tf), lambda i, j: (0, j)),
            pl.BlockSpec((tf, D_MODEL), lambda i, j: (j, 0)),
            pl.BlockSpec((1, D_MODEL), lambda i, j: (0, 0)),
            pl.BlockSpec((1, D_MODEL), lambda i, j: (0, 0)),
        ],
        out_specs=[pl.BlockSpec((tm, D_MODEL), lambda i, j: (i, 0)),
                   pl.BlockSpec((tm, D_MODEL), lambda i, j: (i, 0))],
        out_shape=[jax.ShapeDtypeStruct((t, D_MODEL), F32), jax.ShapeDtypeStruct((t, D_MODEL), BF16)],
        scratch_shapes=[pltpu.VMEM((tm, D_MODEL), BF16), pltpu.VMEM((tm, D_MODEL), F32)],
        compiler_params=pltpu.CompilerParams(
            dimension_semantics=("arbitrary", "arbitrary"), vmem_limit_bytes=VMEM_LIMIT),
        name="mlp",
    )(x, wu, wd, g, b)


def _regroup_w_in(w):
    o = 0
    parts = {}
    for name, width in (("abc", 3 * D_CONV), ("qkv", 3 * D_GDN), ("z", D_GDN), ("logit", 2 * N_HEADS),
                        ("ma", D_MODEL), ("mb", D_MODEL)):
        parts[name] = w[:, o:o + width]
        o += width
    pad = jnp.zeros((w.shape[0], N_PROJ - COL_LOGIT - 2 * N_HEADS), w.dtype)
    return jnp.concatenate(
        [parts["abc"], parts["qkv"], parts["ma"], parts["mb"], parts["z"], parts["logit"], pad],
        axis=1).astype(BF16)


def _lane_row(vec, offset):
    return jnp.zeros((LANE,), F32).at[offset:offset + vec.shape[0]].set(vec.astype(F32))


def _front_pad_state(st):
    n, k1, c = st.shape
    return jnp.concatenate([jnp.zeros((n, SAMPLE_LEN - k1, c), st.dtype), st], axis=1).reshape(n * SAMPLE_LEN, c)


def kernel(x_prompt, x_sample, state_conv_a, state_gdn_conv, state_gdn, w_in, conv_a_w, gdn_conv_w, a_log,
           dt_bias, gdn_norm_w, w_a_out, w_b_out, w_o, ln1_g, ln1_b, w_up, w_down, ln2_g, ln2_b):
    batch, seq, _ = x_prompt.shape
    n_seq = x_sample.shape[0]
    n_prompt = batch * seq
    n_tok = n_prompt + n_seq * SAMPLE_LEN
    assert x_sample.shape[1] == SAMPLE_LEN and seq % GROUP == 0 and n_seq % SEQ_PER_GROUP == 0

    x = jnp.concatenate([x_prompt.reshape(n_prompt, D_MODEL), x_sample.reshape(-1, D_MODEL)], axis=0)
    x_bf = x.astype(BF16)

    ca_p, cg_p, sg_p, ca_s, cg_s, sg_s = [], [], [], [], [], []
    for l in range(DEPTH):
        proj = _in_proj(x_bf, _regroup_w_in(w_in[l]), tm=1088, tn=1152)

        par = jnp.zeros((8, LANE), F32)
        par = par.at[0].set(_lane_row(a_log[l], LOGIT_A))
        par = par.at[1].set(_lane_row(dt_bias[l], LOGIT_A))
        par = par.at[2].set(gdn_norm_w[l].astype(F32))

        ya, og, utail, gtail, s_p = _prompt_mixer(proj, conv_a_w[l], gdn_conv_w[l], par, n_tok, batch, seq)
        ya, og, u_s, s_s = _sample_mixer(
            proj, conv_a_w[l], gdn_conv_w[l], par,
            _front_pad_state(state_conv_a[l]), _front_pad_state(state_gdn_conv[l]), state_gdn[l],
            ya, og, n_prompt // GROUP)

        x1 = _out_proj(ya, og, proj, x, w_a_out[l].astype(BF16), w_b_out[l].astype(BF16), w_o[l].astype(BF16),
                       ln1_g[l].reshape(1, D_MODEL), ln1_b[l].reshape(1, D_MODEL), tm=256)
        x, x_bf = _mlp(x1, w_up[l].astype(BF16), w_down[l].astype(BF16),
                       ln2_g[l].reshape(1, D_MODEL), ln2_b[l].reshape(1, D_MODEL), tm=512, tf=1024)

        ca_p.append(utail[:, 6:8])
        cg_p.append(gtail[:, 5:8])
        sg_p.append(s_p)
        ca_s.append(u_s.reshape(n_seq, SAMPLE_LEN, D_CONV)[:, 2:4])
        cg_s.append(proj[n_prompt:, COL_QKV:COL_QKV + 3 * D_GDN].reshape(n_seq, SAMPLE_LEN, 3 * D_GDN)[:, 1:4])
        sg_s.append(s_s)

    y_prompt = x[:n_prompt].reshape(batch, seq, D_MODEL)
    y_sample = x[n_prompt:].reshape(n_seq, SAMPLE_LEN, D_MODEL)
    return (y_prompt, y_sample, jnp.stack(ca_p), jnp.stack(cg_p), jnp.stack(sg_p),
            jnp.stack(ca_s), jnp.stack(cg_s), jnp.stack(sg_s))
```

```python
import functools

import jax
import jax.numpy as jnp
from jax import lax
from jax.experimental import pallas as pl
from jax.experimental.pallas import tpu as pltpu

F32 = jnp.float32
BF16 = jnp.bfloat16

D_MODEL = 2048
DEPTH = 2
D_CONV = D_MODEL // 2
N_HEADS = 8
HEAD_DIM = 128
D_GDN = N_HEADS * HEAD_DIM
D_FF = 4 * D_MODEL
ALPHA = (2 * DEPTH) ** 0.25
LN_EPS = 1e-5
NORM_EPS = 1e-6

COL_GB, COL_GC, COL_H = 0, D_CONV, 2 * D_CONV
COL_QKV = 3 * D_CONV
COL_MA = COL_QKV + 3 * D_GDN
COL_MB = COL_MA + D_MODEL
COL_Z = COL_MB + D_MODEL
COL_LOGIT = COL_Z + D_GDN
LANE = 128
N_PROJ = COL_LOGIT + 2 * LANE
LOGIT_B, LOGIT_A = 0, N_HEADS

GROUP = 64
SAMPLE_LEN = 4
SEQ_PER_GROUP = GROUP // SAMPLE_LEN
INV_PASSES = 3

VMEM_LIMIT = 56 * 1024 * 1024


def _dot(a, b):
    return jnp.dot(a.astype(BF16), b.astype(BF16), preferred_element_type=F32)


def _dot_nt(a, b):
    return lax.dot_general(a.astype(BF16), b.astype(BF16), (((1,), (1,)), ((), ())),
                           preferred_element_type=F32)


def _split2(a):
    hi = a.astype(BF16)
    lo = (a - hi.astype(F32)).astype(BF16)
    return hi, lo


def _dot_multi(a, b, passes):
    if passes == 1:
        return _dot(a, b)
    ah, al = _split2(a)
    bh, bl = _split2(b)
    d = lambda x, y: jnp.dot(x, y, preferred_element_type=F32)
    return d(ah, bh) + (d(ah, bl) + d(al, bh))


def _dot_exact_lhs(m01, x):
    x1 = x.astype(BF16)
    r1 = x - x1.astype(F32)
    x2 = r1.astype(BF16)
    x3 = (r1 - x2.astype(F32)).astype(BF16)
    d = lambda y: jnp.dot(m01, y, preferred_element_type=F32)
    return d(x1) + (d(x2) + d(x3))


def _same_block(i, j, size):
    shift = size.bit_length() - 1
    assert size == 1 << shift
    return (i >> shift) == (j >> shift)


def _sigmoid(x):
    return 1.0 / (1.0 + jnp.exp(-x))


def _softplus(x):
    return jnp.maximum(x, 0.0) + jnp.log1p(jnp.exp(-jnp.abs(x)))


def _layer_norm(y, g, b):
    mu = jnp.mean(y, axis=-1, keepdims=True)
    yc = y - mu
    var = jnp.mean(yc * yc, axis=-1, keepdims=True)
    return yc * lax.rsqrt(var + LN_EPS) * g + b


def _in_proj_kernel(x_ref, w_ref, o_ref):
    o_ref[...] = jnp.dot(x_ref[...], w_ref[...], preferred_element_type=F32)


def _in_proj(x_bf, w_bf, tm, tn):
    t = x_bf.shape[0]
    return pl.pallas_call(
        _in_proj_kernel,
        grid=(t // tm, N_PROJ // tn),
        in_specs=[pl.BlockSpec((tm, D_MODEL), lambda i, j: (i, 0)),
                  pl.BlockSpec((D_MODEL, tn), lambda i, j: (0, j))],
        out_specs=pl.BlockSpec((tm, tn), lambda i, j: (i, j)),
        out_shape=jax.ShapeDtypeStruct((t, N_PROJ), F32),
        compiler_params=pltpu.CompilerParams(
            dimension_semantics=("arbitrary", "arbitrary"), vmem_limit_bytes=VMEM_LIMIT),
        name="in_proj",
    )(x_bf, w_bf)


def _causal_conv(u, w_ref, prev_fn):
    k = w_ref.shape[0]
    out = u * w_ref[k - 1:k, :]
    for j in range(1, k):
        out = out + prev_fn(j) * w_ref[k - 1 - j:k - j, :]
    return out


def _prompt_prev(u, tail8):
    row8 = lax.broadcasted_iota(jnp.int32, (8, u.shape[1]), 0)

    def prev(j):
        ru = pltpu.roll(u, j, 0)
        top = jnp.where(row8 >= j, ru[0:8], pltpu.roll(tail8, j, 0))
        return jnp.concatenate([top, ru[8:]], axis=0)
    return prev


def _sample_prev(u, st4):
    t = lax.broadcasted_iota(jnp.int32, u.shape, 0) & (SAMPLE_LEN - 1)

    def prev(j):
        return jnp.where(t >= j, pltpu.roll(u, j, 0), pltpu.roll(st4, GROUP - (SAMPLE_LEN - j), 0))
    return prev


def _pair_iotas():
    row = lax.broadcasted_iota(jnp.int32, (GROUP, 2 * GROUP), 0)
    lane = lax.broadcasted_iota(jnp.int32, (GROUP, 2 * GROUP), 1)
    return row, lane & (GROUP - 1), lane >= GROUP


def _block_diag(y0, y1):
    z = jnp.zeros_like(y0)
    return jnp.concatenate([jnp.concatenate([y0, z], axis=1), jnp.concatenate([z, y1], axis=1)], axis=0)


def _pair_block_diag(y):
    _, _, second = _pair_iotas()
    return jnp.concatenate([jnp.where(second, 0.0, y), jnp.where(second, y, 0.0)], axis=0)


def _tri_inverse_pairs(ms, seq_len):
    row, col, _ = _pair_iotas()
    eye = (row == col).astype(F32)
    base = min(8, seq_len)
    mm = lambda x, y: _dot_multi(x, _pair_block_diag(y), INV_PASSES)
    same = lambda b: _same_block(row, col, b)
    d = [jnp.where(same(base), m, 0.0) for m in ms]
    d2 = [mm(x, x) for x in d]
    xs = [eye - x for x in d]
    if base > 4:
        d4 = [mm(y, y) for y in d2]
    inv = [x + mm(x, y) for x, y in zip(xs, d2)]
    if base > 4:
        inv = [i + mm(i, y) for i, y in zip(inv, d4)]
    b = base * 2
    while b <= seq_len:
        off_diag = same(b) & jnp.logical_not(same(b // 2))
        t = [mm(i, jnp.where(off_diag, m, 0.0)) for i, m in zip(inv, ms)]
        inv = [i - mm(x, i) for i, x in zip(inv, t)]
        b *= 2
    return inv


def _gdn_group(q, k, v, z, logit, a_row, dt_row, nw_row, seq_len, state, og_ref):
    row = lax.broadcasted_iota(jnp.int32, (GROUP, GROUP), 0)
    col = lax.broadcasted_iota(jnp.int32, (GROUP, GROUP), 1)
    same_seq = _same_block(row, col, seq_len)
    lmat = jnp.concatenate([(same_seq & (row >= col)).astype(F32), same_seq.astype(F32)], axis=0).astype(BF16)
    prow, pcol, second = _pair_iotas()
    psame = _same_block(prow, pcol, seq_len)
    causal = psame & (prow >= pcol)
    strict = psame & (prow > pcol)

    beta = _sigmoid(logit)
    g = -jnp.exp(a_row) * _softplus(logit + dt_row)
    gc2 = _dot_exact_lhs(lmat, g)
    gcum, gtot = gc2[:GROUP], gc2[GROUP:]
    gcum_t2 = jnp.concatenate([gcum, gcum], axis=0).T
    e_cum = jnp.exp(gcum)
    e_rem = jnp.exp(gtot - gcum)
    e_tot = jnp.exp(gtot)

    heads = range(N_HEADS)
    pairs = range(0, N_HEADS, 2)
    hs = lambda x, h: x[:, h * HEAD_DIM:(h + 1) * HEAD_DIM]
    lcol = lambda x, h: x[:, LOGIT_A + h:LOGIT_A + h + 1]
    qn = [hs(q, h) * (lax.rsqrt(jnp.sum(hs(q, h) * hs(q, h), axis=-1, keepdims=True) + NORM_EPS)
                      * (HEAD_DIM ** -0.5)) for h in heads]
    kn = [hs(k, h) * lax.rsqrt(jnp.sum(hs(k, h) * hs(k, h), axis=-1, keepdims=True) + NORM_EPS) for h in heads]
    bcol = [beta[:, LOGIT_B + h:LOGIT_B + h + 1] for h in heads]
    kb = [kn[h] * bcol[h] for h in heads]

    a = [_dot_nt(jnp.concatenate([jnp.concatenate([kb[h], kb[h + 1]], axis=1),
                                  jnp.concatenate([qn[h], qn[h + 1]], axis=1)], axis=0),
                 _block_diag(kn[h], kn[h + 1])) for h in pairs]
    ms, attn = [], []
    for i, h in enumerate(pairs):
        gcol = jnp.where(second, lcol(gcum, h + 1), lcol(gcum, h))
        grow = jnp.where(second[0:1], gcum_t2[LOGIT_A + h + 1:LOGIT_A + h + 2, :],
                         gcum_t2[LOGIT_A + h:LOGIT_A + h + 1, :])
        dec = jnp.exp(jnp.where(causal, gcol - grow, 0.0))
        ms.append(jnp.where(strict, a[i][:GROUP] * dec, 0.0))
        attn.append(jnp.where(causal, a[i][GROUP:] * dec, 0.0))
    t_inv = _tri_inverse_pairs(ms, seq_len)

    vb = [hs(v, h) * bcol[h] for h in heads]
    kbe = [kb[h] * lcol(e_cum, h) for h in heads]
    uw = [_dot(t_inv[i], _block_diag(jnp.concatenate([vb[h], kbe[h]], axis=1),
                                     jnp.concatenate([vb[h + 1], kbe[h + 1]], axis=1)))
          for i, h in enumerate(pairs)]
    u = [uw[h // 2][:, (h % 2) * 2 * HEAD_DIM:(h % 2) * 2 * HEAD_DIM + HEAD_DIM] for h in heads]
    w = [uw[h // 2][:, (h % 2) * 2 * HEAD_DIM + HEAD_DIM:(h % 2 + 1) * 2 * HEAD_DIM] for h in heads]
    qdec = [qn[h] * lcol(e_cum, h) for h in heads]
    kdec = [kn[h] * lcol(e_rem, h) for h in heads]

    ws = [state.read(h, jnp.concatenate([w[h], qdec[h]], axis=0)) for h in heads]
    v_new = [u[h] - ws[h][0] for h in heads]
    vn_bd = [_block_diag(v_new[h], v_new[h + 1]) for h in pairs]
    o_pair = [jnp.concatenate([ws[h][1], ws[h + 1][1]], axis=1) + _dot(attn[i], vn_bd[i])
              for i, h in enumerate(pairs)]
    for i, h in enumerate(pairs):
        state.update(h, jnp.concatenate([kdec[h], kdec[h + 1]], axis=0).T, vn_bd[i],
                     lcol(e_tot, h), lcol(e_tot, h + 1))
    for h in heads:
        o = hs(o_pair[h // 2], h % 2)
        o = o * lax.rsqrt(jnp.mean(o * o, axis=-1, keepdims=True) + NORM_EPS) * nw_row
        zh = hs(z, h)
        og_ref[:, h * HEAD_DIM:(h + 1) * HEAD_DIM] = (o * (zh * _sigmoid(zh))).astype(og_ref.dtype)


def _mixer_common(abc_ref, qkv_ref, z_ref, logit_ref, caw_ref, gcw_ref, par_ref, prev_a, prev_g,
                  seq_len, state_fn, ya_ref, og_ref):
    gb = abc_ref[:, COL_GB:COL_GB + D_CONV]
    gc = abc_ref[:, COL_GC:COL_GC + D_CONV]
    hv = abc_ref[:, COL_H:COL_H + D_CONV]
    u = gc * hv
    conv_a = _causal_conv(u, caw_ref, prev_a(u))
    ya_ref[...] = (gb * conv_a).astype(ya_ref.dtype)

    raw = qkv_ref[...]
    cg = _causal_conv(raw, gcw_ref, prev_g(raw))
    act = cg * _sigmoid(cg)
    q, k, v = act[:, :D_GDN], act[:, D_GDN:2 * D_GDN], act[:, 2 * D_GDN:]
    _gdn_group(q, k, v, z_ref[...], logit_ref[...], par_ref[0:1, :], par_ref[1:2, :], par_ref[2:3, :],
               seq_len, state_fn, og_ref)
    return u, raw


def _prompt_mixer_kernel(abc_ref, qkv_ref, z_ref, logit_ref, caw_ref, gcw_ref, par_ref,
                         ya_ref, og_ref, utail_ref, gtail_ref, s_ref):
    @pl.when(pl.program_id(1) == 0)
    def _():
        utail_ref[...] = jnp.zeros_like(utail_ref)
        gtail_ref[...] = jnp.zeros_like(gtail_ref)
        s_ref[...] = jnp.zeros_like(s_ref)

    class State:
        @staticmethod
        def read(h, lhs):
            ws = _dot(lhs, s_ref[0, h])
            return ws[:GROUP], ws[GROUP:]

        @staticmethod
        def update(h, kd_t, vn_bd, e0, e1):
            kv = _dot(kd_t, vn_bd)
            s_ref[0, h] = s_ref[0, h] * e0[0:1, :] + kv[:, :HEAD_DIM]
            s_ref[0, h + 1] = s_ref[0, h + 1] * e1[0:1, :] + kv[:, HEAD_DIM:]

    u, raw = _mixer_common(
        abc_ref, qkv_ref, z_ref, logit_ref, caw_ref, gcw_ref, par_ref,
        lambda u: _prompt_prev(u, utail_ref[0]), lambda r: _prompt_prev(r, gtail_ref[0]),
        GROUP, State, ya_ref, og_ref)
    utail_ref[0] = u[GROUP - 8:]
    gtail_ref[0] = raw[GROUP - 8:]


def _sample_mixer_kernel(abc_ref, qkv_ref, z_ref, logit_ref, caw_ref, gcw_ref, par_ref,
                         sta_ref, stg_ref, s0_ref, ya_in, og_in,
                         ya_ref, og_ref, u_ref, s_ref):
    del ya_in, og_in
    row8 = lax.broadcasted_iota(jnp.int32, (8, HEAD_DIM), 0)
    first = row8 < SAMPLE_LEN
    row_bd = lax.broadcasted_iota(jnp.int32, (2 * GROUP, 2 * HEAD_DIM), 0) & (GROUP - 1)

    class State:
        @staticmethod
        def read(h, lhs):
            w, qdec = lhs[:GROUP], lhs[GROUP:]
            ws_w, ws_q = [], []
            for p in range(GROUP // 8):
                slab = jnp.concatenate([w[8 * p:8 * p + 8], qdec[8 * p:8 * p + 8]], axis=0)
                r0 = _dot(slab, s0_ref[2 * p, h])
                r1 = _dot(slab, s0_ref[2 * p + 1, h])
                ws_w.append(jnp.where(first, r0[:8], r1[:8]))
                ws_q.append(jnp.where(first, r0[8:], r1[8:]))
            return jnp.concatenate(ws_w, axis=0), jnp.concatenate(ws_q, axis=0)

        @staticmethod
        def update(h, kd_t, vn_bd, e0, e1):
            for s in range(SEQ_PER_GROUP):
                in_seq = _same_block(row_bd, SAMPLE_LEN * s, SAMPLE_LEN)
                kv = _dot(kd_t, jnp.where(in_seq, vn_bd, 0.0))
                r = SAMPLE_LEN * s
                s_ref[s, h] = s0_ref[s, h] * e0[r:r + 1, :] + kv[:, :HEAD_DIM]
                s_ref[s, h + 1] = s0_ref[s, h + 1] * e1[r:r + 1, :] + kv[:, HEAD_DIM:]

    u, _ = _mixer_common(
        abc_ref, qkv_ref, z_ref, logit_ref, caw_ref, gcw_ref, par_ref,
        lambda u: _sample_prev(u, sta_ref[...]), lambda r: _sample_prev(r, stg_ref[...]),
        SAMPLE_LEN, State, ya_ref, og_ref)
    u_ref[...] = u


def _mixer_in_specs(row_block):
    return [
        pl.BlockSpec((GROUP, 3 * D_CONV), lambda *i: (row_block(*i), 0)),
        pl.BlockSpec((GROUP, 3 * D_GDN), lambda *i: (row_block(*i), COL_QKV // (3 * D_GDN))),
        pl.BlockSpec((GROUP, D_GDN), lambda *i: (row_block(*i), COL_Z // D_GDN)),
        pl.BlockSpec((GROUP, LANE), lambda *i: (row_block(*i), COL_LOGIT // LANE)),
        pl.BlockSpec((3, D_CONV), lambda *i: (0, 0)),
        pl.BlockSpec((4, 3 * D_GDN), lambda *i: (0, 0)),
        pl.BlockSpec((8, LANE), lambda *i: (0, 0)),
    ]


def _prompt_mixer(proj, caw, gcw, par, n_tok, batch, seq):
    chunks = seq // GROUP
    rb = lambda b, c: b * chunks + c
    return pl.pallas_call(
        _prompt_mixer_kernel,
        grid=(batch, chunks),
        in_specs=_mixer_in_specs(rb),
        out_specs=[
            pl.BlockSpec((GROUP, D_CONV), lambda b, c: (rb(b, c), 0)),
            pl.BlockSpec((GROUP, D_GDN), lambda b, c: (rb(b, c), 0)),
            pl.BlockSpec((1, 8, D_CONV), lambda b, c: (b, 0, 0)),
            pl.BlockSpec((1, 8, 3 * D_GDN), lambda b, c: (b, 0, 0)),
            pl.BlockSpec((1, N_HEADS, HEAD_DIM, HEAD_DIM), lambda b, c: (b, 0, 0, 0)),
        ],
        out_shape=[
            jax.ShapeDtypeStruct((n_tok, D_CONV), BF16),
            jax.ShapeDtypeStruct((n_tok, D_GDN), BF16),
            jax.ShapeDtypeStruct((batch, 8, D_CONV), F32),
            jax.ShapeDtypeStruct((batch, 8, 3 * D_GDN), F32),
            jax.ShapeDtypeStruct((batch, N_HEADS, HEAD_DIM, HEAD_DIM), F32),
        ],
        compiler_params=pltpu.CompilerParams(
            dimension_semantics=("arbitrary", "arbitrary"), vmem_limit_bytes=VMEM_LIMIT),
        name="prompt_mixer",
    )(proj, proj, proj, proj, caw, gcw, par)


def _sample_mixer(proj, caw, gcw, par, sta, stg, s0, ya, og, first_block):
    n_seq = s0.shape[0]
    rb = lambda g: first_block + g
    any_spec = pl.BlockSpec(memory_space=pl.ANY)
    return pl.pallas_call(
        _sample_mixer_kernel,
        grid=(n_seq // SEQ_PER_GROUP,),
        in_specs=_mixer_in_specs(rb) + [
            pl.BlockSpec((GROUP, D_CONV), lambda g: (g, 0)),
            pl.BlockSpec((GROUP, 3 * D_GDN), lambda g: (g, 0)),
            pl.BlockSpec((SEQ_PER_GROUP, N_HEADS, HEAD_DIM, HEAD_DIM), lambda g: (g, 0, 0, 0)),
            any_spec, any_spec,
        ],
        out_specs=[
            pl.BlockSpec((GROUP, D_CONV), lambda g: (rb(g), 0)),
            pl.BlockSpec((GROUP, D_GDN), lambda g: (rb(g), 0)),
            pl.BlockSpec((GROUP, D_CONV), lambda g: (g, 0)),
            pl.BlockSpec((SEQ_PER_GROUP, N_HEADS, HEAD_DIM, HEAD_DIM), lambda g: (g, 0, 0, 0)),
        ],
        out_shape=[
            jax.ShapeDtypeStruct(ya.shape, ya.dtype),
            jax.ShapeDtypeStruct(og.shape, og.dtype),
            jax.ShapeDtypeStruct((n_seq * SAMPLE_LEN, D_CONV), F32),
            jax.ShapeDtypeStruct(s0.shape, F32),
        ],
        input_output_aliases={10: 0, 11: 1},
        compiler_params=pltpu.CompilerParams(
            dimension_semantics=("arbitrary",), vmem_limit_bytes=VMEM_LIMIT),
        name="sample_mixer",
    )(proj, proj, proj, proj, caw, gcw, par, sta, stg, s0, ya, og)


def _out_proj_kernel(ya_ref, og_ref, ma_ref, mb_ref, x_ref, wa_ref, wb_ref, wo_ref, g_ref, b_ref, o_ref):
    y_a = jnp.dot(ya_ref[...], wa_ref[...], preferred_element_type=F32)
    y_b = jnp.dot(og_ref[...], wb_ref[...], preferred_element_type=F32)
    merged = _sigmoid(ma_ref[...]) * y_a + _sigmoid(mb_ref[...]) * y_b
    y = ALPHA * x_ref[...] + jnp.dot(merged.astype(BF16), wo_ref[...], preferred_element_type=F32)
    o_ref[...] = _layer_norm(y, g_ref[...], b_ref[...])


def _out_proj(ya, og, proj, x, wa, wb, wo, g, b, tm):
    t = x.shape[0]
    const = lambda shape: pl.BlockSpec(shape, lambda i: (0, 0), pipeline_mode=pl.Buffered(1))
    return pl.pallas_call(
        _out_proj_kernel,
        grid=(t // tm,),
        in_specs=[
            pl.BlockSpec((tm, D_CONV), lambda i: (i, 0)),
            pl.BlockSpec((tm, D_GDN), lambda i: (i, 0)),
            pl.BlockSpec((tm, D_MODEL), lambda i: (i, COL_MA // D_MODEL)),
            pl.BlockSpec((tm, D_MODEL), lambda i: (i, COL_MB // D_MODEL)),
            pl.BlockSpec((tm, D_MODEL), lambda i: (i, 0)),
            const((D_CONV, D_MODEL)), const((D_GDN, D_MODEL)), const((D_MODEL, D_MODEL)),
            const((1, D_MODEL)), const((1, D_MODEL)),
        ],
        out_specs=pl.BlockSpec((tm, D_MODEL), lambda i: (i, 0)),
        out_shape=jax.ShapeDtypeStruct((t, D_MODEL), F32),
        compiler_params=pltpu.CompilerParams(
            dimension_semantics=("arbitrary",), vmem_limit_bytes=VMEM_LIMIT),
        name="out_proj",
    )(ya, og, proj, proj, x, wa, wb, wo, g, b)


def _mlp_kernel(x_ref, wu_ref, wd_ref, g_ref, b_ref, o_ref, obf_ref, xb_ref, acc_ref):
    j = pl.program_id(1)

    @pl.when(j == 0)
    def _():
        xb_ref[...] = x_ref[...].astype(BF16)

    hid = jnp.maximum(jnp.dot(xb_ref[...], wu_ref[...], preferred_element_type=F32), 0.0)
    part = jnp.dot((hid * hid).astype(BF16), wd_ref[...], preferred_element_type=F32)

    @pl.when(j == 0)
    def _():
        acc_ref[...] = part

    @pl.when(j > 0)
    def _():
        acc_ref[...] += part

    @pl.when(j == pl.num_programs(1) - 1)
    def _():
        y = _layer_norm(ALPHA * x_ref[...] + acc_ref[...], g_ref[...], b_ref[...])
        o_ref[...] = y
        obf_ref[...] = y.astype(BF16)


def _mlp(x, wu, wd, g, b, tm, tf):
    t = x.shape[0]
    return pl.pallas_call(
        _mlp_kernel,
        grid=(t // tm, D_FF // tf),
        in_specs=[
            pl.BlockSpec((tm, D_MODEL), lambda i, j: (i, 0)),
            pl.BlockSpec((D_MODEL, tf), lambda i, j: (0, j)),
            pl.BlockSpec((tf, D_MODEL), lambda i, j: (j, 0)),
            pl.BlockSpec((1, D_MODEL), lambda i, j: (0, 0)),
            pl.BlockSpec((1, D_MODEL), lambda i, j: (0, 0)),
        ],
        out_specs=[pl.BlockSpec((tm, D_MODEL), lambda i, j: (i, 0)),
                   pl.BlockSpec((tm, D_MODEL), lambda i, j: (i, 0))],
        out_shape=[jax.ShapeDtypeStruct((t, D_MODEL), F32), jax.ShapeDtypeStruct((t, D_MODEL), BF16)],
        scratch_shapes=[pltpu.VMEM((tm, D_MODEL), BF16), pltpu.VMEM((tm, D_MODEL), F32)],
        compiler_params=pltpu.CompilerParams(
            dimension_semantics=("arbitrary", "arbitrary"), vmem_limit_bytes=VMEM_LIMIT),
        name="mlp",
    )(x, wu, wd, g, b)


def _regroup_w_in(w):
    o = 0
    parts = {}
    for name, width in (("abc", 3 * D_CONV), ("qkv", 3 * D_GDN), ("z", D_GDN), ("logit", 2 * N_HEADS),
                        ("ma", D_MODEL), ("mb", D_MODEL)):
        parts[name] = w[:, o:o + width]
        o += width
    pad = jnp.zeros((w.shape[0], N_PROJ - COL_LOGIT - 2 * N_HEADS), w.dtype)
    return jnp.concatenate(
        [parts["abc"], parts["qkv"], parts["ma"], parts["mb"], parts["z"], parts["logit"], pad],
        axis=1).astype(BF16)


def _lane_row(vec, offset):
    return jnp.zeros((LANE,), F32).at[offset:offset + vec.shape[0]].set(vec.astype(F32))


def _front_pad_state(st):
    n, k1, c = st.shape
    return jnp.concatenate([jnp.zeros((n, SAMPLE_LEN - k1, c), st.dtype), st], axis=1).reshape(n * SAMPLE_LEN, c)


def kernel(x_prompt, x_sample, state_conv_a, state_gdn_conv, state_gdn, w_in, conv_a_w, gdn_conv_w, a_log,
           dt_bias, gdn_norm_w, w_a_out, w_b_out, w_o, ln1_g, ln1_b, w_up, w_down, ln2_g, ln2_b):
    batch, seq, _ = x_prompt.shape
    n_seq = x_sample.shape[0]
    n_prompt = batch * seq
    n_tok = n_prompt + n_seq * SAMPLE_LEN
    assert x_sample.shape[1] == SAMPLE_LEN and seq % GROUP == 0 and n_seq % SEQ_PER_GROUP == 0

    x = jnp.concatenate([x_prompt.reshape(n_prompt, D_MODEL), x_sample.reshape(-1, D_MODEL)], axis=0)
    x_bf = x.astype(BF16)

    ca_p, cg_p, sg_p, ca_s, cg_s, sg_s = [], [], [], [], [], []
    for l in range(DEPTH):
        proj = _in_proj(x_bf, _regroup_w_in(w_in[l]), tm=1088, tn=1152)

        par = jnp.zeros((8, LANE), F32)
        par = par.at[0].set(_lane_row(a_log[l], LOGIT_A))
        par = par.at[1].set(_lane_row(dt_bias[l], LOGIT_A))
        par = par.at[2].set(gdn_norm_w[l].astype(F32))

        ya, og, utail, gtail, s_p = _prompt_mixer(proj, conv_a_w[l], gdn_conv_w[l], par, n_tok, batch, seq)
        ya, og, u_s, s_s = _sample_mixer(
            proj, conv_a_w[l], gdn_conv_w[l], par,
            _front_pad_state(state_conv_a[l]), _front_pad_state(state_gdn_conv[l]), state_gdn[l],
            ya, og, n_prompt // GROUP)

        x1 = _out_proj(ya, og, proj, x, w_a_out[l].astype(BF16), w_b_out[l].astype(BF16), w_o[l].astype(BF16),
                       ln1_g[l].reshape(1, D_MODEL), ln1_b[l].reshape(1, D_MODEL), tm=256)
        x, x_bf = _mlp(x1, w_up[l].astype(BF16), w_down[l].astype(BF16),
                       ln2_g[l].reshape(1, D_MODEL), ln2_b[l].reshape(1, D_MODEL), tm=512, tf=1024)

        ca_p.append(utail[:, 6:8])
        cg_p.append(gtail[:, 5:8])
        sg_p.append(s_p)
        ca_s.append(u_s.reshape(n_seq, SAMPLE_LEN, D_CONV)[:, 2:4])
        cg_s.append(proj[n_prompt:, COL_QKV:COL_QKV + 3 * D_GDN].reshape(n_seq, SAMPLE_LEN, 3 * D_GDN)[:, 1:4])
        sg_s.append(s_s)

    y_prompt = x[:n_prompt].reshape(batch, seq, D_MODEL)
    y_sample = x[n_prompt:].reshape(n_seq, SAMPLE_LEN, D_MODEL)
    return (y_prompt, y_sample, jnp.stack(ca_p), jnp.stack(cg_p), jnp.stack(sg_p),
            jnp.stack(ca_s), jnp.stack(cg_s), jnp.stack(sg_s))
```

```python
import functools

import jax
import jax.numpy as jnp
from jax import lax
from jax.experimental import pallas as pl
from jax.experimental.pallas import tpu as pltpu

F32 = jnp.float32
BF16 = jnp.bfloat16

D_MODEL = 2048
DEPTH = 2
D_CONV = D_MODEL // 2
N_HEADS = 8
HEAD_DIM = 128
D_GDN = N_HEADS * HEAD_DIM
D_FF = 4 * D_MODEL
ALPHA = (2 * DEPTH) ** 0.25
LN_EPS = 1e-5
NORM_EPS = 1e-6

COL_GB, COL_GC, COL_H = 0, D_CONV, 2 * D_CONV
COL_QKV = 3 * D_CONV
COL_Z = COL_QKV + 3 * D_GDN
N_MIX = COL_Z + D_GDN
COL_LOGIT = N_MIX
COL_GATE = COL_LOGIT + 2 * N_HEADS
LANE = 128
LOGIT_B, LOGIT_A = 0, N_HEADS

GROUP = 64
SAMPLE_LEN = 4
SEQ_PER_GROUP = GROUP // SAMPLE_LEN
INV_PASSES = 3

VMEM_LIMIT = 56 * 1024 * 1024


def _dot(a, b):
    return jnp.dot(a.astype(BF16), b.astype(BF16), preferred_element_type=F32)


def _dot_nt(a, b):
    return lax.dot_general(a.astype(BF16), b.astype(BF16), (((1,), (1,)), ((), ())),
                           preferred_element_type=F32)


def _split2(a):
    hi = a.astype(BF16)
    lo = (a - hi.astype(F32)).astype(BF16)
    return hi, lo


def _dot_multi(a, b, passes):
    if passes == 1:
        return _dot(a, b)
    ah, al = _split2(a)
    bh, bl = _split2(b)
    d = lambda x, y: jnp.dot(x, y, preferred_element_type=F32)
    return d(ah, bh) + (d(ah, bl) + d(al, bh))


def _dot_exact_lhs(m01, x):
    x1 = x.astype(BF16)
    r1 = x - x1.astype(F32)
    x2 = r1.astype(BF16)
    x3 = (r1 - x2.astype(F32)).astype(BF16)
    d = lambda y: jnp.dot(m01, y, preferred_element_type=F32)
    return d(x1) + (d(x2) + d(x3))


def _same_block(i, j, size):
    shift = size.bit_length() - 1
    assert size == 1 << shift
    return (i >> shift) == (j >> shift)


def _sigmoid(x):
    return 1.0 / (1.0 + jnp.exp(-x))


def _softplus(x):
    return jnp.maximum(x, 0.0) + jnp.log1p(jnp.exp(-jnp.abs(x)))


def _layer_norm(y, g, b):
    mu = jnp.mean(y, axis=-1, keepdims=True)
    yc = y - mu
    var = jnp.mean(yc * yc, axis=-1, keepdims=True)
    return yc * lax.rsqrt(var + LN_EPS) * g + b


def _layer_spec(layer, shape, index_map=None, **kw):
    if index_map is None:
        index_map = lambda *g: (0,) * len(shape)
    return pl.BlockSpec((None,) + tuple(shape), lambda *g: (layer,) + tuple(index_map(*g)), **kw)


def _params(*sem):
    return pltpu.CompilerParams(dimension_semantics=sem, vmem_limit_bytes=VMEM_LIMIT)


def _proj_kernel(x_ref, w_ref, o_ref, wb_ref, *, gate):
    @pl.when(pl.program_id(1) == 0)
    def _():
        wb_ref[...] = w_ref[...].astype(BF16)

    y = jnp.dot(x_ref[...], wb_ref[...], preferred_element_type=F32)
    if gate:
        y = _sigmoid(y)
    o_ref[...] = y.astype(o_ref.dtype)


def _proj(x_bf, w, layer, n_cols, out_dtype, gate, tm, tn, name):
    t = x_bf.shape[0]
    return pl.pallas_call(
        functools.partial(_proj_kernel, gate=gate),
        grid=(n_cols // tn, t // tm),
        in_specs=[pl.BlockSpec((tm, D_MODEL), lambda j, i: (i, 0)),
                  _layer_spec(layer, (D_MODEL, tn), lambda j, i: (0, j))],
        out_specs=pl.BlockSpec((tm, tn), lambda j, i: (i, j)),
        out_shape=jax.ShapeDtypeStruct((t, n_cols), out_dtype),
        scratch_shapes=[pltpu.VMEM((D_MODEL, tn), BF16)],
        compiler_params=_params("arbitrary", "arbitrary"),
        name=name,
    )(x_bf, w)


def _causal_conv(u, w_ref, prev_fn):
    k = w_ref.shape[0]
    out = u * w_ref[k - 1:k, :]
    for j in range(1, k):
        out = out + prev_fn(j) * w_ref[k - 1 - j:k - j, :]
    return out


def _prompt_prev(u, tail8):
    row8 = lax.broadcasted_iota(jnp.int32, (8, u.shape[1]), 0)

    def prev(j):
        ru = pltpu.roll(u, j, 0)
        top = jnp.where(row8 >= j, ru[0:8], pltpu.roll(tail8, j, 0))
        return jnp.concatenate([top, ru[8:]], axis=0)
    return prev


def _sample_prev(u, st4):
    t = lax.broadcasted_iota(jnp.int32, u.shape, 0) & (SAMPLE_LEN - 1)

    def prev(j):
        return jnp.where(t >= j, pltpu.roll(u, j, 0), pltpu.roll(st4, GROUP - (SAMPLE_LEN - j), 0))
    return prev


def _pair_iotas():
    row = lax.broadcasted_iota(jnp.int32, (GROUP, 2 * GROUP), 0)
    lane = lax.broadcasted_iota(jnp.int32, (GROUP, 2 * GROUP), 1)
    return row, lane & (GROUP - 1), lane >= GROUP


def _block_diag(y0, y1):
    z = jnp.zeros_like(y0)
    return jnp.concatenate([jnp.concatenate([y0, z], axis=1), jnp.concatenate([z, y1], axis=1)], axis=0)


def _pair_block_diag(y):
    _, _, second = _pair_iotas()
    return jnp.concatenate([jnp.where(second, 0.0, y), jnp.where(second, y, 0.0)], axis=0)


def _tri_inverse_pairs(ms, seq_len):
    row, col, _ = _pair_iotas()
    eye = (row == col).astype(F32)
    base = min(8, seq_len)
    mm = lambda x, y: _dot_multi(x, _pair_block_diag(y), INV_PASSES)
    same = lambda b: _same_block(row, col, b)
    d = [jnp.where(same(base), m, 0.0) for m in ms]
    d2 = [mm(x, x) for x in d]
    xs = [eye - x for x in d]
    if base > 4:
        d4 = [mm(y, y) for y in d2]
    inv = [x + mm(x, y) for x, y in zip(xs, d2)]
    if base > 4:
        inv = [i + mm(i, y) for i, y in zip(inv, d4)]
    b = base * 2
    while b <= seq_len:
        off_diag = same(b) & jnp.logical_not(same(b // 2))
        t = [mm(i, jnp.where(off_diag, m, 0.0)) for i, m in zip(inv, ms)]
        inv = [i - mm(x, i) for i, x in zip(inv, t)]
        b *= 2
    return inv


def _gdn_group(q, k, v, z, logit, a_row, dt_row, nw_row, seq_len, state, og_ref):
    row = lax.broadcasted_iota(jnp.int32, (GROUP, GROUP), 0)
    col = lax.broadcasted_iota(jnp.int32, (GROUP, GROUP), 1)
    same_seq = _same_block(row, col, seq_len)
    lmat = jnp.concatenate([(same_seq & (row >= col)).astype(F32), same_seq.astype(F32)], axis=0).astype(BF16)
    prow, pcol, second = _pair_iotas()
    psame = _same_block(prow, pcol, seq_len)
    causal = psame & (prow >= pcol)
    strict = psame & (prow > pcol)

    beta = _sigmoid(logit)
    g = -jnp.exp(a_row) * _softplus(logit + dt_row)
    gc2 = _dot_exact_lhs(lmat, g)
    gcum, gtot = gc2[:GROUP], gc2[GROUP:]
    gcum_t2 = jnp.concatenate([gcum, gcum], axis=0).T
    e_cum = jnp.exp(gcum)
    e_rem = jnp.exp(gtot - gcum)
    e_tot = jnp.exp(gtot)

    heads = range(N_HEADS)
    pairs = range(0, N_HEADS, 2)
    hs = lambda x, h: x[:, h * HEAD_DIM:(h + 1) * HEAD_DIM]
    lcol = lambda x, h: x[:, LOGIT_A + h:LOGIT_A + h + 1]
    qn = [hs(q, h) * (lax.rsqrt(jnp.sum(hs(q, h) * hs(q, h), axis=-1, keepdims=True) + NORM_EPS)
                      * (HEAD_DIM ** -0.5)) for h in heads]
    kn = [hs(k, h) * lax.rsqrt(jnp.sum(hs(k, h) * hs(k, h), axis=-1, keepdims=True) + NORM_EPS) for h in heads]
    bcol = [beta[:, LOGIT_B + h:LOGIT_B + h + 1] for h in heads]
    kb = [kn[h] * bcol[h] for h in heads]

    a = [_dot_nt(jnp.concatenate([jnp.concatenate([kb[h], kb[h + 1]], axis=1),
                                  jnp.concatenate([qn[h], qn[h + 1]], axis=1)], axis=0),
                 _block_diag(kn[h], kn[h + 1])) for h in pairs]
    ms, attn = [], []
    for i, h in enumerate(pairs):
        gcol = jnp.where(second, lcol(gcum, h + 1), lcol(gcum, h))
        grow = jnp.where(second[0:1], gcum_t2[LOGIT_A + h + 1:LOGIT_A + h + 2, :],
                         gcum_t2[LOGIT_A + h:LOGIT_A + h + 1, :])
        dec = jnp.exp(jnp.where(causal, gcol - grow, 0.0))
        ms.append(jnp.where(strict, a[i][:GROUP] * dec, 0.0))
        attn.append(jnp.where(causal, a[i][GROUP:] * dec, 0.0))
    t_inv = _tri_inverse_pairs(ms, seq_len)

    vb = [hs(v, h) * bcol[h] for h in heads]
    kbe = [kb[h] * lcol(e_cum, h) for h in heads]
    uw = [_dot(t_inv[i], _block_diag(jnp.concatenate([vb[h], kbe[h]], axis=1),
                                     jnp.concatenate([vb[h + 1], kbe[h + 1]], axis=1)))
          for i, h in enumerate(pairs)]
    u = [uw[h // 2][:, (h % 2) * 2 * HEAD_DIM:(h % 2) * 2 * HEAD_DIM + HEAD_DIM] for h in heads]
    w = [uw[h // 2][:, (h % 2) * 2 * HEAD_DIM + HEAD_DIM:(h % 2 + 1) * 2 * HEAD_DIM] for h in heads]
    qdec = [qn[h] * lcol(e_cum, h) for h in heads]
    kdec = [kn[h] * lcol(e_rem, h) for h in heads]

    ws = [state.read(h, jnp.concatenate([w[h], qdec[h]], axis=0)) for h in heads]
    v_new = [u[h] - ws[h][0] for h in heads]
    vn_bd = [_block_diag(v_new[h], v_new[h + 1]) for h in pairs]
    o_pair = [jnp.concatenate([ws[h][1], ws[h + 1][1]], axis=1) + _dot(attn[i], vn_bd[i])
              for i, h in enumerate(pairs)]
    for i, h in enumerate(pairs):
        state.update(h, jnp.concatenate([kdec[h], kdec[h + 1]], axis=0).T, vn_bd[i],
                     lcol(e_tot, h), lcol(e_tot, h + 1))
    for h in heads:
        o = hs(o_pair[h // 2], h % 2)
        o = o * lax.rsqrt(jnp.mean(o * o, axis=-1, keepdims=True) + NORM_EPS) * nw_row
        zh = hs(z, h)
        og_ref[:, h * HEAD_DIM:(h + 1) * HEAD_DIM] = (o * (zh * _sigmoid(zh))).astype(og_ref.dtype)


def _mixer_common(x_ref, wl_ref, abc_ref, qkv_ref, z_ref, caw_ref, gcw_ref, par_ref, prev_a, prev_g,
                  seq_len, state, ya_ref, og_ref):
    gb = abc_ref[:, COL_GB:COL_GB + D_CONV]
    gc = abc_ref[:, COL_GC:COL_GC + D_CONV]
    hv = abc_ref[:, COL_H:COL_H + D_CONV]
    u = gc * hv
    conv_a = _causal_conv(u, caw_ref, prev_a(u))
    ya_ref[...] = (gb * conv_a).astype(ya_ref.dtype)

    raw = qkv_ref[...]
    cg = _causal_conv(raw, gcw_ref, prev_g(raw))
    act = cg * _sigmoid(cg)
    q, k, v = act[:, :D_GDN], act[:, D_GDN:2 * D_GDN], act[:, 2 * D_GDN:]
    logit = jnp.dot(x_ref[...], wl_ref[...], preferred_element_type=F32)
    _gdn_group(q, k, v, z_ref[...], logit, par_ref[0:1, :], par_ref[1:2, :], par_ref[2:3, :],
               seq_len, state, og_ref)
    return u, raw


def _prompt_mixer_kernel(x_ref, wl_ref, abc_ref, qkv_ref, z_ref, caw_ref, gcw_ref, par_ref,
                         ya_ref, og_ref, utail_ref, gtail_ref, s_ref):
    @pl.when(pl.program_id(1) == 0)
    def _():
        utail_ref[...] = jnp.zeros_like(utail_ref)
        gtail_ref[...] = jnp.zeros_like(gtail_ref)
        s_ref[...] = jnp.zeros_like(s_ref)

    class State:
        @staticmethod
        def read(h, lhs):
            ws = _dot(lhs, s_ref[0, h])
            return ws[:GROUP], ws[GROUP:]

        @staticmethod
        def update(h, kd_t, vn_bd, e0, e1):
            kv = _dot(kd_t, vn_bd)
            s_ref[0, h] = s_ref[0, h] * e0[0:1, :] + kv[:, :HEAD_DIM]
            s_ref[0, h + 1] = s_ref[0, h + 1] * e1[0:1, :] + kv[:, HEAD_DIM:]

    u, raw = _mixer_common(
        x_ref, wl_ref, abc_ref, qkv_ref, z_ref, caw_ref, gcw_ref, par_ref,
        lambda u: _prompt_prev(u, utail_ref[0]), lambda r: _prompt_prev(r, gtail_ref[0]),
        GROUP, State, ya_ref, og_ref)
    utail_ref[0] = u[GROUP - 8:]
    gtail_ref[0] = raw[GROUP - 8:]


def _sample_mixer_kernel(x_ref, wl_ref, abc_ref, qkv_ref, z_ref, caw_ref, gcw_ref, par_ref,
                         sta_ref, stg_ref, s0_ref, *rest):
    ya_ref, og_ref, u_ref, s_ref = rest[-4:]
    row8 = lax.broadcasted_iota(jnp.int32, (8, HEAD_DIM), 0)
    first = row8 < SAMPLE_LEN
    row_bd = lax.broadcasted_iota(jnp.int32, (2 * GROUP, 2 * HEAD_DIM), 0) & (GROUP - 1)

    class State:
        @staticmethod
        def read(h, lhs):
            w, qdec = lhs[:GROUP], lhs[GROUP:]
            ws_w, ws_q = [], []
            for p in range(GROUP // 8):
                slab = jnp.concatenate([w[8 * p:8 * p + 8], qdec[8 * p:8 * p + 8]], axis=0)
                r0 = _dot(slab, s0_ref[2 * p, h])
                r1 = _dot(slab, s0_ref[2 * p + 1, h])
                ws_w.append(jnp.where(first, r0[:8], r1[:8]))
                ws_q.append(jnp.where(first, r0[8:], r1[8:]))
            return jnp.concatenate(ws_w, axis=0), jnp.concatenate(ws_q, axis=0)

        @staticmethod
        def update(h, kd_t, vn_bd, e0, e1):
            for s in range(SEQ_PER_GROUP):
                in_seq = _same_block(row_bd, SAMPLE_LEN * s, SAMPLE_LEN)
                kv = _dot(kd_t, jnp.where(in_seq, vn_bd, 0.0))
                r = SAMPLE_LEN * s
                s_ref[s, h] = s0_ref[s, h] * e0[r:r + 1, :] + kv[:, :HEAD_DIM]
                s_ref[s, h + 1] = s0_ref[s, h + 1] * e1[r:r + 1, :] + kv[:, HEAD_DIM:]

    u, _ = _mixer_common(
        x_ref, wl_ref, abc_ref, qkv_ref, z_ref, caw_ref, gcw_ref, par_ref,
        lambda u: _sample_prev(u, sta_ref[...]), lambda r: _sample_prev(r, stg_ref[...]),
        SAMPLE_LEN, State, ya_ref, og_ref)
    u_ref[...] = u


def _mixer_in_specs(layer, row_block):
    rows = lambda width, col_block: pl.BlockSpec((GROUP, width), lambda *i: (row_block(*i), col_block))
    return [
        rows(D_MODEL, 0),
        _layer_spec(layer, (D_MODEL, LANE)),
        rows(3 * D_CONV, 0),
        rows(3 * D_GDN, COL_QKV // (3 * D_GDN)),
        rows(D_GDN, COL_Z // D_GDN),
        _layer_spec(layer, (3, D_CONV)),
        _layer_spec(layer, (4, 3 * D_GDN)),
        _layer_spec(layer, (8, LANE)),
    ]


def _prompt_mixer(layer, x_bf, w_logit, proj, caw, gcw, par, batch, seq):
    chunks = seq // GROUP
    rb = lambda b, c: b * chunks + c
    n = batch * seq
    return pl.pallas_call(
        _prompt_mixer_kernel,
        grid=(batch, chunks),
        in_specs=_mixer_in_specs(layer, rb),
        out_specs=[
            pl.BlockSpec((GROUP, D_CONV), lambda b, c: (rb(b, c), 0)),
            pl.BlockSpec((GROUP, D_GDN), lambda b, c: (rb(b, c), 0)),
            pl.BlockSpec((1, 8, D_CONV), lambda b, c: (b, 0, 0)),
            pl.BlockSpec((1, 8, 3 * D_GDN), lambda b, c: (b, 0, 0)),
            pl.BlockSpec((1, N_HEADS, HEAD_DIM, HEAD_DIM), lambda b, c: (b, 0, 0, 0)),
        ],
        out_shape=[
            jax.ShapeDtypeStruct((n, D_CONV), BF16),
            jax.ShapeDtypeStruct((n, D_GDN), BF16),
            jax.ShapeDtypeStruct((batch, 8, D_CONV), F32),
            jax.ShapeDtypeStruct((batch, 8, 3 * D_GDN), F32),
            jax.ShapeDtypeStruct((batch, N_HEADS, HEAD_DIM, HEAD_DIM), F32),
        ],
        compiler_params=_params("arbitrary", "arbitrary"),
        name="prompt_mixer",
    )(x_bf, w_logit, proj, proj, proj, caw, gcw, par)


def _sample_mixer(layer, x_bf, w_logit, proj, caw, gcw, par, sta, stg, s0, s_prev, first_block):
    n_seq = s0.shape[1]
    rb = lambda g: first_block + g
    state_block = (SEQ_PER_GROUP, N_HEADS, HEAD_DIM, HEAD_DIM)
    in_specs = _mixer_in_specs(layer, rb) + [
        _layer_spec(layer, (GROUP, D_CONV), lambda g: (g, 0)),
        _layer_spec(layer, (GROUP, 3 * D_GDN), lambda g: (g, 0)),
        _layer_spec(layer, state_block, lambda g: (g, 0, 0, 0)),
    ]
    args = [x_bf, w_logit, proj, proj, proj, caw, gcw, par, sta, stg, s0]
    aliases = {}
    if s_prev is not None:
        in_specs.append(pl.BlockSpec(memory_space=pl.ANY))
        aliases = {len(args): 3}
        args.append(s_prev)
    return pl.pallas_call(
        _sample_mixer_kernel,
        grid=(n_seq // SEQ_PER_GROUP,),
        in_specs=in_specs,
        out_specs=[
            pl.BlockSpec((GROUP, D_CONV), lambda g: (g, 0)),
            pl.BlockSpec((GROUP, D_GDN), lambda g: (g, 0)),
            pl.BlockSpec((GROUP, D_CONV), lambda g: (g, 0)),
            _layer_spec(layer, state_block, lambda g: (g, 0, 0, 0)),
        ],
        out_shape=[
            jax.ShapeDtypeStruct((n_seq * SAMPLE_LEN, D_CONV), BF16),
            jax.ShapeDtypeStruct((n_seq * SAMPLE_LEN, D_GDN), BF16),
            jax.ShapeDtypeStruct((n_seq * SAMPLE_LEN, D_CONV), F32),
            jax.ShapeDtypeStruct(s0.shape, F32),
        ],
        input_output_aliases=aliases,
        compiler_params=_params("arbitrary"),
        name="sample_mixer",
    )(*args)


def _out_proj_kernel(ya_lo, ya_hi, og_lo, og_hi, gate_ref, x_lo, x_hi, wa_ref, wb_ref, wo_ref, g_ref, b_ref,
                     o_ref, *, n_lo):
    lo = pl.program_id(0) < n_lo
    pick = lambda a, b: jnp.where(lo, a[...], b[...])
    y_a = jnp.dot(pick(ya_lo, ya_hi), wa_ref[...], preferred_element_type=F32)
    y_b = jnp.dot(pick(og_lo, og_hi), wb_ref[...], preferred_element_type=F32)
    merged = gate_ref[:, :D_MODEL].astype(F32) * y_a + gate_ref[:, D_MODEL:].astype(F32) * y_b
    y = ALPHA * pick(x_lo, x_hi) + jnp.dot(merged.astype(BF16), wo_ref[...], preferred_element_type=F32)
    o_ref[...] = _layer_norm(y, g_ref[...], b_ref[...])


def _out_proj(layer, ya_p, ya_s, og_p, og_s, gates, x_lo, x_hi, hi_block0, wa, wb, wo, g, b, tm):
    t = gates.shape[0]
    n_lo = ya_p.shape[0] // tm
    lo = lambda width: pl.BlockSpec((tm, width), lambda i: (jnp.minimum(i, n_lo - 1), 0))
    hi = lambda width, block0=0: pl.BlockSpec((tm, width), lambda i: (jnp.maximum(i - n_lo, 0) + block0, 0))
    const = lambda shape: _layer_spec(layer, shape, pipeline_mode=pl.Buffered(1))
    return pl.pallas_call(
        functools.partial(_out_proj_kernel, n_lo=n_lo),
        grid=(t // tm,),
        in_specs=[
            lo(D_CONV), hi(D_CONV), lo(D_GDN), hi(D_GDN),
            pl.BlockSpec((tm, 2 * D_MODEL), lambda i: (i, 0)),
            lo(D_MODEL), hi(D_MODEL, hi_block0),
            const((D_CONV, D_MODEL)), const((D_GDN, D_MODEL)), const((D_MODEL, D_MODEL)),
            const((1, D_MODEL)), const((1, D_MODEL)),
        ],
        out_specs=pl.BlockSpec((tm, D_MODEL), lambda i: (i, 0)),
        out_shape=jax.ShapeDtypeStruct((t, D_MODEL), F32),
        compiler_params=_params("arbitrary"),
        name="out_proj",
    )(ya_p, ya_s, og_p, og_s, gates, x_lo, x_hi, wa, wb, wo, g, b)


def _mlp_kernel(x_ref, wu_ref, wd_ref, g_ref, b_ref, o0_ref, o1_ref, xb_ref, acc_ref, *, n_lo):
    i, j = pl.program_id(0), pl.program_id(1)

    @pl.when(j == 0)
    def _():
        xb_ref[...] = x_ref[...].astype(BF16)

    hid = jnp.maximum(jnp.dot(xb_ref[...], wu_ref[...], preferred_element_type=F32), 0.0)
    part = jnp.dot((hid * hid).astype(BF16), wd_ref[...], preferred_element_type=F32)

    @pl.when(j == 0)
    def _():
        acc_ref[...] = part

    @pl.when(j > 0)
    def _():
        acc_ref[...] += part

    @pl.when(j == pl.num_programs(1) - 1)
    def _():
        y = _layer_norm(ALPHA * x_ref[...] + acc_ref[...], g_ref[...], b_ref[...])
        if n_lo is None:
            o0_ref[...] = y
            o1_ref[...] = y.astype(BF16)
        else:
            @pl.when(i < n_lo)
            def _():
                o0_ref[...] = y

            @pl.when(i >= n_lo)
            def _():
                o1_ref[...] = y


def _mlp(layer, x, wu, wd, g, b, tm, tf, n_prompt=None):
    t = x.shape[0]
    if n_prompt is None:
        n_lo = None
        out_specs = [pl.BlockSpec((tm, D_MODEL), lambda i, j: (i, 0)),
                     pl.BlockSpec((tm, D_MODEL), lambda i, j: (i, 0))]
        out_shape = [jax.ShapeDtypeStruct((t, D_MODEL), F32), jax.ShapeDtypeStruct((t, D_MODEL), BF16)]
    else:
        n_lo = n_prompt // tm
        out_specs = [pl.BlockSpec((tm, D_MODEL), lambda i, j: (jnp.minimum(i, n_lo - 1), 0)),
                     pl.BlockSpec((tm, D_MODEL), lambda i, j: (jnp.maximum(i - n_lo, 0), 0))]
        out_shape = [jax.ShapeDtypeStruct((n_prompt, D_MODEL), F32),
                     jax.ShapeDtypeStruct((t - n_prompt, D_MODEL), F32)]
    return pl.pallas_call(
        functools.partial(_mlp_kernel, n_lo=n_lo),
        grid=(t // tm, D_FF // tf),
        in_specs=[
            pl.BlockSpec((tm, D_MODEL), lambda i, j: (i, 0)),
            _layer_spec(layer, (D_MODEL, tf), lambda i, j: (0, j)),
            _layer_spec(layer, (tf, D_MODEL), lambda i, j: (j, 0)),
            _layer_spec(layer, (1, D_MODEL)),
            _layer_spec(layer, (1, D_MODEL)),
        ],
        out_specs=out_specs,
        out_shape=out_shape,
        scratch_shapes=[pltpu.VMEM((tm, D_MODEL), BF16), pltpu.VMEM((tm, D_MODEL), F32)],
        compiler_params=_params("arbitrary", "arbitrary"),
        name="mlp",
    )(x, wu, wd, g, b)


def _front_pad_state(st):
    d, n, k1, c = st.shape
    return jnp.pad(st, ((0, 0), (0, 0), (SAMPLE_LEN - k1, 0), (0, 0))).reshape(d, n * SAMPLE_LEN, c)


def kernel(x_prompt, x_sample, state_conv_a, state_gdn_conv, state_gdn, w_in, conv_a_w, gdn_conv_w, a_log,
           dt_bias, gdn_norm_w, w_a_out, w_b_out, w_o, ln1_g, ln1_b, w_up, w_down, ln2_g, ln2_b):
    batch, seq, _ = x_prompt.shape
    n_seq = x_sample.shape[0]
    n_prompt = batch * seq
    n_sample = n_seq * SAMPLE_LEN
    tm_proj, tm_out, tm_mlp = 1088, 256, 512
    assert x_sample.shape[1] == SAMPLE_LEN and seq % GROUP == 0 and n_seq % SEQ_PER_GROUP == 0
    assert n_prompt % tm_out == 0 and n_sample % tm_out == 0 and n_prompt % tm_mlp == 0 and n_sample == tm_mlp

    xp = x_prompt.reshape(n_prompt, D_MODEL)
    xs = x_sample.reshape(n_sample, D_MODEL)
    x_bf = jnp.concatenate([xp.astype(BF16), xs.astype(BF16)], axis=0)
    x_lo, x_hi, hi_block0 = xp, xs, 0

    w_logit = jnp.pad(w_in[:, :, COL_LOGIT:COL_GATE], ((0, 0), (0, 0), (0, LANE - 2 * N_HEADS))).astype(BF16)
    w_gate = w_in[:, :, COL_GATE:].astype(BF16)
    wa, wb, wo = w_a_out.astype(BF16), w_b_out.astype(BF16), w_o.astype(BF16)
    wu, wd = w_up.astype(BF16), w_down.astype(BF16)
    par = jnp.zeros((DEPTH, 8, LANE), F32)
    par = par.at[:, 0, LOGIT_A:LOGIT_A + N_HEADS].set(a_log.astype(F32))
    par = par.at[:, 1, LOGIT_A:LOGIT_A + N_HEADS].set(dt_bias.astype(F32))
    par = par.at[:, 2, :].set(gdn_norm_w.astype(F32))
    sta, stg = _front_pad_state(state_conv_a), _front_pad_state(state_gdn_conv)
    row3 = lambda a: a.reshape(DEPTH, 1, D_MODEL)

    ca_p, cg_p, sg_p, ca_s, cg_s = [], [], [], [], []
    sg_s = None
    for l in range(DEPTH):
        proj = _proj(x_bf, w_in, l, N_MIX, F32, False, tm_proj, 1024, "proj")
        gates = _proj(x_bf, w_gate, l, 2 * D_MODEL, BF16, True, tm_proj, 1024, "gates")

        ya_p, og_p, utail, gtail, s_p = _prompt_mixer(l, x_bf, w_logit, proj, conv_a_w, gdn_conv_w, par, batch, seq)
        ya_s, og_s, u_s, sg_s = _sample_mixer(l, x_bf, w_logit, proj, conv_a_w, gdn_conv_w, par, sta, stg,
                                              state_gdn, sg_s, n_prompt // GROUP)

        x1 = _out_proj(l, ya_p, ya_s, og_p, og_s, gates, x_lo, x_hi, hi_block0, wa, wb, wo,
                       row3(ln1_g), row3(ln1_b), tm_out)
        if l < DEPTH - 1:
            x, x_bf = _mlp(l, x1, wu, wd, row3(ln2_g), row3(ln2_b), tm_mlp, 1024)
            x_lo, x_hi, hi_block0 = x, x, n_prompt // tm_out
        else:
            y_p, y_s = _mlp(l, x1, wu, wd, row3(ln2_g), row3(ln2_b), tm_mlp, 1024, n_prompt=n_prompt)

        ca_p.append(utail[:, 6:8])
        cg_p.append(gtail[:, 5:8])
        sg_p.append(s_p)
        ca_s.append(u_s.reshape(n_seq, SAMPLE_LEN, D_CONV)[:, 2:4])
        cg_s.append(proj[n_prompt:, COL_QKV:COL_QKV + 3 * D_GDN].reshape(n_seq, SAMPLE_LEN, 3 * D_GDN)[:, 1:4])

    return (y_p.reshape(batch, seq, D_MODEL), y_s.reshape(n_seq, SAMPLE_LEN, D_MODEL),
            jnp.stack(ca_p), jnp.stack(cg_p), jnp.stack(sg_p), jnp.stack(ca_s), jnp.stack(cg_s), sg_s)
```

```python
import functools

import jax
import jax.numpy as jnp
from jax import lax
from jax.experimental import pallas as pl
from jax.experimental.pallas import tpu as pltpu

F32 = jnp.float32
BF16 = jnp.bfloat16

D_MODEL = 2048
DEPTH = 2
D_CONV = D_MODEL // 2
N_HEADS = 8
HEAD_DIM = 128
D_GDN = N_HEADS * HEAD_DIM
D_FF = 4 * D_MODEL
ALPHA = (2 * DEPTH) ** 0.25
LN_EPS = 1e-5
NORM_EPS = 1e-6

COL_GB, COL_GC, COL_H = 0, D_CONV, 2 * D_CONV
COL_QKV = 3 * D_CONV
COL_Z = COL_QKV + 3 * D_GDN
N_MIX = COL_Z + D_GDN
COL_LOGIT = N_MIX
COL_GATE = COL_LOGIT + 2 * N_HEADS
LANE = 128
LOGIT_B, LOGIT_A = 0, N_HEADS

GROUP = 64
PROMPT_SUB = 2
SAMPLE_LEN = 4
SEQ_PER_GROUP = GROUP // SAMPLE_LEN
INV_PASSES = 1

VMEM_LIMIT = 56 * 1024 * 1024


def _dot(a, b):
    return jnp.dot(a.astype(BF16), b.astype(BF16), preferred_element_type=F32)


def _dot_nt(a, b):
    return lax.dot_general(a.astype(BF16), b.astype(BF16), (((1,), (1,)), ((), ())),
                           preferred_element_type=F32)


def _split2(a):
    hi = a.astype(BF16)
    lo = (a - hi.astype(F32)).astype(BF16)
    return hi, lo


def _dot_multi(a, b, passes):
    if passes == 1:
        return _dot(a, b)
    ah, al = _split2(a)
    bh, bl = _split2(b)
    d = lambda x, y: jnp.dot(x, y, preferred_element_type=F32)
    return d(ah, bh) + (d(ah, bl) + d(al, bh))


def _dot_exact_lhs(m01, x):
    x1 = x.astype(BF16)
    r1 = x - x1.astype(F32)
    x2 = r1.astype(BF16)
    x3 = (r1 - x2.astype(F32)).astype(BF16)
    d = lambda y: jnp.dot(m01, y, preferred_element_type=F32)
    return d(x1) + (d(x2) + d(x3))


def _same_block(i, j, size):
    shift = size.bit_length() - 1
    assert size == 1 << shift
    return (i >> shift) == (j >> shift)


def _sigmoid(x):
    return 1.0 / (1.0 + jnp.exp(-x))


def _softplus(x):
    return jnp.maximum(x, 0.0) + jnp.log1p(jnp.exp(-jnp.abs(x)))


def _layer_norm(y, g, b):
    mu = jnp.mean(y, axis=-1, keepdims=True)
    yc = y - mu
    var = jnp.mean(yc * yc, axis=-1, keepdims=True)
    return yc * lax.rsqrt(var + LN_EPS) * g + b


def _layer_spec(layer, shape, index_map=None, **kw):
    if index_map is None:
        index_map = lambda *g: (0,) * len(shape)
    return pl.BlockSpec((None,) + tuple(shape), lambda *g: (layer,) + tuple(index_map(*g)), **kw)


def _params(*sem):
    return pltpu.CompilerParams(dimension_semantics=sem, vmem_limit_bytes=VMEM_LIMIT)


def _proj_kernel(x_ref, wt_ref, o_ref, wb_ref, *, gate):
    @pl.when(pl.program_id(1) == 0)
    def _():
        wb_ref[...] = wt_ref[0].astype(BF16)

    y = lax.dot_general(x_ref[...], wb_ref[...], (((1,), (1,)), ((), ())), preferred_element_type=F32)
    if gate:
        y = _sigmoid(y)
    o_ref[...] = y.astype(o_ref.dtype)


def _proj(x_bf, w_t, layer, col0, n_cols, out_dtype, gate, tm, tn, name):
    t = x_bf.shape[0]
    return pl.pallas_call(
        functools.partial(_proj_kernel, gate=gate),
        grid=(n_cols // tn, t // tm),
        in_specs=[pl.BlockSpec((tm, D_MODEL), lambda j, i: (i, 0)),
                  pl.BlockSpec((pl.Element(1), pl.Element(tn), pl.Element(D_MODEL)),
                               lambda j, i: (layer, pl.multiple_of(col0 + j * tn, 16), 0))],
        out_specs=pl.BlockSpec((tm, tn), lambda j, i: (i, j)),
        out_shape=jax.ShapeDtypeStruct((t, n_cols), out_dtype),
        scratch_shapes=[pltpu.VMEM((tn, D_MODEL), BF16)],
        compiler_params=_params("arbitrary", "arbitrary"),
        name=name,
    )(x_bf, w_t)


def _causal_conv(u, w_ref, prev_fn):
    k = w_ref.shape[0]
    out = u * w_ref[k - 1:k, :]
    for j in range(1, k):
        out = out + prev_fn(j) * w_ref[k - 1 - j:k - j, :]
    return out


def _prompt_prev(u, buf_ref):
    buf_ref[8:, :] = u
    return lambda j: buf_ref[8 - j:8 - j + u.shape[0], :]


def _sample_prev(u, st4):
    t = lax.broadcasted_iota(jnp.int32, u.shape, 0) & (SAMPLE_LEN - 1)

    def prev(j):
        return jnp.where(t >= j, pltpu.roll(u, j, 0), pltpu.roll(st4, GROUP - (SAMPLE_LEN - j), 0))
    return prev


def _pair_iotas():
    row = lax.broadcasted_iota(jnp.int32, (GROUP, 2 * GROUP), 0)
    lane = lax.broadcasted_iota(jnp.int32, (GROUP, 2 * GROUP), 1)
    return row, lane & (GROUP - 1), lane >= GROUP


def _block_diag(y0, y1):
    z = jnp.zeros_like(y0)
    return jnp.concatenate([jnp.concatenate([y0, z], axis=1), jnp.concatenate([z, y1], axis=1)], axis=0)


def _pair_block_diag(y):
    _, _, second = _pair_iotas()
    return jnp.concatenate([jnp.where(second, 0.0, y), jnp.where(second, y, 0.0)], axis=0)


def _tri_inverse_pairs(ms, seq_len):
    row, col, _ = _pair_iotas()
    eye = (row == col).astype(F32)
    base = min(8, seq_len)
    mm = lambda x, y: _dot_multi(x, _pair_block_diag(y), INV_PASSES)
    same = lambda b: _same_block(row, col, b)
    d = [jnp.where(same(base), m, 0.0) for m in ms]
    d2 = [mm(x, x) for x in d]
    xs = [eye - x for x in d]
    if base > 4:
        d4 = [mm(y, y) for y in d2]
    inv = [x + mm(x, y) for x, y in zip(xs, d2)]
    if base > 4:
        inv = [i + mm(i, y) for i, y in zip(inv, d4)]
    b = base * 2
    while b <= seq_len:
        off_diag = same(b) & jnp.logical_not(same(b // 2))
        t = [mm(i, jnp.where(off_diag, m, 0.0)) for i, m in zip(inv, ms)]
        inv = [i - mm(x, i) for i, x in zip(inv, t)]
        b *= 2
    return inv


def _gdn_group(q, k, v, z, logit, a_row, dt_row, nw_row, seq_len, state, og_ref):
    row = lax.broadcasted_iota(jnp.int32, (GROUP, GROUP), 0)
    col = lax.broadcasted_iota(jnp.int32, (GROUP, GROUP), 1)
    same_seq = _same_block(row, col, seq_len)
    lmat = jnp.concatenate([(same_seq & (row >= col)).astype(F32), same_seq.astype(F32)], axis=0).astype(BF16)
    prow, pcol, second = _pair_iotas()
    psame = _same_block(prow, pcol, seq_len)
    causal = psame & (prow >= pcol)
    strict = psame & (prow > pcol)

    subs = range(q.shape[0] // GROUP)
    heads = range(N_HEADS)
    pairs = range(0, N_HEADS, 2)
    units = [(c, h) for c in subs for h in pairs]
    rows = lambda x, c: x[c * GROUP:(c + 1) * GROUP]
    hs = lambda x, c, h: x[c * GROUP:(c + 1) * GROUP, h * HEAD_DIM:(h + 1) * HEAD_DIM]
    lcol = lambda x, h: x[:, LOGIT_A + h:LOGIT_A + h + 1]
    cat = jnp.concatenate

    beta = _sigmoid(logit)
    g = -jnp.exp(a_row) * _softplus(logit + dt_row)
    gc2 = [_dot_exact_lhs(lmat, rows(g, c)) for c in subs]
    gcum = [x[:GROUP] for x in gc2]
    gtot = [x[GROUP:] for x in gc2]
    gcum_t2 = [cat([x, x], axis=0).T for x in gcum]
    e_cum = [jnp.exp(x) for x in gcum]
    e_rem = [jnp.exp(t - x) for t, x in zip(gtot, gcum)]
    e_tot = [jnp.exp(t) for t in gtot]

    qn, kn, bcol, kb = {}, {}, {}, {}
    for c in subs:
        for h in heads:
            qh, kh = hs(q, c, h), hs(k, c, h)
            qn[c, h] = qh * (lax.rsqrt(jnp.sum(qh * qh, axis=-1, keepdims=True) + NORM_EPS) * (HEAD_DIM ** -0.5))
            kn[c, h] = kh * lax.rsqrt(jnp.sum(kh * kh, axis=-1, keepdims=True) + NORM_EPS)
            bcol[c, h] = rows(beta, c)[:, LOGIT_B + h:LOGIT_B + h + 1]
            kb[c, h] = kn[c, h] * bcol[c, h]

    a = {(c, h): _dot_nt(cat([cat([kb[c, h], kb[c, h + 1]], axis=1), cat([qn[c, h], qn[c, h + 1]], axis=1)], axis=0),
                         _block_diag(kn[c, h], kn[c, h + 1])) for c, h in units}
    ms, attn = {}, {}
    for c, h in units:
        gcol = jnp.where(second, lcol(gcum[c], h + 1), lcol(gcum[c], h))
        grow = jnp.where(second[0:1], gcum_t2[c][LOGIT_A + h + 1:LOGIT_A + h + 2, :],
                         gcum_t2[c][LOGIT_A + h:LOGIT_A + h + 1, :])
        dec = jnp.exp(jnp.where(causal, gcol - grow, 0.0))
        ms[c, h] = jnp.where(strict, a[c, h][:GROUP] * dec, 0.0)
        attn[c, h] = jnp.where(causal, a[c, h][GROUP:] * dec, 0.0)
    t_inv = dict(zip(units, _tri_inverse_pairs([ms[un] for un in units], seq_len)))

    u, w, qdec, kdec = {}, {}, {}, {}
    for c, h in units:
        rhs = [cat([hs(v, c, hh) * bcol[c, hh], kb[c, hh] * lcol(e_cum[c], hh)], axis=1) for hh in (h, h + 1)]
        uw = _dot(t_inv[c, h], _block_diag(*rhs))
        for i, hh in enumerate((h, h + 1)):
            u[c, hh] = uw[:, 2 * i * HEAD_DIM:(2 * i + 1) * HEAD_DIM]
            w[c, hh] = uw[:, (2 * i + 1) * HEAD_DIM:(2 * i + 2) * HEAD_DIM]
            qdec[c, hh] = qn[c, hh] * lcol(e_cum[c], hh)
            kdec[c, hh] = kn[c, hh] * lcol(e_rem[c], hh)

    for c in subs:
        ws = [state.read(h, cat([w[c, h], qdec[c, h]], axis=0)) for h in heads]
        v_new = [u[c, h] - ws[h][0] for h in heads]
        vn_bd = {h: _block_diag(v_new[h], v_new[h + 1]) for h in pairs}
        o_pair = {h: cat([ws[h][1], ws[h + 1][1]], axis=1) + _dot(attn[c, h], vn_bd[h]) for h in pairs}
        for h in pairs:
            state.update(h, cat([kdec[c, h], kdec[c, h + 1]], axis=0).T, vn_bd[h],
                         lcol(e_tot[c], h), lcol(e_tot[c], h + 1))
        for h in heads:
            o = o_pair[h - h % 2][:, (h % 2) * HEAD_DIM:(h % 2 + 1) * HEAD_DIM]
            o = o * lax.rsqrt(jnp.mean(o * o, axis=-1, keepdims=True) + NORM_EPS) * nw_row
            zh = hs(z, c, h)
            og_ref[c * GROUP:(c + 1) * GROUP, h * HEAD_DIM:(h + 1) * HEAD_DIM] = (
                o * (zh * _sigmoid(zh))).astype(og_ref.dtype)
    state.finish()


def _mixer_common(x_ref, wl_ref, abc_ref, qkv_ref, z_ref, caw_ref, gcw_ref, par_ref, prev_a, prev_g,
                  seq_len, state, ya_ref, og_ref):
    gb = abc_ref[:, COL_GB:COL_GB + D_CONV]
    gc = abc_ref[:, COL_GC:COL_GC + D_CONV]
    hv = abc_ref[:, COL_H:COL_H + D_CONV]
    u = gc * hv
    conv_a = _causal_conv(u, caw_ref, prev_a(u))
    ya_ref[...] = (gb * conv_a).astype(ya_ref.dtype)

    raw = qkv_ref[...]
    cg = _causal_conv(raw, gcw_ref, prev_g(raw))
    act = cg * _sigmoid(cg)
    q, k, v = act[:, :D_GDN], act[:, D_GDN:2 * D_GDN], act[:, 2 * D_GDN:]
    logit = lax.dot_general(x_ref[...], wl_ref[...], (((1,), (1,)), ((), ())), preferred_element_type=F32)
    _gdn_group(q, k, v, z_ref[...], logit, par_ref[0:1, :], par_ref[1:2, :], par_ref[2:3, :],
               seq_len, state, og_ref)
    return u, raw


def _prompt_mixer_kernel(x_ref, wl_ref, abc_ref, qkv_ref, z_ref, caw_ref, gcw_ref, par_ref,
                         ya_ref, og_ref, utail_ref, gtail_ref, s_ref, ubuf_ref, gbuf_ref):
    @pl.when(pl.program_id(1) == 0)
    def _():
        ubuf_ref[0:8, :] = jnp.zeros((8, D_CONV), F32)
        gbuf_ref[0:8, :] = jnp.zeros((8, 3 * D_GDN), F32)
        s_ref[...] = jnp.zeros_like(s_ref)

    cur = {}

    class State:
        @staticmethod
        def read(h, lhs):
            if h not in cur:
                cur[h] = s_ref[0, h]
            ws = _dot(lhs, cur[h])
            return ws[:GROUP], ws[GROUP:]

        @staticmethod
        def update(h, kd_t, vn_bd, e0, e1):
            kv = _dot(kd_t, vn_bd)
            cur[h] = cur[h] * e0[0:1, :] + kv[:, :HEAD_DIM]
            cur[h + 1] = cur[h + 1] * e1[0:1, :] + kv[:, HEAD_DIM:]

        @staticmethod
        def finish():
            for h, s in cur.items():
                s_ref[0, h] = s

    u, raw = _mixer_common(
        x_ref, wl_ref, abc_ref, qkv_ref, z_ref, caw_ref, gcw_ref, par_ref,
        lambda u: _prompt_prev(u, ubuf_ref), lambda r: _prompt_prev(r, gbuf_ref),
        GROUP, State, ya_ref, og_ref)
    n = u.shape[0]
    ubuf_ref[0:8, :] = u[n - 8:]
    gbuf_ref[0:8, :] = raw[n - 8:]
    utail_ref[0] = u[n - 8:]
    gtail_ref[0] = raw[n - 8:]


def _sample_mixer_kernel(x_ref, wl_ref, abc_ref, qkv_ref, z_ref, caw_ref, gcw_ref, par_ref,
                         sta_ref, stg_ref, s0_ref, *rest):
    ya_ref, og_ref, u_ref, s_ref = rest[-4:]
    row8 = lax.broadcasted_iota(jnp.int32, (8, HEAD_DIM), 0)
    first = row8 < SAMPLE_LEN
    row_bd = lax.broadcasted_iota(jnp.int32, (2 * GROUP, 2 * HEAD_DIM), 0) & (GROUP - 1)

    class State:
        @staticmethod
        def read(h, lhs):
            w, qdec = lhs[:GROUP], lhs[GROUP:]
            ws_w, ws_q = [], []
            for p in range(GROUP // 8):
                slab = jnp.concatenate([w[8 * p:8 * p + 8], qdec[8 * p:8 * p + 8]], axis=0)
                r0 = _dot(slab, s0_ref[2 * p, h])
                r1 = _dot(slab, s0_ref[2 * p + 1, h])
                ws_w.append(jnp.where(first, r0[:8], r1[:8]))
                ws_q.append(jnp.where(first, r0[8:], r1[8:]))
            return jnp.concatenate(ws_w, axis=0), jnp.concatenate(ws_q, axis=0)

        @staticmethod
        def update(h, kd_t, vn_bd, e0, e1):
            for s in range(SEQ_PER_GROUP):
                in_seq = _same_block(row_bd, SAMPLE_LEN * s, SAMPLE_LEN)
                kv = _dot(kd_t, jnp.where(in_seq, vn_bd, 0.0))
                r = SAMPLE_LEN * s
                s_ref[s, h] = s0_ref[s, h] * e0[r:r + 1, :] + kv[:, :HEAD_DIM]
                s_ref[s, h + 1] = s0_ref[s, h + 1] * e1[r:r + 1, :] + kv[:, HEAD_DIM:]

        @staticmethod
        def finish():
            pass

    u, _ = _mixer_common(
        x_ref, wl_ref, abc_ref, qkv_ref, z_ref, caw_ref, gcw_ref, par_ref,
        lambda u: _sample_prev(u, sta_ref[...]), lambda r: _sample_prev(r, stg_ref[...]),
        SAMPLE_LEN, State, ya_ref, og_ref)
    u_ref[...] = u


def _mixer_in_specs(layer, row_block, n_rows=GROUP):
    rows = lambda width, col_block: pl.BlockSpec((n_rows, width), lambda *i: (row_block(*i), col_block))
    return [
        rows(D_MODEL, 0),
        _layer_spec(layer, (LANE, D_MODEL)),
        rows(3 * D_CONV, 0),
        rows(3 * D_GDN, COL_QKV // (3 * D_GDN)),
        rows(D_GDN, COL_Z // D_GDN),
        _layer_spec(layer, (3, D_CONV)),
        _layer_spec(layer, (4, 3 * D_GDN)),
        _layer_spec(layer, (8, LANE)),
    ]


def _prompt_mixer(layer, x_bf, w_logit, proj, caw, gcw, par, batch, seq):
    step = GROUP * PROMPT_SUB
    chunks = seq // step
    rb = lambda b, c: b * chunks + c
    n = batch * seq
    return pl.pallas_call(
        _prompt_mixer_kernel,
        grid=(batch, chunks),
        in_specs=_mixer_in_specs(layer, rb, step),
        out_specs=[
            pl.BlockSpec((step, D_CONV), lambda b, c: (rb(b, c), 0)),
            pl.BlockSpec((step, D_GDN), lambda b, c: (rb(b, c), 0)),
            pl.BlockSpec((1, 8, D_CONV), lambda b, c: (b, 0, 0)),
            pl.BlockSpec((1, 8, 3 * D_GDN), lambda b, c: (b, 0, 0)),
            pl.BlockSpec((1, N_HEADS, HEAD_DIM, HEAD_DIM), lambda b, c: (b, 0, 0, 0)),
        ],
        out_shape=[
            jax.ShapeDtypeStruct((n, D_CONV), BF16),
            jax.ShapeDtypeStruct((n, D_GDN), BF16),
            jax.ShapeDtypeStruct((batch, 8, D_CONV), F32),
            jax.ShapeDtypeStruct((batch, 8, 3 * D_GDN), F32),
            jax.ShapeDtypeStruct((batch, N_HEADS, HEAD_DIM, HEAD_DIM), F32),
        ],
        scratch_shapes=[pltpu.VMEM((step + 8, D_CONV), F32), pltpu.VMEM((step + 8, 3 * D_GDN), F32)],
        compiler_params=_params("arbitrary", "arbitrary"),
        name="prompt_mixer",
    )(x_bf, w_logit, proj, proj, proj, caw, gcw, par)


def _sample_mixer(layer, x_bf, w_logit, proj, caw, gcw, par, sta, stg, s0, s_prev, first_block):
    n_seq = s0.shape[1]
    rb = lambda g: first_block + g
    state_block = (SEQ_PER_GROUP, N_HEADS, HEAD_DIM, HEAD_DIM)
    in_specs = _mixer_in_specs(layer, rb) + [
        _layer_spec(layer, (GROUP, D_CONV), lambda g: (g, 0)),
        _layer_spec(layer, (GROUP, 3 * D_GDN), lambda g: (g, 0)),
        _layer_spec(layer, state_block, lambda g: (g, 0, 0, 0)),
    ]
    args = [x_bf, w_logit, proj, proj, proj, caw, gcw, par, sta, stg, s0]
    aliases = {}
    if s_prev is not None:
        in_specs.append(pl.BlockSpec(memory_space=pl.ANY))
        aliases = {len(args): 3}
        args.append(s_prev)
    return pl.pallas_call(
        _sample_mixer_kernel,
        grid=(n_seq // SEQ_PER_GROUP,),
        in_specs=in_specs,
        out_specs=[
            pl.BlockSpec((GROUP, D_CONV), lambda g: (g, 0)),
            pl.BlockSpec((GROUP, D_GDN), lambda g: (g, 0)),
            pl.BlockSpec((GROUP, D_CONV), lambda g: (g, 0)),
            _layer_spec(layer, state_block, lambda g: (g, 0, 0, 0)),
        ],
        out_shape=[
            jax.ShapeDtypeStruct((n_seq * SAMPLE_LEN, D_CONV), BF16),
            jax.ShapeDtypeStruct((n_seq * SAMPLE_LEN, D_GDN), BF16),
            jax.ShapeDtypeStruct((n_seq * SAMPLE_LEN, D_CONV), F32),
            jax.ShapeDtypeStruct(s0.shape, F32),
        ],
        input_output_aliases=aliases,
        compiler_params=_params("arbitrary"),
        name="sample_mixer",
    )(*args)


def _out_proj_kernel(ya_lo, ya_hi, og_lo, og_hi, gate_ref, x_lo, x_hi, wa_ref, wb_ref, wo_ref, g_ref, b_ref,
                     o_ref, *, n_lo):
    lo = pl.program_id(0) < n_lo
    pick = lambda a, b: jnp.where(lo, a[...], b[...])
    y_a = jnp.dot(pick(ya_lo, ya_hi), wa_ref[...], preferred_element_type=F32)
    y_b = jnp.dot(pick(og_lo, og_hi), wb_ref[...], preferred_element_type=F32)
    merged = gate_ref[:, :D_MODEL].astype(F32) * y_a + gate_ref[:, D_MODEL:].astype(F32) * y_b
    y = ALPHA * pick(x_lo, x_hi) + jnp.dot(merged.astype(BF16), wo_ref[...], preferred_element_type=F32)
    o_ref[...] = _layer_norm(y, g_ref[...], b_ref[...])


def _out_proj(layer, ya_p, ya_s, og_p, og_s, gates, x_lo, x_hi, hi_block0, wa, wb, wo, g, b, tm):
    t = gates.shape[0]
    n_lo = ya_p.shape[0] // tm
    lo = lambda width: pl.BlockSpec((tm, width), lambda i: (jnp.minimum(i, n_lo - 1), 0))
    hi = lambda width, block0=0: pl.BlockSpec((tm, width), lambda i: (jnp.maximum(i - n_lo, 0) + block0, 0))
    const = lambda shape: _layer_spec(layer, shape, pipeline_mode=pl.Buffered(1))
    return pl.pallas_call(
        functools.partial(_out_proj_kernel, n_lo=n_lo),
        grid=(t // tm,),
        in_specs=[
            lo(D_CONV), hi(D_CONV), lo(D_GDN), hi(D_GDN),
            pl.BlockSpec((tm, 2 * D_MODEL), lambda i: (i, 0)),
            lo(D_MODEL), hi(D_MODEL, hi_block0),
            const((D_CONV, D_MODEL)), const((D_GDN, D_MODEL)), const((D_MODEL, D_MODEL)),
            const((1, D_MODEL)), const((1, D_MODEL)),
        ],
        out_specs=pl.BlockSpec((tm, D_MODEL), lambda i: (i, 0)),
        out_shape=jax.ShapeDtypeStruct((t, D_MODEL), F32),
        compiler_params=_params("arbitrary"),
        name="out_proj",
    )(ya_p, ya_s, og_p, og_s, gates, x_lo, x_hi, wa, wb, wo, g, b)


def _mlp_kernel(x_ref, wu_ref, wd_ref, g_ref, b_ref, o0_ref, o1_ref, xb_ref, acc_ref, *, n_lo):
    i, j = pl.program_id(0), pl.program_id(1)

    @pl.when(j == 0)
    def _():
        xb_ref[...] = x_ref[...].astype(BF16)

    hid = jnp.maximum(jnp.dot(xb_ref[...], wu_ref[...], preferred_element_type=F32), 0.0)
    part = jnp.dot((hid * hid).astype(BF16), wd_ref[...], preferred_element_type=F32)

    @pl.when(j == 0)
    def _():
        acc_ref[...] = part

    @pl.when(j > 0)
    def _():
        acc_ref[...] += part

    @pl.when(j == pl.num_programs(1) - 1)
    def _():
        y = _layer_norm(ALPHA * x_ref[...] + acc_ref[...], g_ref[...], b_ref[...])
        if n_lo is None:
            o0_ref[...] = y
            o1_ref[...] = y.astype(BF16)
        else:
            @pl.when(i < n_lo)
            def _():
                o0_ref[...] = y

            @pl.when(i >= n_lo)
            def _():
                o1_ref[...] = y


def _mlp(layer, x, wu, wd, g, b, tm, tf, n_prompt=None):
    t = x.shape[0]
    if n_prompt is None:
        n_lo = None
        out_specs = [pl.BlockSpec((tm, D_MODEL), lambda i, j: (i, 0)),
                     pl.BlockSpec((tm, D_MODEL), lambda i, j: (i, 0))]
        out_shape = [jax.ShapeDtypeStruct((t, D_MODEL), F32), jax.ShapeDtypeStruct((t, D_MODEL), BF16)]
    else:
        n_lo = n_prompt // tm
        out_specs = [pl.BlockSpec((tm, D_MODEL), lambda i, j: (jnp.minimum(i, n_lo - 1), 0)),
                     pl.BlockSpec((tm, D_MODEL), lambda i, j: (jnp.maximum(i - n_lo, 0), 0))]
        out_shape = [jax.ShapeDtypeStruct((n_prompt, D_MODEL), F32),
                     jax.ShapeDtypeStruct((t - n_prompt, D_MODEL), F32)]
    return pl.pallas_call(
        functools.partial(_mlp_kernel, n_lo=n_lo),
        grid=(t // tm, D_FF // tf),
        in_specs=[
            pl.BlockSpec((tm, D_MODEL), lambda i, j: (i, 0)),
            _layer_spec(layer, (D_MODEL, tf), lambda i, j: (0, j)),
            _layer_spec(layer, (tf, D_MODEL), lambda i, j: (j, 0)),
            _layer_spec(layer, (1, D_MODEL)),
            _layer_spec(layer, (1, D_MODEL)),
        ],
        out_specs=out_specs,
        out_shape=out_shape,
        scratch_shapes=[pltpu.VMEM((tm, D_MODEL), BF16), pltpu.VMEM((tm, D_MODEL), F32)],
        compiler_params=_params("arbitrary", "arbitrary"),
        name="mlp",
    )(x, wu, wd, g, b)


def _front_pad_state(st):
    d, n, k1, c = st.shape
    return jnp.pad(st, ((0, 0), (0, 0), (SAMPLE_LEN - k1, 0), (0, 0))).reshape(d, n * SAMPLE_LEN, c)


def kernel(x_prompt, x_sample, state_conv_a, state_gdn_conv, state_gdn, w_in, conv_a_w, gdn_conv_w, a_log,
           dt_bias, gdn_norm_w, w_a_out, w_b_out, w_o, ln1_g, ln1_b, w_up, w_down, ln2_g, ln2_b):
    batch, seq, _ = x_prompt.shape
    n_seq = x_sample.shape[0]
    n_prompt = batch * seq
    n_sample = n_seq * SAMPLE_LEN
    tm_proj, tm_out, tm_mlp = 1088, 256, 512
    assert x_sample.shape[1] == SAMPLE_LEN and seq % (GROUP * PROMPT_SUB) == 0 and n_seq % SEQ_PER_GROUP == 0
    assert n_prompt % tm_out == 0 and n_sample % tm_out == 0 and n_prompt % tm_mlp == 0 and n_sample == tm_mlp

    xp = x_prompt.reshape(n_prompt, D_MODEL)
    xs = x_sample.reshape(n_sample, D_MODEL)
    x_bf = jnp.concatenate([xp, xs], axis=0).astype(BF16)
    x_lo, x_hi, hi_block0 = xp, xs, 0

    w_t = jnp.swapaxes(w_in, 1, 2)
    w_logit = jnp.pad(w_t[:, COL_LOGIT:COL_GATE], ((0, 0), (0, LANE - 2 * N_HEADS), (0, 0))).astype(BF16)
    wa, wb, wo = w_a_out.astype(BF16), w_b_out.astype(BF16), w_o.astype(BF16)
    wu, wd = w_up.astype(BF16), w_down.astype(BF16)
    par = jnp.zeros((DEPTH, 8, LANE), F32)
    par = par.at[:, 0, LOGIT_A:LOGIT_A + N_HEADS].set(a_log.astype(F32))
    par = par.at[:, 1, LOGIT_A:LOGIT_A + N_HEADS].set(dt_bias.astype(F32))
    par = par.at[:, 2, :].set(gdn_norm_w.astype(F32))
    sta, stg = _front_pad_state(state_conv_a), _front_pad_state(state_gdn_conv)
    row3 = lambda a: a.reshape(DEPTH, 1, D_MODEL)

    ca_p, cg_p, sg_p, ca_s, cg_s = [], [], [], [], []
    sg_s = None
    for l in range(DEPTH):
        proj = _proj(x_bf, w_t, l, 0, N_MIX, F32, False, tm_proj, 1024, "proj")
        gates = _proj(x_bf, w_t, l, COL_GATE, 2 * D_MODEL, BF16, True, tm_proj, 1024, "gates")

        ya_p, og_p, utail, gtail, s_p = _prompt_mixer(l, x_bf, w_logit, proj, conv_a_w, gdn_conv_w, par, batch, seq)
        ya_s, og_s, u_s, sg_s = _sample_mixer(l, x_bf, w_logit, proj, conv_a_w, gdn_conv_w, par, sta, stg,
                                              state_gdn, sg_s, n_prompt // GROUP)

        x1 = _out_proj(l, ya_p, ya_s, og_p, og_s, gates, x_lo, x_hi, hi_block0, wa, wb, wo,
                       row3(ln1_g), row3(ln1_b), tm_out)
        if l < DEPTH - 1:
            x, x_bf = _mlp(l, x1, wu, wd, row3(ln2_g), row3(ln2_b), tm_mlp, 1024)
            x_lo, x_hi, hi_block0 = x, x, n_prompt // tm_out
        else:
            y_p, y_s = _mlp(l, x1, wu, wd, row3(ln2_g), row3(ln2_b), tm_mlp, 1024, n_prompt=n_prompt)

        ca_p.append(utail[:, 6:8])
        cg_p.append(gtail[:, 5:8])
        sg_p.append(s_p)
        ca_s.append(u_s.reshape(n_seq, SAMPLE_LEN, D_CONV)[:, 2:4])
        cg_s.append(proj[n_prompt:, COL_QKV:COL_QKV + 3 * D_GDN].reshape(n_seq, SAMPLE_LEN, 3 * D_GDN)[:, 1:4])

    return (y_p.reshape(batch, seq, D_MODEL), y_s.reshape(n_seq, SAMPLE_LEN, D_MODEL),
            jnp.stack(ca_p), jnp.stack(cg_p), jnp.stack(sg_p), jnp.stack(ca_s), jnp.stack(cg_s), sg_s)
```

```python
import functools

import jax
import jax.numpy as jnp
from jax import lax
from jax.experimental import pallas as pl
from jax.experimental.pallas import tpu as pltpu

F32 = jnp.float32
BF16 = jnp.bfloat16

D_MODEL = 2048
DEPTH = 2
D_CONV = D_MODEL // 2
N_HEADS = 8
HEAD_DIM = 128
D_GDN = N_HEADS * HEAD_DIM
D_FF = 4 * D_MODEL
ALPHA = (2 * DEPTH) ** 0.25
LN_EPS = 1e-5
NORM_EPS = 1e-6

COL_GB, COL_GC, COL_H = 0, D_CONV, 2 * D_CONV
COL_QKV = 3 * D_CONV
COL_Z = COL_QKV + 3 * D_GDN
N_MIX = COL_Z + D_GDN
COL_LOGIT = N_MIX
COL_GATE = COL_LOGIT + 2 * N_HEADS
LANE = 128
LOGIT_B, LOGIT_A = 0, N_HEADS

GROUP = 64
PROMPT_SUB = 2
SAMPLE_LEN = 4
SEQ_PER_GROUP = GROUP // SAMPLE_LEN
INV_PASSES = 1

VMEM_LIMIT = 56 * 1024 * 1024


def _dot(a, b):
    return jnp.dot(a.astype(BF16), b.astype(BF16), preferred_element_type=F32)


def _dot_nt(a, b):
    return lax.dot_general(a.astype(BF16), b.astype(BF16), (((1,), (1,)), ((), ())),
                           preferred_element_type=F32)


def _split2(a):
    hi = a.astype(BF16)
    lo = (a - hi.astype(F32)).astype(BF16)
    return hi, lo


def _dot_multi(a, b, passes):
    if passes == 1:
        return _dot(a, b)
    ah, al = _split2(a)
    bh, bl = _split2(b)
    d = lambda x, y: jnp.dot(x, y, preferred_element_type=F32)
    return d(ah, bh) + (d(ah, bl) + d(al, bh))


def _dot_exact_lhs(m01, x):
    x1 = x.astype(BF16)
    r1 = x - x1.astype(F32)
    x2 = r1.astype(BF16)
    x3 = (r1 - x2.astype(F32)).astype(BF16)
    d = lambda y: jnp.dot(m01, y, preferred_element_type=F32)
    return d(x1) + (d(x2) + d(x3))


def _same_block(i, j, size):
    shift = size.bit_length() - 1
    assert size == 1 << shift
    return (i >> shift) == (j >> shift)


def _sigmoid(x):
    return 0.5 * jnp.tanh(0.5 * x) + 0.5


def _softplus(x):
    return jnp.maximum(x, 0.0) + jnp.log1p(jnp.exp(-jnp.abs(x)))


def _layer_norm(y, g, b):
    mu = jnp.mean(y, axis=-1, keepdims=True)
    yc = y - mu
    var = jnp.mean(yc * yc, axis=-1, keepdims=True)
    return yc * lax.rsqrt(var + LN_EPS) * g + b


def _layer_spec(layer, shape, index_map=None, **kw):
    if index_map is None:
        index_map = lambda *g: (0,) * len(shape)
    return pl.BlockSpec((None,) + tuple(shape), lambda *g: (layer,) + tuple(index_map(*g)), **kw)


def _params(*sem):
    return pltpu.CompilerParams(dimension_semantics=sem, vmem_limit_bytes=VMEM_LIMIT)


def _stream_kernel(lo_ref, hi_ref, o_ref, *, n_lo):
    o_ref[...] = jnp.where(pl.program_id(0) < n_lo, lo_ref[...], hi_ref[...]).astype(o_ref.dtype)


def _bf16_stream(x_lo, x_hi, tm):
    n_lo, n_hi = x_lo.shape[0] // tm, x_hi.shape[0] // tm
    return pl.pallas_call(
        functools.partial(_stream_kernel, n_lo=n_lo),
        grid=(n_lo + n_hi,),
        in_specs=[pl.BlockSpec((tm, D_MODEL), lambda i: (jnp.minimum(i, n_lo - 1), 0)),
                  pl.BlockSpec((tm, D_MODEL), lambda i: (jnp.maximum(i - n_lo, 0), 0))],
        out_specs=pl.BlockSpec((tm, D_MODEL), lambda i: (i, 0)),
        out_shape=jax.ShapeDtypeStruct((x_lo.shape[0] + x_hi.shape[0], D_MODEL), BF16),
        compiler_params=_params("arbitrary"),
        name="bf16_stream",
    )(x_lo, x_hi)


def _proj_kernel(x_ref, wt_ref, o_ref, wb_ref, *, gate):
    @pl.when(pl.program_id(1) == 0)
    def _():
        wb_ref[...] = wt_ref[0].astype(BF16)

    y = lax.dot_general(x_ref[...], wb_ref[...], (((1,), (1,)), ((), ())), preferred_element_type=F32)
    if gate:
        y = _sigmoid(y)
    o_ref[...] = y.astype(o_ref.dtype)


def _proj(x_bf, w_t, layer, col0, n_cols, out_dtype, gate, tm, tn, name):
    t = x_bf.shape[0]
    return pl.pallas_call(
        functools.partial(_proj_kernel, gate=gate),
        grid=(n_cols // tn, t // tm),
        in_specs=[pl.BlockSpec((tm, D_MODEL), lambda j, i: (i, 0)),
                  pl.BlockSpec((pl.Element(1), pl.Element(tn), pl.Element(D_MODEL)),
                               lambda j, i: (layer, pl.multiple_of(col0 + j * tn, 16), 0))],
        out_specs=pl.BlockSpec((tm, tn), lambda j, i: (i, j)),
        out_shape=jax.ShapeDtypeStruct((t, n_cols), out_dtype),
        scratch_shapes=[pltpu.VMEM((tn, D_MODEL), BF16)],
        compiler_params=_params("arbitrary", "arbitrary"),
        name=name,
    )(x_bf, w_t)


def _causal_conv(u, w_ref, prev_fn):
    k = w_ref.shape[0]
    out = u * w_ref[k - 1:k, :]
    for j in range(1, k):
        out = out + prev_fn(j) * w_ref[k - 1 - j:k - j, :]
    return out


def _prompt_prev(u, buf_ref):
    buf_ref[8:, :] = u
    return lambda j: buf_ref[8 - j:8 - j + u.shape[0], :]


def _sample_prev(u, st4):
    t = lax.broadcasted_iota(jnp.int32, u.shape, 0) & (SAMPLE_LEN - 1)

    def prev(j):
        return jnp.where(t >= j, pltpu.roll(u, j, 0), pltpu.roll(st4, GROUP - (SAMPLE_LEN - j), 0))
    return prev


def _pair_iotas():
    row = lax.broadcasted_iota(jnp.int32, (GROUP, 2 * GROUP), 0)
    lane = lax.broadcasted_iota(jnp.int32, (GROUP, 2 * GROUP), 1)
    return row, lane & (GROUP - 1), lane >= GROUP


def _block_diag(y0, y1):
    z = jnp.zeros_like(y0)
    return jnp.concatenate([jnp.concatenate([y0, z], axis=1), jnp.concatenate([z, y1], axis=1)], axis=0)


def _pair_block_diag(y):
    _, _, second = _pair_iotas()
    return jnp.concatenate([jnp.where(second, 0.0, y), jnp.where(second, y, 0.0)], axis=0)


def _tri_inverse_pairs(ms, seq_len):
    row, col, _ = _pair_iotas()
    eye = (row == col).astype(F32)
    base = min(8, seq_len)
    mm = lambda x, y: _dot_multi(x, _pair_block_diag(y), INV_PASSES)
    same = lambda b: _same_block(row, col, b)
    d = [jnp.where(same(base), m, 0.0) for m in ms]
    d2 = [mm(x, x) for x in d]
    xs = [eye - x for x in d]
    if base > 4:
        d4 = [mm(y, y) for y in d2]
    inv = [x + mm(x, y) for x, y in zip(xs, d2)]
    if base > 4:
        inv = [i + mm(i, y) for i, y in zip(inv, d4)]
    b = base * 2
    while b <= seq_len:
        off_diag = same(b) & jnp.logical_not(same(b // 2))
        t = [mm(i, jnp.where(off_diag, m, 0.0)) for i, m in zip(inv, ms)]
        inv = [i - mm(x, i) for i, x in zip(inv, t)]
        b *= 2
    return inv


def _gdn_group(q, k, v, z, logit, a_row, dt_row, nw_row, seq_len, state, og_ref):
    row = lax.broadcasted_iota(jnp.int32, (GROUP, GROUP), 0)
    col = lax.broadcasted_iota(jnp.int32, (GROUP, GROUP), 1)
    same_seq = _same_block(row, col, seq_len)
    lmat = jnp.concatenate([(same_seq & (row >= col)).astype(F32), same_seq.astype(F32)], axis=0).astype(BF16)
    prow, pcol, second = _pair_iotas()
    psame = _same_block(prow, pcol, seq_len)
    causal = psame & (prow >= pcol)
    strict = psame & (prow > pcol)

    subs = range(q.shape[0] // GROUP)
    heads = range(N_HEADS)
    pairs = range(0, N_HEADS, 2)
    units = [(c, h) for c in subs for h in pairs]
    rows = lambda x, c: x[c * GROUP:(c + 1) * GROUP]
    hs = lambda x, c, h: x[c * GROUP:(c + 1) * GROUP, h * HEAD_DIM:(h + 1) * HEAD_DIM]
    lcol = lambda x, h: x[:, LOGIT_A + h:LOGIT_A + h + 1]
    cat = jnp.concatenate

    beta = _sigmoid(logit)
    g = -jnp.exp(a_row) * _softplus(logit + dt_row)
    gc2 = [_dot_exact_lhs(lmat, rows(g, c)) for c in subs]
    gcum = [x[:GROUP] for x in gc2]
    gtot = [x[GROUP:] for x in gc2]
    gcum_t2 = [cat([x, x], axis=0).T for x in gcum]
    e_cum = [jnp.exp(x) for x in gcum]
    e_rem = [jnp.exp(t - x) for t, x in zip(gtot, gcum)]
    e_tot = [jnp.exp(t) for t in gtot]

    qn, kn, bcol, kb = {}, {}, {}, {}
    for c in subs:
        for h in heads:
            qh, kh = hs(q, c, h), hs(k, c, h)
            qn[c, h] = qh * (lax.rsqrt(jnp.sum(qh * qh, axis=-1, keepdims=True) + NORM_EPS) * (HEAD_DIM ** -0.5))
            kn[c, h] = kh * lax.rsqrt(jnp.sum(kh * kh, axis=-1, keepdims=True) + NORM_EPS)
            bcol[c, h] = rows(beta, c)[:, LOGIT_B + h:LOGIT_B + h + 1]
            kb[c, h] = kn[c, h] * bcol[c, h]

    a = {(c, h): _dot_nt(cat([cat([kb[c, h], kb[c, h + 1]], axis=1), cat([qn[c, h], qn[c, h + 1]], axis=1)], axis=0),
                         _block_diag(kn[c, h], kn[c, h + 1])) for c, h in units}
    ms, attn = {}, {}
    for c, h in units:
        gcol = jnp.where(second, lcol(gcum[c], h + 1), lcol(gcum[c], h))
        grow = jnp.where(second[0:1], gcum_t2[c][LOGIT_A + h + 1:LOGIT_A + h + 2, :],
                         gcum_t2[c][LOGIT_A + h:LOGIT_A + h + 1, :])
        dec = jnp.exp(jnp.where(causal, gcol - grow, 0.0))
        ms[c, h] = jnp.where(strict, a[c, h][:GROUP] * dec, 0.0)
        attn[c, h] = jnp.where(causal, a[c, h][GROUP:] * dec, 0.0)
    t_inv = dict(zip(units, _tri_inverse_pairs([ms[un] for un in units], seq_len)))

    u, w, qdec, kdec = {}, {}, {}, {}
    for c, h in units:
        rhs = [cat([hs(v, c, hh) * bcol[c, hh], kb[c, hh] * lcol(e_cum[c], hh)], axis=1) for hh in (h, h + 1)]
        uw = _dot(t_inv[c, h], _block_diag(*rhs))
        for i, hh in enumerate((h, h + 1)):
            u[c, hh] = uw[:, 2 * i * HEAD_DIM:(2 * i + 1) * HEAD_DIM]
            w[c, hh] = uw[:, (2 * i + 1) * HEAD_DIM:(2 * i + 2) * HEAD_DIM]
            qdec[c, hh] = qn[c, hh] * lcol(e_cum[c], hh)
            kdec[c, hh] = kn[c, hh] * lcol(e_rem[c], hh)

    for c in subs:
        ws = [state.read(h, cat([w[c, h], qdec[c, h]], axis=0)) for h in heads]
        v_new = [u[c, h] - ws[h][0] for h in heads]
        vn_bd = {h: _block_diag(v_new[h], v_new[h + 1]) for h in pairs}
        o_pair = {h: cat([ws[h][1], ws[h + 1][1]], axis=1) + _dot(attn[c, h], vn_bd[h]) for h in pairs}
        for h in pairs:
            state.update(h, cat([kdec[c, h], kdec[c, h + 1]], axis=0).T, vn_bd[h],
                         lcol(e_tot[c], h), lcol(e_tot[c], h + 1))
        for h in heads:
            o = o_pair[h - h % 2][:, (h % 2) * HEAD_DIM:(h % 2 + 1) * HEAD_DIM]
            o = o * lax.rsqrt(jnp.mean(o * o, axis=-1, keepdims=True) + NORM_EPS) * nw_row
            zh = hs(z, c, h)
            og_ref[c * GROUP:(c + 1) * GROUP, h * HEAD_DIM:(h + 1) * HEAD_DIM] = (
                o * (zh * _sigmoid(zh))).astype(og_ref.dtype)
    state.finish()


def _mixer_common(x_ref, wl_ref, abc_ref, qkv_ref, z_ref, caw_ref, gcw_ref, par_ref, prev_a, prev_g,
                  seq_len, state, ya_ref, og_ref):
    gb = abc_ref[:, COL_GB:COL_GB + D_CONV]
    gc = abc_ref[:, COL_GC:COL_GC + D_CONV]
    hv = abc_ref[:, COL_H:COL_H + D_CONV]
    u = gc * hv
    conv_a = _causal_conv(u, caw_ref, prev_a(u))
    ya_ref[...] = (gb * conv_a).astype(ya_ref.dtype)

    raw = qkv_ref[...]
    cg = _causal_conv(raw, gcw_ref, prev_g(raw))
    act = cg * _sigmoid(cg)
    q, k, v = act[:, :D_GDN], act[:, D_GDN:2 * D_GDN], act[:, 2 * D_GDN:]
    logit = lax.dot_general(x_ref[...], wl_ref[...], (((1,), (1,)), ((), ())), preferred_element_type=F32)
    _gdn_group(q, k, v, z_ref[...], logit, par_ref[0:1, :], par_ref[1:2, :], par_ref[2:3, :],
               seq_len, state, og_ref)
    return u, raw


def _prompt_mixer_kernel(x_ref, wl_ref, abc_ref, qkv_ref, z_ref, caw_ref, gcw_ref, par_ref,
                         ya_ref, og_ref, utail_ref, gtail_ref, s_ref, ubuf_ref, gbuf_ref):
    @pl.when(pl.program_id(1) == 0)
    def _():
        ubuf_ref[0:8, :] = jnp.zeros((8, D_CONV), F32)
        gbuf_ref[0:8, :] = jnp.zeros((8, 3 * D_GDN), F32)
        s_ref[...] = jnp.zeros_like(s_ref)

    cur = {}

    class State:
        @staticmethod
        def read(h, lhs):
            if h not in cur:
                cur[h] = s_ref[0, h]
            ws = _dot(lhs, cur[h])
            return ws[:GROUP], ws[GROUP:]

        @staticmethod
        def update(h, kd_t, vn_bd, e0, e1):
            kv = _dot(kd_t, vn_bd)
            cur[h] = cur[h] * e0[0:1, :] + kv[:, :HEAD_DIM]
            cur[h + 1] = cur[h + 1] * e1[0:1, :] + kv[:, HEAD_DIM:]

        @staticmethod
        def finish():
            for h, s in cur.items():
                s_ref[0, h] = s

    u, raw = _mixer_common(
        x_ref, wl_ref, abc_ref, qkv_ref, z_ref, caw_ref, gcw_ref, par_ref,
        lambda u: _prompt_prev(u, ubuf_ref), lambda r: _prompt_prev(r, gbuf_ref),
        GROUP, State, ya_ref, og_ref)
    n = u.shape[0]
    ubuf_ref[0:8, :] = u[n - 8:]
    gbuf_ref[0:8, :] = raw[n - 8:]
    utail_ref[0] = u[n - 8:]
    gtail_ref[0] = raw[n - 8:]


def _sample_mixer_kernel(x_ref, wl_ref, abc_ref, qkv_ref, z_ref, caw_ref, gcw_ref, par_ref,
                         sta_ref, stg_ref, s0_ref, *rest):
    ya_ref, og_ref, u_ref, s_ref = rest[-4:]
    row8 = lax.broadcasted_iota(jnp.int32, (8, HEAD_DIM), 0)
    first = row8 < SAMPLE_LEN
    row_bd = lax.broadcasted_iota(jnp.int32, (2 * GROUP, 2 * HEAD_DIM), 0) & (GROUP - 1)

    class State:
        @staticmethod
        def read(h, lhs):
            w, qdec = lhs[:GROUP], lhs[GROUP:]
            ws_w, ws_q = [], []
            for p in range(GROUP // 8):
                slab = jnp.concatenate([w[8 * p:8 * p + 8], qdec[8 * p:8 * p + 8]], axis=0)
                r0 = _dot(slab, s0_ref[2 * p, h])
                r1 = _dot(slab, s0_ref[2 * p + 1, h])
                ws_w.append(jnp.where(first, r0[:8], r1[:8]))
                ws_q.append(jnp.where(first, r0[8:], r1[8:]))
            return jnp.concatenate(ws_w, axis=0), jnp.concatenate(ws_q, axis=0)

        @staticmethod
        def update(h, kd_t, vn_bd, e0, e1):
            for s in range(SEQ_PER_GROUP):
                in_seq = _same_block(row_bd, SAMPLE_LEN * s, SAMPLE_LEN)
                kv = _dot(kd_t, jnp.where(in_seq, vn_bd, 0.0))
                r = SAMPLE_LEN * s
                s_ref[s, h] = s0_ref[s, h] * e0[r:r + 1, :] + kv[:, :HEAD_DIM]
                s_ref[s, h + 1] = s0_ref[s, h + 1] * e1[r:r + 1, :] + kv[:, HEAD_DIM:]

        @staticmethod
        def finish():
            pass

    u, _ = _mixer_common(
        x_ref, wl_ref, abc_ref, qkv_ref, z_ref, caw_ref, gcw_ref, par_ref,
        lambda u: _sample_prev(u, sta_ref[...]), lambda r: _sample_prev(r, stg_ref[...]),
        SAMPLE_LEN, State, ya_ref, og_ref)
    u_ref[...] = u


def _mixer_in_specs(layer, row_block, n_rows=GROUP):
    rows = lambda width, col_block: pl.BlockSpec((n_rows, width), lambda *i: (row_block(*i), col_block))
    return [
        rows(D_MODEL, 0),
        _layer_spec(layer, (LANE, D_MODEL)),
        rows(3 * D_CONV, 0),
        rows(3 * D_GDN, COL_QKV // (3 * D_GDN)),
        rows(D_GDN, COL_Z // D_GDN),
        _layer_spec(layer, (3, D_CONV)),
        _layer_spec(layer, (4, 3 * D_GDN)),
        _layer_spec(layer, (8, LANE)),
    ]


def _prompt_mixer(layer, x_bf, w_logit, proj, caw, gcw, par, batch, seq):
    step = GROUP * PROMPT_SUB
    chunks = seq // step
    rb = lambda b, c: b * chunks + c
    n = batch * seq
    return pl.pallas_call(
        _prompt_mixer_kernel,
        grid=(batch, chunks),
        in_specs=_mixer_in_specs(layer, rb, step),
        out_specs=[
            pl.BlockSpec((step, D_CONV), lambda b, c: (rb(b, c), 0)),
            pl.BlockSpec((step, D_GDN), lambda b, c: (rb(b, c), 0)),
            pl.BlockSpec((1, 8, D_CONV), lambda b, c: (b, 0, 0)),
            pl.BlockSpec((1, 8, 3 * D_GDN), lambda b, c: (b, 0, 0)),
            pl.BlockSpec((1, N_HEADS, HEAD_DIM, HEAD_DIM), lambda b, c: (b, 0, 0, 0)),
        ],
        out_shape=[
            jax.ShapeDtypeStruct((n, D_CONV), BF16),
            jax.ShapeDtypeStruct((n, D_GDN), BF16),
            jax.ShapeDtypeStruct((batch, 8, D_CONV), F32),
            jax.ShapeDtypeStruct((batch, 8, 3 * D_GDN), F32),
            jax.ShapeDtypeStruct((batch, N_HEADS, HEAD_DIM, HEAD_DIM), F32),
        ],
        scratch_shapes=[pltpu.VMEM((step + 8, D_CONV), F32), pltpu.VMEM((step + 8, 3 * D_GDN), F32)],
        compiler_params=_params("arbitrary", "arbitrary"),
        name="prompt_mixer",
    )(x_bf, w_logit, proj, proj, proj, caw, gcw, par)


def _sample_mixer(layer, x_bf, w_logit, proj, caw, gcw, par, sta, stg, s0, s_prev, first_block):
    n_seq = s0.shape[1]
    rb = lambda g: first_block + g
    state_block = (SEQ_PER_GROUP, N_HEADS, HEAD_DIM, HEAD_DIM)
    in_specs = _mixer_in_specs(layer, rb) + [
        _layer_spec(layer, (GROUP, D_CONV), lambda g: (g, 0)),
        _layer_spec(layer, (GROUP, 3 * D_GDN), lambda g: (g, 0)),
        _layer_spec(layer, state_block, lambda g: (g, 0, 0, 0)),
    ]
    args = [x_bf, w_logit, proj, proj, proj, caw, gcw, par, sta, stg, s0]
    aliases = {}
    if s_prev is not None:
        in_specs.append(pl.BlockSpec(memory_space=pl.ANY))
        aliases = {len(args): 3}
        args.append(s_prev)
    return pl.pallas_call(
        _sample_mixer_kernel,
        grid=(n_seq // SEQ_PER_GROUP,),
        in_specs=in_specs,
        out_specs=[
            pl.BlockSpec((GROUP, D_CONV), lambda g: (g, 0)),
            pl.BlockSpec((GROUP, D_GDN), lambda g: (g, 0)),
            pl.BlockSpec((GROUP, D_CONV), lambda g: (g, 0)),
            _layer_spec(layer, state_block, lambda g: (g, 0, 0, 0)),
        ],
        out_shape=[
            jax.ShapeDtypeStruct((n_seq * SAMPLE_LEN, D_CONV), BF16),
            jax.ShapeDtypeStruct((n_seq * SAMPLE_LEN, D_GDN), BF16),
            jax.ShapeDtypeStruct((n_seq * SAMPLE_LEN, D_CONV), F32),
            jax.ShapeDtypeStruct(s0.shape, F32),
        ],
        input_output_aliases=aliases,
        compiler_params=_params("arbitrary"),
        name="sample_mixer",
    )(*args)


def _out_proj_kernel(ya_lo, ya_hi, og_lo, og_hi, gate_ref, x_lo, x_hi, wa_ref, wb_ref, wo_ref, g_ref, b_ref,
                     o_ref, *, n_lo):
    lo = pl.program_id(0) < n_lo
    half = o_ref.shape[0] // 2
    halves = (slice(0, half), slice(half, 2 * half))
    pick = lambda a, b, r: jnp.where(lo, a[r, :], b[r, :])
    dot = lambda a, w_ref: jnp.dot(a, w_ref[...], preferred_element_type=F32)
    y_a = [dot(pick(ya_lo, ya_hi, r), wa_ref) for r in halves]
    y_b = [dot(pick(og_lo, og_hi, r), wb_ref) for r in halves]
    merged = [gate_ref[r, :D_MODEL].astype(F32) * a + gate_ref[r, D_MODEL:].astype(F32) * b
              for r, a, b in zip(halves, y_a, y_b)]
    z = [dot(m.astype(BF16), wo_ref) for m in merged]
    for r, zz in zip(halves, z):
        o_ref[r, :] = _layer_norm(ALPHA * pick(x_lo, x_hi, r) + zz, g_ref[...], b_ref[...])


def _out_proj(layer, ya_p, ya_s, og_p, og_s, gates, x_lo, x_hi, hi_block0, wa, wb, wo, g, b, tm):
    t = gates.shape[0]
    n_lo = ya_p.shape[0] // tm
    lo = lambda width: pl.BlockSpec((tm, width), lambda i: (jnp.minimum(i, n_lo - 1), 0))
    hi = lambda width, block0=0: pl.BlockSpec((tm, width), lambda i: (jnp.maximum(i - n_lo, 0) + block0, 0))
    const = lambda shape: _layer_spec(layer, shape, pipeline_mode=pl.Buffered(1))
    return pl.pallas_call(
        functools.partial(_out_proj_kernel, n_lo=n_lo),
        grid=(t // tm,),
        in_specs=[
            lo(D_CONV), hi(D_CONV), lo(D_GDN), hi(D_GDN),
            pl.BlockSpec((tm, 2 * D_MODEL), lambda i: (i, 0)),
            lo(D_MODEL), hi(D_MODEL, hi_block0),
            const((D_CONV, D_MODEL)), const((D_GDN, D_MODEL)), const((D_MODEL, D_MODEL)),
            const((1, D_MODEL)), const((1, D_MODEL)),
        ],
        out_specs=pl.BlockSpec((tm, D_MODEL), lambda i: (i, 0)),
        out_shape=jax.ShapeDtypeStruct((t, D_MODEL), F32),
        compiler_params=_params("arbitrary"),
        name="out_proj",
    )(ya_p, ya_s, og_p, og_s, gates, x_lo, x_hi, wa, wb, wo, g, b)


def _mlp_kernel(x_ref, wu_ref, wd_ref, g_ref, b_ref, o0_ref, o1_ref, xb_ref, acc_ref, *, n_lo):
    i, j = pl.program_id(0), pl.program_id(1)

    @pl.when(j == 0)
    def _():
        xb_ref[...] = x_ref[...].astype(BF16)
        acc_ref[...] = jnp.zeros_like(acc_ref)

    hid = jnp.maximum(jnp.dot(xb_ref[...], wu_ref[...], preferred_element_type=F32), 0.0)
    acc_ref[...] += jnp.dot((hid * hid).astype(BF16), wd_ref[...], preferred_element_type=F32)

    @pl.when(j == pl.num_programs(1) - 1)
    def _():
        y = _layer_norm(ALPHA * x_ref[...] + acc_ref[...], g_ref[...], b_ref[...])
        if n_lo is None:
            o0_ref[...] = y
            o1_ref[...] = y.astype(BF16)
        else:
            @pl.when(i < n_lo)
            def _():
                o0_ref[...] = y

            @pl.when(i >= n_lo)
            def _():
                o1_ref[...] = y


def _mlp(layer, x, wu, wd, g, b, tm, tf, n_prompt=None):
    t = x.shape[0]
    if n_prompt is None:
        n_lo = None
        out_specs = [pl.BlockSpec((tm, D_MODEL), lambda i, j: (i, 0)),
                     pl.BlockSpec((tm, D_MODEL), lambda i, j: (i, 0))]
        out_shape = [jax.ShapeDtypeStruct((t, D_MODEL), F32), jax.ShapeDtypeStruct((t, D_MODEL), BF16)]
    else:
        n_lo = n_prompt // tm
        out_specs = [pl.BlockSpec((tm, D_MODEL), lambda i, j: (jnp.minimum(i, n_lo - 1), 0)),
                     pl.BlockSpec((tm, D_MODEL), lambda i, j: (jnp.maximum(i - n_lo, 0), 0))]
        out_shape = [jax.ShapeDtypeStruct((n_prompt, D_MODEL), F32),
                     jax.ShapeDtypeStruct((t - n_prompt, D_MODEL), F32)]
    return pl.pallas_call(
        functools.partial(_mlp_kernel, n_lo=n_lo),
        grid=(t // tm, D_FF // tf),
        in_specs=[
            pl.BlockSpec((tm, D_MODEL), lambda i, j: (i, 0)),
            _layer_spec(layer, (D_MODEL, tf), lambda i, j: (0, j)),
            _layer_spec(layer, (tf, D_MODEL), lambda i, j: (j, 0)),
            _layer_spec(layer, (1, D_MODEL)),
            _layer_spec(layer, (1, D_MODEL)),
        ],
        out_specs=out_specs,
        out_shape=out_shape,
        scratch_shapes=[pltpu.VMEM((tm, D_MODEL), BF16), pltpu.VMEM((tm, D_MODEL), F32)],
        compiler_params=_params("arbitrary", "arbitrary"),
        name="mlp",
    )(x, wu, wd, g, b)


def _front_pad_state(st):
    d, n, k1, c = st.shape
    return jnp.pad(st, ((0, 0), (0, 0), (SAMPLE_LEN - k1, 0), (0, 0))).reshape(d, n * SAMPLE_LEN, c)


def kernel(x_prompt, x_sample, state_conv_a, state_gdn_conv, state_gdn, w_in, conv_a_w, gdn_conv_w, a_log,
           dt_bias, gdn_norm_w, w_a_out, w_b_out, w_o, ln1_g, ln1_b, w_up, w_down, ln2_g, ln2_b):
    batch, seq, _ = x_prompt.shape
    n_seq = x_sample.shape[0]
    n_prompt = batch * seq
    n_sample = n_seq * SAMPLE_LEN
    tm_proj, tm_out, tm_mlp = 1088, 256, 512
    assert x_sample.shape[1] == SAMPLE_LEN and seq % (GROUP * PROMPT_SUB) == 0 and n_seq % SEQ_PER_GROUP == 0
    assert n_prompt % tm_out == 0 and n_sample % tm_out == 0 and n_prompt % tm_mlp == 0 and n_sample == tm_mlp

    xp = x_prompt.reshape(n_prompt, D_MODEL)
    xs = x_sample.reshape(n_sample, D_MODEL)
    x_bf = _bf16_stream(xp, xs, tm_mlp)
    x_lo, x_hi, hi_block0 = xp, xs, 0

    w_t = jnp.swapaxes(w_in, 1, 2)
    w_logit = jnp.pad(w_t[:, COL_LOGIT:COL_GATE], ((0, 0), (0, LANE - 2 * N_HEADS), (0, 0))).astype(BF16)
    wa, wb, wo = w_a_out.astype(BF16), w_b_out.astype(BF16), w_o.astype(BF16)
    wu, wd = w_up.astype(BF16), w_down.astype(BF16)
    par = jnp.zeros((DEPTH, 8, LANE), F32)
    par = par.at[:, 0, LOGIT_A:LOGIT_A + N_HEADS].set(a_log.astype(F32))
    par = par.at[:, 1, LOGIT_A:LOGIT_A + N_HEADS].set(dt_bias.astype(F32))
    par = par.at[:, 2, :].set(gdn_norm_w.astype(F32))
    sta, stg = _front_pad_state(state_conv_a), _front_pad_state(state_gdn_conv)
    row3 = lambda a: a.reshape(DEPTH, 1, D_MODEL)

    ca_p, cg_p, sg_p, ca_s, cg_s = [], [], [], [], []
    sg_s = None
    for l in range(DEPTH):
        proj = _proj(x_bf, w_t, l, 0, N_MIX, F32, False, tm_proj, 1024, "proj")
        gates = _proj(x_bf, w_t, l, COL_GATE, 2 * D_MODEL, BF16, True, tm_proj, 1024, "gates")

        ya_p, og_p, utail, gtail, s_p = _prompt_mixer(l, x_bf, w_logit, proj, conv_a_w, gdn_conv_w, par, batch, seq)
        ya_s, og_s, u_s, sg_s = _sample_mixer(l, x_bf, w_logit, proj, conv_a_w, gdn_conv_w, par, sta, stg,
                                              state_gdn, sg_s, n_prompt // GROUP)

        x1 = _out_proj(l, ya_p, ya_s, og_p, og_s, gates, x_lo, x_hi, hi_block0, wa, wb, wo,
                       row3(ln1_g), row3(ln1_b), tm_out)
        if l < DEPTH - 1:
            x, x_bf = _mlp(l, x1, wu, wd, row3(ln2_g), row3(ln2_b), tm_mlp, 1024)
            x_lo, x_hi, hi_block0 = x, x, n_prompt // tm_out
        else:
            y_p, y_s = _mlp(l, x1, wu, wd, row3(ln2_g), row3(ln2_b), tm_mlp, 1024, n_prompt=n_prompt)

        ca_p.append(utail[:, 6:8])
        cg_p.append(gtail[:, 5:8])
        sg_p.append(s_p)
        ca_s.append(u_s.reshape(n_seq, SAMPLE_LEN, D_CONV)[:, 2:4])
        cg_s.append(proj[n_prompt:, COL_QKV:COL_QKV + 3 * D_GDN].reshape(n_seq, SAMPLE_LEN, 3 * D_GDN)[:, 1:4])

    return (y_p.reshape(batch, seq, D_MODEL), y_s.reshape(n_seq, SAMPLE_LEN, D_MODEL),
            jnp.stack(ca_p), jnp.stack(cg_p), jnp.stack(sg_p), jnp.stack(ca_s), jnp.stack(cg_s), sg_s)
```

```python
import functools

import jax
import jax.numpy as jnp
from jax import lax
from jax.experimental import pallas as pl
from jax.experimental.pallas import tpu as pltpu

F32 = jnp.float32
BF16 = jnp.bfloat16

D_MODEL = 2048
DEPTH = 2
D_CONV = D_MODEL // 2
N_HEADS = 8
HEAD_DIM = 128
D_GDN = N_HEADS * HEAD_DIM
D_FF = 4 * D_MODEL
ALPHA = (2 * DEPTH) ** 0.25
LN_EPS = 1e-5
NORM_EPS = 1e-6

COL_GB, COL_GC, COL_H = 0, D_CONV, 2 * D_CONV
COL_QKV = 3 * D_CONV
COL_Z = COL_QKV + 3 * D_GDN
N_MIX = COL_Z + D_GDN
COL_LOGIT = N_MIX
COL_GATE = COL_LOGIT + 2 * N_HEADS
LANE = 128
LOGIT_B, LOGIT_A = 0, N_HEADS

GROUP = 64
PROMPT_SUB = 2
SAMPLE_LEN = 4
SEQ_PER_GROUP = GROUP // SAMPLE_LEN
INV_PASSES = 1

VMEM_LIMIT = 56 * 1024 * 1024


def _dot(a, b):
    return jnp.dot(a.astype(BF16), b.astype(BF16), preferred_element_type=F32)


def _dot_nt(a, b):
    return lax.dot_general(a.astype(BF16), b.astype(BF16), (((1,), (1,)), ((), ())),
                           preferred_element_type=F32)


def _split2(a):
    hi = a.astype(BF16)
    lo = (a - hi.astype(F32)).astype(BF16)
    return hi, lo


def _dot_multi(a, b, passes):
    if passes == 1:
        return _dot(a, b)
    ah, al = _split2(a)
    bh, bl = _split2(b)
    d = lambda x, y: jnp.dot(x, y, preferred_element_type=F32)
    return d(ah, bh) + (d(ah, bl) + d(al, bh))


def _dot_exact_lhs(m01, x):
    x1 = x.astype(BF16)
    r1 = x - x1.astype(F32)
    x2 = r1.astype(BF16)
    x3 = (r1 - x2.astype(F32)).astype(BF16)
    d = lambda y: jnp.dot(m01, y, preferred_element_type=F32)
    return d(x1) + (d(x2) + d(x3))


def _same_block(i, j, size):
    shift = size.bit_length() - 1
    assert size == 1 << shift
    return (i >> shift) == (j >> shift)


def _sigmoid(x):
    return 0.5 * jnp.tanh(0.5 * x) + 0.5


def _softplus(x):
    return jnp.maximum(x, 0.0) + jnp.log1p(jnp.exp(-jnp.abs(x)))


def _layer_norm(y, g, b):
    mu = jnp.mean(y, axis=-1, keepdims=True)
    yc = y - mu
    var = jnp.mean(yc * yc, axis=-1, keepdims=True)
    return yc * lax.rsqrt(var + LN_EPS) * g + b


def _layer_spec(layer, shape, index_map=None, **kw):
    if index_map is None:
        index_map = lambda *g: (0,) * len(shape)
    return pl.BlockSpec((None,) + tuple(shape), lambda *g: (layer,) + tuple(index_map(*g)), **kw)


def _params(*sem):
    return pltpu.CompilerParams(dimension_semantics=sem, vmem_limit_bytes=VMEM_LIMIT)


def _stream_kernel(lo_ref, hi_ref, o_ref, *, n_lo):
    o_ref[...] = jnp.where(pl.program_id(0) < n_lo, lo_ref[...], hi_ref[...]).astype(o_ref.dtype)


def _bf16_stream(x_lo, x_hi, tm):
    n_lo, n_hi = x_lo.shape[0] // tm, x_hi.shape[0] // tm
    return pl.pallas_call(
        functools.partial(_stream_kernel, n_lo=n_lo),
        grid=(n_lo + n_hi,),
        in_specs=[pl.BlockSpec((tm, D_MODEL), lambda i: (jnp.minimum(i, n_lo - 1), 0)),
                  pl.BlockSpec((tm, D_MODEL), lambda i: (jnp.maximum(i - n_lo, 0), 0))],
        out_specs=pl.BlockSpec((tm, D_MODEL), lambda i: (i, 0)),
        out_shape=jax.ShapeDtypeStruct((x_lo.shape[0] + x_hi.shape[0], D_MODEL), BF16),
        compiler_params=_params("arbitrary"),
        name="bf16_stream",
    )(x_lo, x_hi)


def _proj_kernel(x_ref, wt_ref, *rest, gate, n_cast):
    cast_in, o_ref, cast_out, wb_ref = rest[:n_cast], rest[n_cast], rest[n_cast + 1:2 * n_cast + 1], rest[-1]

    @pl.when(pl.program_id(1) == 0)
    def _():
        wb_ref[...] = wt_ref[0].astype(BF16)

    y = lax.dot_general(x_ref[...], wb_ref[...], (((1,), (1,)), ((), ())), preferred_element_type=F32)
    if gate:
        y = _sigmoid(y)
    o_ref[...] = y.astype(o_ref.dtype)
    for src, dst in zip(cast_in, cast_out):
        dst[...] = src[...].astype(dst.dtype)


def _proj(x_bf, w_t, layer, col0, n_cols, out_dtype, gate, tm, tn, name, cast=()):
    t = x_bf.shape[0]
    n_j, n_i = n_cols // tn, t // tm
    step = lambda j, i: j * n_i + i
    rows = [w.shape[1] // (n_j * n_i) for w in cast]
    assert all(r % 16 == 0 and r * n_j * n_i == w.shape[1] for r, w in zip(rows, cast))
    outs = pl.pallas_call(
        functools.partial(_proj_kernel, gate=gate, n_cast=len(cast)),
        grid=(n_j, n_i),
        in_specs=[pl.BlockSpec((tm, D_MODEL), lambda j, i: (i, 0)),
                  pl.BlockSpec((pl.Element(1), pl.Element(tn), pl.Element(D_MODEL)),
                               lambda j, i: (layer, pl.multiple_of(col0 + j * tn, 16), 0))]
        + [_layer_spec(layer, (r, w.shape[2]), lambda j, i: (step(j, i), 0)) for r, w in zip(rows, cast)],
        out_specs=[pl.BlockSpec((tm, tn), lambda j, i: (i, j))]
        + [pl.BlockSpec((r, w.shape[2]), lambda j, i: (step(j, i), 0)) for r, w in zip(rows, cast)],
        out_shape=[jax.ShapeDtypeStruct((t, n_cols), out_dtype)]
        + [jax.ShapeDtypeStruct(w.shape[1:], BF16) for w in cast],
        scratch_shapes=[pltpu.VMEM((tn, D_MODEL), BF16)],
        compiler_params=_params("arbitrary", "arbitrary"),
        name=name,
    )(x_bf, w_t, *cast)
    return outs if cast else outs[0]


def _causal_conv(u, w_ref, prev_fn):
    k = w_ref.shape[0]
    out = u * w_ref[k - 1:k, :]
    for j in range(1, k):
        out = out + prev_fn(j) * w_ref[k - 1 - j:k - j, :]
    return out


CONV_PHASES = 4


def _prompt_conv(u, w_ref, buf_ref, out_ref, act=None):
    k = w_ref.shape[0]
    n, c = u.shape
    slabs = c // LANE
    for s in range(slabs):
        buf_ref[s, 8:, :] = u[:, s * LANE:(s + 1) * LANE]
    for s in range(slabs):
        w = [w_ref[j:j + 1, s * LANE:(s + 1) * LANE] for j in range(k)]
        for b in range(CONV_PHASES):
            acc = None
            for j in range(k):
                tap = buf_ref[s, pl.ds(8 - (k - 1) + j + b, n // CONV_PHASES, stride=CONV_PHASES), :]
                acc = tap * w[j] if acc is None else acc + tap * w[j]
            out_ref[s, pl.ds(b, n // CONV_PHASES, stride=CONV_PHASES), :] = acc if act is None else act(acc)
    for s in range(slabs):
        buf_ref[s, 0:8, :] = buf_ref[s, n:n + 8, :]
    return jnp.concatenate([out_ref[s] for s in range(slabs)], axis=1)


def _sample_prev(u, st4):
    t = lax.broadcasted_iota(jnp.int32, u.shape, 0) & (SAMPLE_LEN - 1)

    def prev(j):
        return jnp.where(t >= j, pltpu.roll(u, j, 0), pltpu.roll(st4, GROUP - (SAMPLE_LEN - j), 0))
    return prev


def _pair_iotas():
    row = lax.broadcasted_iota(jnp.int32, (GROUP, 2 * GROUP), 0)
    lane = lax.broadcasted_iota(jnp.int32, (GROUP, 2 * GROUP), 1)
    return row, lane & (GROUP - 1), lane >= GROUP


def _block_diag(y0, y1):
    z = jnp.zeros_like(y0)
    return jnp.concatenate([jnp.concatenate([y0, z], axis=1), jnp.concatenate([z, y1], axis=1)], axis=0)


def _pair_block_diag(y):
    _, _, second = _pair_iotas()
    return jnp.concatenate([jnp.where(second, 0.0, y), jnp.where(second, y, 0.0)], axis=0)


def _tri_inverse_pairs(ms, seq_len):
    row, col, _ = _pair_iotas()
    eye = (row == col).astype(F32)
    base = min(8, seq_len)
    mm = lambda x, y: _dot_multi(x, _pair_block_diag(y), INV_PASSES)
    same = lambda b: _same_block(row, col, b)
    d = [jnp.where(same(base), m, 0.0) for m in ms]
    d2 = [mm(x, x) for x in d]
    xs = [eye - x for x in d]
    if base > 4:
        d4 = [mm(y, y) for y in d2]
    inv = [x + mm(x, y) for x, y in zip(xs, d2)]
    if base > 4:
        inv = [i + mm(i, y) for i, y in zip(inv, d4)]
    b = base * 2
    while b <= seq_len:
        off_diag = same(b) & jnp.logical_not(same(b // 2))
        t = [mm(i, jnp.where(off_diag, m, 0.0)) for i, m in zip(inv, ms)]
        inv = [i - mm(x, i) for i, x in zip(inv, t)]
        b *= 2
    return inv


def _gdn_group(q, k, v, z, logit, a_row, dt_row, nw_row, seq_len, state, og_ref):
    row = lax.broadcasted_iota(jnp.int32, (GROUP, GROUP), 0)
    col = lax.broadcasted_iota(jnp.int32, (GROUP, GROUP), 1)
    same_seq = _same_block(row, col, seq_len)
    lmat = jnp.concatenate([(same_seq & (row >= col)).astype(F32), same_seq.astype(F32)], axis=0).astype(BF16)
    prow, pcol, second = _pair_iotas()
    psame = _same_block(prow, pcol, seq_len)
    causal = psame & (prow >= pcol)
    strict = psame & (prow > pcol)

    subs = range(q.shape[0] // GROUP)
    heads = range(N_HEADS)
    pairs = range(0, N_HEADS, 2)
    units = [(c, h) for c in subs for h in pairs]
    rows = lambda x, c: x[c * GROUP:(c + 1) * GROUP]
    hs = lambda x, c, h: x[c * GROUP:(c + 1) * GROUP, h * HEAD_DIM:(h + 1) * HEAD_DIM]
    lcol = lambda x, h: x[:, LOGIT_A + h:LOGIT_A + h + 1]
    cat = jnp.concatenate

    beta = _sigmoid(logit)
    g = -jnp.exp(a_row) * _softplus(logit + dt_row)
    gc2 = [_dot_exact_lhs(lmat, rows(g, c)) for c in subs]
    gcum = [x[:GROUP] for x in gc2]
    gtot = [x[GROUP:] for x in gc2]
    gcum_t2 = [cat([x, x], axis=0).T for x in gcum]
    e_cum = [jnp.exp(x) for x in gcum]
    e_rem = [jnp.exp(t - x) for t, x in zip(gtot, gcum)]
    e_tot = [jnp.exp(t) for t in gtot]

    qn, kn, bcol, kb = {}, {}, {}, {}
    for c in subs:
        for h in heads:
            qh, kh = hs(q, c, h), hs(k, c, h)
            qn[c, h] = qh * (lax.rsqrt(jnp.sum(qh * qh, axis=-1, keepdims=True) + NORM_EPS) * (HEAD_DIM ** -0.5))
            kn[c, h] = kh * lax.rsqrt(jnp.sum(kh * kh, axis=-1, keepdims=True) + NORM_EPS)
            bcol[c, h] = rows(beta, c)[:, LOGIT_B + h:LOGIT_B + h + 1]
            kb[c, h] = kn[c, h] * bcol[c, h]

    a = {(c, h): _dot_nt(cat([cat([kb[c, h], kb[c, h + 1]], axis=1), cat([qn[c, h], qn[c, h + 1]], axis=1)], axis=0),
                         _block_diag(kn[c, h], kn[c, h + 1])) for c, h in units}
    ms, attn = {}, {}
    for c, h in units:
        gcol = jnp.where(second, lcol(gcum[c], h + 1), lcol(gcum[c], h))
        grow = jnp.where(second[0:1], gcum_t2[c][LOGIT_A + h + 1:LOGIT_A + h + 2, :],
                         gcum_t2[c][LOGIT_A + h:LOGIT_A + h + 1, :])
        dec = jnp.exp(jnp.where(causal, gcol - grow, 0.0))
        ms[c, h] = jnp.where(strict, a[c, h][:GROUP] * dec, 0.0)
        attn[c, h] = jnp.where(causal, a[c, h][GROUP:] * dec, 0.0)
    t_inv = dict(zip(units, _tri_inverse_pairs([ms[un] for un in units], seq_len)))

    u, w, qdec, kdec = {}, {}, {}, {}
    for c, h in units:
        rhs = [cat([hs(v, c, hh) * bcol[c, hh], kb[c, hh] * lcol(e_cum[c], hh)], axis=1) for hh in (h, h + 1)]
        uw = _dot(t_inv[c, h], _block_diag(*rhs))
        for i, hh in enumerate((h, h + 1)):
            u[c, hh] = uw[:, 2 * i * HEAD_DIM:(2 * i + 1) * HEAD_DIM]
            w[c, hh] = uw[:, (2 * i + 1) * HEAD_DIM:(2 * i + 2) * HEAD_DIM]
            qdec[c, hh] = qn[c, hh] * lcol(e_cum[c], hh)
            kdec[c, hh] = kn[c, hh] * lcol(e_rem[c], hh)

    for c in subs:
        ws = [state.read(h, cat([w[c, h], qdec[c, h]], axis=0)) for h in heads]
        v_new = [u[c, h] - ws[h][0] for h in heads]
        vn_bd = {h: _block_diag(v_new[h], v_new[h + 1]) for h in pairs}
        o_pair = {h: cat([ws[h][1], ws[h + 1][1]], axis=1) + _dot(attn[c, h], vn_bd[h]) for h in pairs}
        for h in pairs:
            state.update(h, cat([kdec[c, h], kdec[c, h + 1]], axis=0).T, vn_bd[h],
                         lcol(e_tot[c], h), lcol(e_tot[c], h + 1))
        for h in heads:
            o = o_pair[h - h % 2][:, (h % 2) * HEAD_DIM:(h % 2 + 1) * HEAD_DIM]
            o = o * lax.rsqrt(jnp.mean(o * o, axis=-1, keepdims=True) + NORM_EPS) * nw_row
            zh = hs(z, c, h)
            og_ref[c * GROUP:(c + 1) * GROUP, h * HEAD_DIM:(h + 1) * HEAD_DIM] = (
                o * (zh * _sigmoid(zh))).astype(og_ref.dtype)
    state.finish()


def _silu(x):
    return x * _sigmoid(x)


def _mixer_common(x_ref, wl_ref, abc_ref, qkv_ref, z_ref, par_ref, conv_a, conv_g, seq_len, state, ya_ref, og_ref):
    gb = abc_ref[:, COL_GB:COL_GB + D_CONV]
    gc = abc_ref[:, COL_GC:COL_GC + D_CONV]
    hv = abc_ref[:, COL_H:COL_H + D_CONV]
    u = gc * hv
    ya_ref[...] = (gb * conv_a(u)).astype(ya_ref.dtype)

    raw = qkv_ref[...]
    act = conv_g(raw)
    q, k, v = act[:, :D_GDN], act[:, D_GDN:2 * D_GDN], act[:, 2 * D_GDN:]
    logit = lax.dot_general(x_ref[...], wl_ref[...], (((1,), (1,)), ((), ())), preferred_element_type=F32)
    _gdn_group(q, k, v, z_ref[...], logit, par_ref[0:1, :], par_ref[1:2, :], par_ref[2:3, :],
               seq_len, state, og_ref)
    return u, raw


def _prompt_mixer_kernel(x_ref, wl_ref, abc_ref, qkv_ref, z_ref, caw_ref, gcw_ref, par_ref,
                         ya_ref, og_ref, utail_ref, gtail_ref, s_ref, ubuf_ref, uout_ref, gbuf_ref, gout_ref):
    @pl.when(pl.program_id(1) == 0)
    def _():
        ubuf_ref[:, 0:8, :] = jnp.zeros((D_CONV // LANE, 8, LANE), F32)
        gbuf_ref[:, 0:8, :] = jnp.zeros((3 * D_GDN // LANE, 8, LANE), F32)
        s_ref[...] = jnp.zeros_like(s_ref)

    cur = {}

    class State:
        @staticmethod
        def read(h, lhs):
            if h not in cur:
                cur[h] = s_ref[0, h]
            ws = _dot(lhs, cur[h])
            return ws[:GROUP], ws[GROUP:]

        @staticmethod
        def update(h, kd_t, vn_bd, e0, e1):
            kv = _dot(kd_t, vn_bd)
            cur[h] = cur[h] * e0[0:1, :] + kv[:, :HEAD_DIM]
            cur[h + 1] = cur[h + 1] * e1[0:1, :] + kv[:, HEAD_DIM:]

        @staticmethod
        def finish():
            for h, s in cur.items():
                s_ref[0, h] = s

    u, raw = _mixer_common(
        x_ref, wl_ref, abc_ref, qkv_ref, z_ref, par_ref,
        lambda u: _prompt_conv(u, caw_ref, ubuf_ref, uout_ref),
        lambda r: _prompt_conv(r, gcw_ref, gbuf_ref, gout_ref, _silu),
        GROUP, State, ya_ref, og_ref)
    n = u.shape[0]
    utail_ref[0] = u[n - 8:]
    gtail_ref[0] = raw[n - 8:]


def _sample_mixer_kernel(x_ref, wl_ref, abc_ref, qkv_ref, z_ref, caw_ref, gcw_ref, par_ref,
                         sta_ref, stg_ref, s0_ref, *rest):
    ya_ref, og_ref, u_ref, s_ref = rest[-4:]
    row8 = lax.broadcasted_iota(jnp.int32, (8, HEAD_DIM), 0)
    first = row8 < SAMPLE_LEN
    row_bd = lax.broadcasted_iota(jnp.int32, (2 * GROUP, 2 * HEAD_DIM), 0) & (GROUP - 1)

    class State:
        @staticmethod
        def read(h, lhs):
            w, qdec = lhs[:GROUP], lhs[GROUP:]
            ws_w, ws_q = [], []
            for p in range(GROUP // 8):
                slab = jnp.concatenate([w[8 * p:8 * p + 8], qdec[8 * p:8 * p + 8]], axis=0)
                r0 = _dot(slab, s0_ref[2 * p, h])
                r1 = _dot(slab, s0_ref[2 * p + 1, h])
                ws_w.append(jnp.where(first, r0[:8], r1[:8]))
                ws_q.append(jnp.where(first, r0[8:], r1[8:]))
            return jnp.concatenate(ws_w, axis=0), jnp.concatenate(ws_q, axis=0)

        @staticmethod
        def update(h, kd_t, vn_bd, e0, e1):
            for s in range(SEQ_PER_GROUP):
                in_seq = _same_block(row_bd, SAMPLE_LEN * s, SAMPLE_LEN)
                kv = _dot(kd_t, jnp.where(in_seq, vn_bd, 0.0))
                r = SAMPLE_LEN * s
                s_ref[s, h] = s0_ref[s, h] * e0[r:r + 1, :] + kv[:, :HEAD_DIM]
                s_ref[s, h + 1] = s0_ref[s, h + 1] * e1[r:r + 1, :] + kv[:, HEAD_DIM:]

        @staticmethod
        def finish():
            pass

    u, _ = _mixer_common(
        x_ref, wl_ref, abc_ref, qkv_ref, z_ref, par_ref,
        lambda u: _causal_conv(u, caw_ref, _sample_prev(u, sta_ref[...])),
        lambda r: _silu(_causal_conv(r, gcw_ref, _sample_prev(r, stg_ref[...]))),
        SAMPLE_LEN, State, ya_ref, og_ref)
    u_ref[...] = u


def _mixer_in_specs(layer, row_block, n_rows=GROUP):
    rows = lambda width, col_block: pl.BlockSpec((n_rows, width), lambda *i: (row_block(*i), col_block))
    return [
        rows(D_MODEL, 0),
        _layer_spec(layer, (LANE, D_MODEL)),
        rows(3 * D_CONV, 0),
        rows(3 * D_GDN, COL_QKV // (3 * D_GDN)),
        rows(D_GDN, COL_Z // D_GDN),
        _layer_spec(layer, (3, D_CONV)),
        _layer_spec(layer, (4, 3 * D_GDN)),
        _layer_spec(layer, (8, LANE)),
    ]


def _prompt_mixer(layer, x_bf, w_logit, proj, caw, gcw, par, batch, seq):
    step = GROUP * PROMPT_SUB
    chunks = seq // step
    rb = lambda b, c: b * chunks + c
    n = batch * seq
    return pl.pallas_call(
        _prompt_mixer_kernel,
        grid=(batch, chunks),
        in_specs=_mixer_in_specs(layer, rb, step),
        out_specs=[
            pl.BlockSpec((step, D_CONV), lambda b, c: (rb(b, c), 0)),
            pl.BlockSpec((step, D_GDN), lambda b, c: (rb(b, c), 0)),
            pl.BlockSpec((1, 8, D_CONV), lambda b, c: (b, 0, 0)),
            pl.BlockSpec((1, 8, 3 * D_GDN), lambda b, c: (b, 0, 0)),
            pl.BlockSpec((1, N_HEADS, HEAD_DIM, HEAD_DIM), lambda b, c: (b, 0, 0, 0)),
        ],
        out_shape=[
            jax.ShapeDtypeStruct((n, D_CONV), BF16),
            jax.ShapeDtypeStruct((n, D_GDN), BF16),
            jax.ShapeDtypeStruct((batch, 8, D_CONV), F32),
            jax.ShapeDtypeStruct((batch, 8, 3 * D_GDN), F32),
            jax.ShapeDtypeStruct((batch, N_HEADS, HEAD_DIM, HEAD_DIM), F32),
        ],
        scratch_shapes=[pltpu.VMEM((D_CONV // LANE, step + 8, LANE), F32),
                        pltpu.VMEM((D_CONV // LANE, step, LANE), F32),
                        pltpu.VMEM((3 * D_GDN // LANE, step + 8, LANE), F32),
                        pltpu.VMEM((3 * D_GDN // LANE, step, LANE), F32)],
        compiler_params=_params("arbitrary", "arbitrary"),
        name="prompt_mixer",
    )(x_bf, w_logit, proj, proj, proj, caw, gcw, par)


def _sample_mixer(layer, x_bf, w_logit, proj, caw, gcw, par, sta, stg, s0, s_prev, first_block):
    n_seq = s0.shape[1]
    rb = lambda g: first_block + g
    state_block = (SEQ_PER_GROUP, N_HEADS, HEAD_DIM, HEAD_DIM)
    in_specs = _mixer_in_specs(layer, rb) + [
        _layer_spec(layer, (GROUP, D_CONV), lambda g: (g, 0)),
        _layer_spec(layer, (GROUP, 3 * D_GDN), lambda g: (g, 0)),
        _layer_spec(layer, state_block, lambda g: (g, 0, 0, 0)),
    ]
    args = [x_bf, w_logit, proj, proj, proj, caw, gcw, par, sta, stg, s0]
    aliases = {}
    if s_prev is not None:
        in_specs.append(pl.BlockSpec(memory_space=pl.ANY))
        aliases = {len(args): 3}
        args.append(s_prev)
    return pl.pallas_call(
        _sample_mixer_kernel,
        grid=(n_seq // SEQ_PER_GROUP,),
        in_specs=in_specs,
        out_specs=[
            pl.BlockSpec((GROUP, D_CONV), lambda g: (g, 0)),
            pl.BlockSpec((GROUP, D_GDN), lambda g: (g, 0)),
            pl.BlockSpec((GROUP, D_CONV), lambda g: (g, 0)),
            _layer_spec(layer, state_block, lambda g: (g, 0, 0, 0)),
        ],
        out_shape=[
            jax.ShapeDtypeStruct((n_seq * SAMPLE_LEN, D_CONV), BF16),
            jax.ShapeDtypeStruct((n_seq * SAMPLE_LEN, D_GDN), BF16),
            jax.ShapeDtypeStruct((n_seq * SAMPLE_LEN, D_CONV), F32),
            jax.ShapeDtypeStruct(s0.shape, F32),
        ],
        input_output_aliases=aliases,
        compiler_params=_params("arbitrary"),
        name="sample_mixer",
    )(*args)


def _out_proj_kernel(ya_lo, ya_hi, og_lo, og_hi, gate_ref, x_lo, x_hi, wa_ref, wb_ref, wo_ref, g_ref, b_ref,
                     o_ref, *, n_lo):
    lo = pl.program_id(0) < n_lo
    half = o_ref.shape[0] // 2
    halves = (slice(0, half), slice(half, 2 * half))
    pick = lambda a, b, r: jnp.where(lo, a[r, :], b[r, :])
    dot = lambda a, w_ref: jnp.dot(a, w_ref[...], preferred_element_type=F32)
    y_a = [dot(pick(ya_lo, ya_hi, r), wa_ref) for r in halves]
    y_b = [dot(pick(og_lo, og_hi, r), wb_ref) for r in halves]
    merged = [gate_ref[r, :D_MODEL].astype(F32) * a + gate_ref[r, D_MODEL:].astype(F32) * b
              for r, a, b in zip(halves, y_a, y_b)]
    z = [dot(m.astype(BF16), wo_ref) for m in merged]
    for r, zz in zip(halves, z):
        o_ref[r, :] = _layer_norm(ALPHA * pick(x_lo, x_hi, r) + zz, g_ref[...], b_ref[...])


def _out_proj(layer, ya_p, ya_s, og_p, og_s, gates, x_lo, x_hi, hi_block0, wa, wb, wo, g, b, tm):
    t = gates.shape[0]
    n_lo = ya_p.shape[0] // tm
    lo = lambda width: pl.BlockSpec((tm, width), lambda i: (jnp.minimum(i, n_lo - 1), 0))
    hi = lambda width, block0=0: pl.BlockSpec((tm, width), lambda i: (jnp.maximum(i - n_lo, 0) + block0, 0))
    const = lambda shape: _layer_spec(layer, shape, pipeline_mode=pl.Buffered(1))
    weight = lambda shape: pl.BlockSpec(shape, lambda i: (0, 0), pipeline_mode=pl.Buffered(1))
    return pl.pallas_call(
        functools.partial(_out_proj_kernel, n_lo=n_lo),
        grid=(t // tm,),
        in_specs=[
            lo(D_CONV), hi(D_CONV), lo(D_GDN), hi(D_GDN),
            pl.BlockSpec((tm, 2 * D_MODEL), lambda i: (i, 0)),
            lo(D_MODEL), hi(D_MODEL, hi_block0),
            weight((D_CONV, D_MODEL)), weight((D_GDN, D_MODEL)), weight((D_MODEL, D_MODEL)),
            const((1, D_MODEL)), const((1, D_MODEL)),
        ],
        out_specs=pl.BlockSpec((tm, D_MODEL), lambda i: (i, 0)),
        out_shape=jax.ShapeDtypeStruct((t, D_MODEL), F32),
        compiler_params=_params("arbitrary"),
        name="out_proj",
    )(ya_p, ya_s, og_p, og_s, gates, x_lo, x_hi, wa, wb, wo, g, b)


def _mlp_kernel(x_ref, wu_ref, wd_ref, g_ref, b_ref, o0_ref, o1_ref, xb_ref, acc_ref, *, n_lo):
    i, j = pl.program_id(0), pl.program_id(1)

    @pl.when(j == 0)
    def _():
        xb_ref[...] = x_ref[...].astype(BF16)
        acc_ref[...] = jnp.zeros_like(acc_ref)

    hid = jnp.maximum(jnp.dot(xb_ref[...], wu_ref[...], preferred_element_type=F32), 0.0)
    acc_ref[...] += jnp.dot((hid * hid).astype(BF16), wd_ref[...], preferred_element_type=F32)

    @pl.when(j == pl.num_programs(1) - 1)
    def _():
        y = _layer_norm(ALPHA * x_ref[...] + acc_ref[...], g_ref[...], b_ref[...])
        if n_lo is None:
            o0_ref[...] = y
            o1_ref[...] = y.astype(BF16)
        else:
            @pl.when(i < n_lo)
            def _():
                o0_ref[...] = y

            @pl.when(i >= n_lo)
            def _():
                o1_ref[...] = y


def _mlp(layer, x, wu, wd, g, b, tm, tf, n_prompt=None):
    t = x.shape[0]
    if n_prompt is None:
        n_lo = None
        out_specs = [pl.BlockSpec((tm, D_MODEL), lambda i, j: (i, 0)),
                     pl.BlockSpec((tm, D_MODEL), lambda i, j: (i, 0))]
        out_shape = [jax.ShapeDtypeStruct((t, D_MODEL), F32), jax.ShapeDtypeStruct((t, D_MODEL), BF16)]
    else:
        n_lo = n_prompt // tm
        out_specs = [pl.BlockSpec((tm, D_MODEL), lambda i, j: (jnp.minimum(i, n_lo - 1), 0)),
                     pl.BlockSpec((tm, D_MODEL), lambda i, j: (jnp.maximum(i - n_lo, 0), 0))]
        out_shape = [jax.ShapeDtypeStruct((n_prompt, D_MODEL), F32),
                     jax.ShapeDtypeStruct((t - n_prompt, D_MODEL), F32)]
    return pl.pallas_call(
        functools.partial(_mlp_kernel, n_lo=n_lo),
        grid=(t // tm, D_FF // tf),
        in_specs=[
            pl.BlockSpec((tm, D_MODEL), lambda i, j: (i, 0)),
            pl.BlockSpec((D_MODEL, tf), lambda i, j: (0, j)),
            pl.BlockSpec((tf, D_MODEL), lambda i, j: (j, 0)),
            _layer_spec(layer, (1, D_MODEL)),
            _layer_spec(layer, (1, D_MODEL)),
        ],
        out_specs=out_specs,
        out_shape=out_shape,
        scratch_shapes=[pltpu.VMEM((tm, D_MODEL), BF16), pltpu.VMEM((tm, D_MODEL), F32)],
        compiler_params=_params("arbitrary", "arbitrary"),
        name="mlp",
    )(x, wu, wd, g, b)


def _front_pad_state(st):
    d, n, k1, c = st.shape
    return jnp.pad(st, ((0, 0), (0, 0), (SAMPLE_LEN - k1, 0), (0, 0))).reshape(d, n * SAMPLE_LEN, c)


def kernel(x_prompt, x_sample, state_conv_a, state_gdn_conv, state_gdn, w_in, conv_a_w, gdn_conv_w, a_log,
           dt_bias, gdn_norm_w, w_a_out, w_b_out, w_o, ln1_g, ln1_b, w_up, w_down, ln2_g, ln2_b):
    batch, seq, _ = x_prompt.shape
    n_seq = x_sample.shape[0]
    n_prompt = batch * seq
    n_sample = n_seq * SAMPLE_LEN
    tm_proj, tm_out, tm_mlp = 1088, 256, 512
    assert x_sample.shape[1] == SAMPLE_LEN and seq % (GROUP * PROMPT_SUB) == 0 and n_seq % SEQ_PER_GROUP == 0
    assert n_prompt % tm_out == 0 and n_sample % tm_out == 0 and n_prompt % tm_mlp == 0 and n_sample == tm_mlp

    xp = x_prompt.reshape(n_prompt, D_MODEL)
    xs = x_sample.reshape(n_sample, D_MODEL)
    x_bf = _bf16_stream(xp, xs, tm_mlp)
    x_lo, x_hi, hi_block0 = xp, xs, 0

    w_t = jnp.swapaxes(w_in, 1, 2)
    w_logit = jnp.pad(w_t[:, COL_LOGIT:COL_GATE], ((0, 0), (0, LANE - 2 * N_HEADS), (0, 0))).astype(BF16)
    par = jnp.zeros((DEPTH, 8, LANE), F32)
    par = par.at[:, 0, LOGIT_A:LOGIT_A + N_HEADS].set(a_log.astype(F32))
    par = par.at[:, 1, LOGIT_A:LOGIT_A + N_HEADS].set(dt_bias.astype(F32))
    par = par.at[:, 2, :].set(gdn_norm_w.astype(F32))
    sta, stg = _front_pad_state(state_conv_a), _front_pad_state(state_gdn_conv)
    row3 = lambda a: a.reshape(DEPTH, 1, D_MODEL)

    ca_p, cg_p, sg_p, ca_s, cg_s = [], [], [], [], []
    sg_s = None
    for l in range(DEPTH):
        proj = _proj(x_bf, w_t, l, 0, N_MIX, F32, False, tm_proj, 1024, "proj")
        gates, wu, wd, wa, wb, wo = _proj(x_bf, w_t, l, COL_GATE, 2 * D_MODEL, BF16, True, tm_proj, 1024, "gates",
                                          cast=(w_up, w_down, w_a_out, w_b_out, w_o))

        ya_p, og_p, utail, gtail, s_p = _prompt_mixer(l, x_bf, w_logit, proj, conv_a_w, gdn_conv_w, par, batch, seq)
        ya_s, og_s, u_s, sg_s = _sample_mixer(l, x_bf, w_logit, proj, conv_a_w, gdn_conv_w, par, sta, stg,
                                              state_gdn, sg_s, n_prompt // GROUP)

        x1 = _out_proj(l, ya_p, ya_s, og_p, og_s, gates, x_lo, x_hi, hi_block0, wa, wb, wo,
                       row3(ln1_g), row3(ln1_b), tm_out)
        if l < DEPTH - 1:
            x, x_bf = _mlp(l, x1, wu, wd, row3(ln2_g), row3(ln2_b), tm_mlp, 1024)
            x_lo, x_hi, hi_block0 = x, x, n_prompt // tm_out
        else:
            y_p, y_s = _mlp(l, x1, wu, wd, row3(ln2_g), row3(ln2_b), tm_mlp, 1024, n_prompt=n_prompt)

        ca_p.append(utail[:, 6:8])
        cg_p.append(gtail[:, 5:8])
        sg_p.append(s_p)
        ca_s.append(u_s.reshape(n_seq, SAMPLE_LEN, D_CONV)[:, 2:4])
        cg_s.append(proj[n_prompt:, COL_QKV:COL_QKV + 3 * D_GDN].reshape(n_seq, SAMPLE_LEN, 3 * D_GDN)[:, 1:4])

    return (y_p.reshape(batch, seq, D_MODEL), y_s.reshape(n_seq, SAMPLE_LEN, D_MODEL),
            jnp.stack(ca_p), jnp.stack(cg_p), jnp.stack(sg_p), jnp.stack(ca_s), jnp.stack(cg_s), sg_s)
```

```python
import functools

import jax
import jax.numpy as jnp
from jax import lax
from jax.experimental import pallas as pl
from jax.experimental.pallas import tpu as pltpu

F32 = jnp.float32
BF16 = jnp.bfloat16

D_MODEL = 2048
DEPTH = 2
D_CONV = D_MODEL // 2
N_HEADS = 8
HEAD_DIM = 128
D_GDN = N_HEADS * HEAD_DIM
D_FF = 4 * D_MODEL
ALPHA = (2 * DEPTH) ** 0.25
LN_EPS = 1e-5
NORM_EPS = 1e-6

COL_GB, COL_GC, COL_H = 0, D_CONV, 2 * D_CONV
COL_QKV = 3 * D_CONV
COL_Z = COL_QKV + 3 * D_GDN
N_MIX = COL_Z + D_GDN
COL_LOGIT = N_MIX
COL_GATE = COL_LOGIT + 2 * N_HEADS
LANE = 128
LOGIT_B, LOGIT_A = 0, N_HEADS

GROUP = 64
PROMPT_SUB = 4
SAMPLE_LEN = 4
SEQ_PER_GROUP = GROUP // SAMPLE_LEN
INV_PASSES = 1

VMEM_LIMIT = 56 * 1024 * 1024


def _dot(a, b):
    return jnp.dot(a.astype(BF16), b.astype(BF16), preferred_element_type=F32)


def _dot_nt(a, b):
    return lax.dot_general(a.astype(BF16), b.astype(BF16), (((1,), (1,)), ((), ())),
                           preferred_element_type=F32)


def _split2(a):
    hi = a.astype(BF16)
    lo = (a - hi.astype(F32)).astype(BF16)
    return hi, lo


def _dot_multi(a, b, passes):
    if passes == 1:
        return _dot(a, b)
    ah, al = _split2(a)
    bh, bl = _split2(b)
    d = lambda x, y: jnp.dot(x, y, preferred_element_type=F32)
    return d(ah, bh) + (d(ah, bl) + d(al, bh))


def _dot_exact_lhs(m01, x):
    x1 = x.astype(BF16)
    r1 = x - x1.astype(F32)
    x2 = r1.astype(BF16)
    x3 = (r1 - x2.astype(F32)).astype(BF16)
    d = lambda y: jnp.dot(m01, y, preferred_element_type=F32)
    return d(x1) + (d(x2) + d(x3))


def _same_block(i, j, size):
    shift = size.bit_length() - 1
    assert size == 1 << shift
    return (i >> shift) == (j >> shift)


def _sigmoid(x):
    return 0.5 * jnp.tanh(0.5 * x) + 0.5


def _softplus(x):
    return jnp.maximum(x, 0.0) + jnp.log1p(jnp.exp(-jnp.abs(x)))


def _layer_norm(y, g, b):
    mu = jnp.mean(y, axis=-1, keepdims=True)
    yc = y - mu
    var = jnp.mean(yc * yc, axis=-1, keepdims=True)
    return yc * lax.rsqrt(var + LN_EPS) * g + b


def _layer_spec(layer, shape, index_map=None, **kw):
    if index_map is None:
        index_map = lambda *g: (0,) * len(shape)
    return pl.BlockSpec((None,) + tuple(shape), lambda *g: (layer,) + tuple(index_map(*g)), **kw)


def _params(*sem):
    return pltpu.CompilerParams(dimension_semantics=sem, vmem_limit_bytes=VMEM_LIMIT)


def _stream_kernel(lo_ref, hi_ref, o_ref, *, n_lo):
    o_ref[...] = jnp.where(pl.program_id(0) < n_lo, lo_ref[...], hi_ref[...]).astype(o_ref.dtype)


def _bf16_stream(x_lo, x_hi, tm):
    n_lo, n_hi = x_lo.shape[0] // tm, x_hi.shape[0] // tm
    return pl.pallas_call(
        functools.partial(_stream_kernel, n_lo=n_lo),
        grid=(n_lo + n_hi,),
        in_specs=[pl.BlockSpec((tm, D_MODEL), lambda i: (jnp.minimum(i, n_lo - 1), 0)),
                  pl.BlockSpec((tm, D_MODEL), lambda i: (jnp.maximum(i - n_lo, 0), 0))],
        out_specs=pl.BlockSpec((tm, D_MODEL), lambda i: (i, 0)),
        out_shape=jax.ShapeDtypeStruct((x_lo.shape[0] + x_hi.shape[0], D_MODEL), BF16),
        compiler_params=_params("arbitrary"),
        name="bf16_stream",
    )(x_lo, x_hi)


def _proj_kernel(x_ref, wt_ref, *rest, gate, n_cast):
    cast_in, o_ref, cast_out, wb_ref = rest[:n_cast], rest[n_cast], rest[n_cast + 1:2 * n_cast + 1], rest[-1]

    @pl.when(pl.program_id(1) == 0)
    def _():
        wb_ref[...] = wt_ref[0].astype(BF16)

    y = lax.dot_general(x_ref[...], wb_ref[...], (((1,), (1,)), ((), ())), preferred_element_type=F32)
    if gate:
        y = _sigmoid(y)
    o_ref[...] = y.astype(o_ref.dtype)
    for src, dst in zip(cast_in, cast_out):
        dst[...] = src[...].astype(dst.dtype)


def _proj(x_bf, w_t, layer, col0, n_cols, out_dtype, gate, tm, tn, name, cast=()):
    t = x_bf.shape[0]
    n_j, n_i = n_cols // tn, t // tm
    step = lambda j, i: j * n_i + i
    rows = [w.shape[1] // (n_j * n_i) for w in cast]
    assert all(r % 16 == 0 and r * n_j * n_i == w.shape[1] for r, w in zip(rows, cast))
    outs = pl.pallas_call(
        functools.partial(_proj_kernel, gate=gate, n_cast=len(cast)),
        grid=(n_j, n_i),
        in_specs=[pl.BlockSpec((tm, D_MODEL), lambda j, i: (i, 0)),
                  pl.BlockSpec((pl.Element(1), pl.Element(tn), pl.Element(D_MODEL)),
                               lambda j, i: (layer, pl.multiple_of(col0 + j * tn, 16), 0))]
        + [_layer_spec(layer, (r, w.shape[2]), lambda j, i: (step(j, i), 0)) for r, w in zip(rows, cast)],
        out_specs=[pl.BlockSpec((tm, tn), lambda j, i: (i, j))]
        + [pl.BlockSpec((r, w.shape[2]), lambda j, i: (step(j, i), 0)) for r, w in zip(rows, cast)],
        out_shape=[jax.ShapeDtypeStruct((t, n_cols), out_dtype)]
        + [jax.ShapeDtypeStruct(w.shape[1:], BF16) for w in cast],
        scratch_shapes=[pltpu.VMEM((tn, D_MODEL), BF16)],
        compiler_params=_params("arbitrary", "arbitrary"),
        name=name,
    )(x_bf, w_t, *cast)
    return outs if cast else outs[0]


def _causal_conv(u, w_ref, prev_fn):
    k = w_ref.shape[0]
    out = u * w_ref[k - 1:k, :]
    for j in range(1, k):
        out = out + prev_fn(j) * w_ref[k - 1 - j:k - j, :]
    return out


CONV_PHASES = 4


def _prompt_conv(u, w_ref, buf_ref, out_ref, act=None):
    k = w_ref.shape[0]
    n, c = u.shape
    slabs = c // LANE
    for s in range(slabs):
        buf_ref[s, 8:, :] = u[:, s * LANE:(s + 1) * LANE]
    for s in range(slabs):
        w = [w_ref[j:j + 1, s * LANE:(s + 1) * LANE] for j in range(k)]
        for b in range(CONV_PHASES):
            acc = None
            for j in range(k):
                tap = buf_ref[s, pl.ds(8 - (k - 1) + j + b, n // CONV_PHASES, stride=CONV_PHASES), :]
                acc = tap * w[j] if acc is None else acc + tap * w[j]
            out_ref[s, pl.ds(b, n // CONV_PHASES, stride=CONV_PHASES), :] = acc if act is None else act(acc)
    for s in range(slabs):
        buf_ref[s, 0:8, :] = buf_ref[s, n:n + 8, :]
    return jnp.concatenate([out_ref[s] for s in range(slabs)], axis=1)


def _sample_prev(u, st4):
    t = lax.broadcasted_iota(jnp.int32, u.shape, 0) & (SAMPLE_LEN - 1)

    def prev(j):
        return jnp.where(t >= j, pltpu.roll(u, j, 0), pltpu.roll(st4, GROUP - (SAMPLE_LEN - j), 0))
    return prev


def _pair_iotas():
    row = lax.broadcasted_iota(jnp.int32, (GROUP, 2 * GROUP), 0)
    lane = lax.broadcasted_iota(jnp.int32, (GROUP, 2 * GROUP), 1)
    return row, lane & (GROUP - 1), lane >= GROUP


def _block_diag(y0, y1):
    z = jnp.zeros_like(y0)
    return jnp.concatenate([jnp.concatenate([y0, z], axis=1), jnp.concatenate([z, y1], axis=1)], axis=0)


def _pair_block_diag(y):
    _, _, second = _pair_iotas()
    return jnp.concatenate([jnp.where(second, 0.0, y), jnp.where(second, y, 0.0)], axis=0)


def _tri_inverse_pairs(ms, seq_len):
    row, col, _ = _pair_iotas()
    eye = (row == col).astype(F32)
    base = min(8, seq_len)
    mm = lambda x, y: _dot_multi(x, _pair_block_diag(y), INV_PASSES)
    same = lambda b: _same_block(row, col, b)
    d = [jnp.where(same(base), m, 0.0) for m in ms]
    d2 = [mm(x, x) for x in d]
    xs = [eye - x for x in d]
    if base > 4:
        d4 = [mm(y, y) for y in d2]
    inv = [x + mm(x, y) for x, y in zip(xs, d2)]
    if base > 4:
        inv = [i + mm(i, y) for i, y in zip(inv, d4)]
    b = base * 2
    while b <= seq_len:
        off_diag = same(b) & jnp.logical_not(same(b // 2))
        t = [mm(i, jnp.where(off_diag, m, 0.0)) for i, m in zip(inv, ms)]
        inv = [i - mm(x, i) for i, x in zip(inv, t)]
        b *= 2
    return inv


def _gdn_group(q, k, v, z, logit, a_row, dt_row, nw_row, seq_len, state, og_ref):
    row = lax.broadcasted_iota(jnp.int32, (GROUP, GROUP), 0)
    col = lax.broadcasted_iota(jnp.int32, (GROUP, GROUP), 1)
    same_seq = _same_block(row, col, seq_len)
    lmat = jnp.concatenate([(same_seq & (row >= col)).astype(F32), same_seq.astype(F32)], axis=0).astype(BF16)
    prow, pcol, second = _pair_iotas()
    psame = _same_block(prow, pcol, seq_len)
    causal = psame & (prow >= pcol)
    strict = psame & (prow > pcol)

    subs = range(q.shape[0] // GROUP)
    heads = range(N_HEADS)
    pairs = range(0, N_HEADS, 2)
    units = [(c, h) for c in subs for h in pairs]
    rows = lambda x, c: x[c * GROUP:(c + 1) * GROUP]
    hs = lambda x, c, h: x[c * GROUP:(c + 1) * GROUP, h * HEAD_DIM:(h + 1) * HEAD_DIM]
    lcol = lambda x, h: x[:, LOGIT_A + h:LOGIT_A + h + 1]
    cat = jnp.concatenate

    beta = _sigmoid(logit)
    g = -jnp.exp(a_row) * _softplus(logit + dt_row)
    gc2 = [_dot_exact_lhs(lmat, rows(g, c)) for c in subs]
    gcum = [x[:GROUP] for x in gc2]
    gtot = [x[GROUP:] for x in gc2]
    gcum_t2 = [cat([x, x], axis=0).T for x in gcum]
    e_cum = [jnp.exp(x) for x in gcum]
    e_rem = [jnp.exp(t - x) for t, x in zip(gtot, gcum)]
    e_tot = [jnp.exp(t) for t in gtot]

    qn, kn, bcol, kb = {}, {}, {}, {}
    for c in subs:
        for h in heads:
            qh, kh = hs(q, c, h), hs(k, c, h)
            qn[c, h] = qh * (lax.rsqrt(jnp.sum(qh * qh, axis=-1, keepdims=True) + NORM_EPS) * (HEAD_DIM ** -0.5))
            kn[c, h] = kh * lax.rsqrt(jnp.sum(kh * kh, axis=-1, keepdims=True) + NORM_EPS)
            bcol[c, h] = rows(beta, c)[:, LOGIT_B + h:LOGIT_B + h + 1]
            kb[c, h] = kn[c, h] * bcol[c, h]

    a = {(c, h): _dot_nt(cat([cat([kb[c, h], kb[c, h + 1]], axis=1), cat([qn[c, h], qn[c, h + 1]], axis=1)], axis=0),
                         _block_diag(kn[c, h], kn[c, h + 1])) for c, h in units}
    ms, attn = {}, {}
    for c, h in units:
        gcol = jnp.where(second, lcol(gcum[c], h + 1), lcol(gcum[c], h))
        grow = jnp.where(second[0:1], gcum_t2[c][LOGIT_A + h + 1:LOGIT_A + h + 2, :],
                         gcum_t2[c][LOGIT_A + h:LOGIT_A + h + 1, :])
        dec = jnp.exp(jnp.where(causal, gcol - grow, 0.0))
        ms[c, h] = jnp.where(strict, a[c, h][:GROUP] * dec, 0.0)
        attn[c, h] = jnp.where(causal, a[c, h][GROUP:] * dec, 0.0)
    t_inv = dict(zip(units, _tri_inverse_pairs([ms[un] for un in units], seq_len)))

    u, w, qdec, kdec = {}, {}, {}, {}
    for c, h in units:
        rhs = [cat([hs(v, c, hh) * bcol[c, hh], kb[c, hh] * lcol(e_cum[c], hh)], axis=1) for hh in (h, h + 1)]
        uw = _dot(t_inv[c, h], _block_diag(*rhs))
        for i, hh in enumerate((h, h + 1)):
            u[c, hh] = uw[:, 2 * i * HEAD_DIM:(2 * i + 1) * HEAD_DIM]
            w[c, hh] = uw[:, (2 * i + 1) * HEAD_DIM:(2 * i + 2) * HEAD_DIM]
            qdec[c, hh] = qn[c, hh] * lcol(e_cum[c], hh)
            kdec[c, hh] = kn[c, hh] * lcol(e_rem[c], hh)

    for c in subs:
        ws = [state.read(h, cat([w[c, h], qdec[c, h]], axis=0)) for h in heads]
        v_new = [u[c, h] - ws[h][0] for h in heads]
        vn_bd = {h: _block_diag(v_new[h], v_new[h + 1]) for h in pairs}
        o_pair = {h: cat([ws[h][1], ws[h + 1][1]], axis=1) + _dot(attn[c, h], vn_bd[h]) for h in pairs}
        for h in pairs:
            state.update(h, cat([kdec[c, h], kdec[c, h + 1]], axis=0).T, vn_bd[h],
                         lcol(e_tot[c], h), lcol(e_tot[c], h + 1))
        for h in heads:
            o = o_pair[h - h % 2][:, (h % 2) * HEAD_DIM:(h % 2 + 1) * HEAD_DIM]
            o = o * lax.rsqrt(jnp.mean(o * o, axis=-1, keepdims=True) + NORM_EPS) * nw_row
            zh = hs(z, c, h)
            og_ref[c * GROUP:(c + 1) * GROUP, h * HEAD_DIM:(h + 1) * HEAD_DIM] = (
                o * (zh * _sigmoid(zh))).astype(og_ref.dtype)
    state.finish()


def _silu(x):
    return x * _sigmoid(x)


def _mixer_common(x_ref, wl_ref, abc_ref, qkv_ref, z_ref, par_ref, conv_a, conv_g, seq_len, state, ya_ref, og_ref):
    gb = abc_ref[:, COL_GB:COL_GB + D_CONV]
    gc = abc_ref[:, COL_GC:COL_GC + D_CONV]
    hv = abc_ref[:, COL_H:COL_H + D_CONV]
    u = gc * hv
    ya_ref[...] = (gb * conv_a(u)).astype(ya_ref.dtype)

    raw = qkv_ref[...]
    act = conv_g(raw)
    q, k, v = act[:, :D_GDN], act[:, D_GDN:2 * D_GDN], act[:, 2 * D_GDN:]
    logit = lax.dot_general(x_ref[...], wl_ref[...], (((1,), (1,)), ((), ())), preferred_element_type=F32)
    _gdn_group(q, k, v, z_ref[...], logit, par_ref[0:1, :], par_ref[1:2, :], par_ref[2:3, :],
               seq_len, state, og_ref)
    return u, raw


def _prompt_mixer_kernel(x_ref, wl_ref, abc_ref, qkv_ref, z_ref, caw_ref, gcw_ref, par_ref,
                         ya_ref, og_ref, utail_ref, gtail_ref, s_ref, ubuf_ref, uout_ref, gbuf_ref, gout_ref):
    @pl.when(pl.program_id(1) == 0)
    def _():
        ubuf_ref[:, 0:8, :] = jnp.zeros((D_CONV // LANE, 8, LANE), F32)
        gbuf_ref[:, 0:8, :] = jnp.zeros((3 * D_GDN // LANE, 8, LANE), F32)
        s_ref[...] = jnp.zeros_like(s_ref)

    cur = {}

    class State:
        @staticmethod
        def read(h, lhs):
            if h not in cur:
                cur[h] = s_ref[0, h]
            ws = _dot(lhs, cur[h])
            return ws[:GROUP], ws[GROUP:]

        @staticmethod
        def update(h, kd_t, vn_bd, e0, e1):
            kv = _dot(kd_t, vn_bd)
            cur[h] = cur[h] * e0[0:1, :] + kv[:, :HEAD_DIM]
            cur[h + 1] = cur[h + 1] * e1[0:1, :] + kv[:, HEAD_DIM:]

        @staticmethod
        def finish():
            for h, s in cur.items():
                s_ref[0, h] = s

    u, raw = _mixer_common(
        x_ref, wl_ref, abc_ref, qkv_ref, z_ref, par_ref,
        lambda u: _prompt_conv(u, caw_ref, ubuf_ref, uout_ref),
        lambda r: _prompt_conv(r, gcw_ref, gbuf_ref, gout_ref, _silu),
        GROUP, State, ya_ref, og_ref)
    n = u.shape[0]
    utail_ref[0] = u[n - 8:]
    gtail_ref[0] = raw[n - 8:]


def _sample_mixer_kernel(x_ref, wl_ref, abc_ref, qkv_ref, z_ref, caw_ref, gcw_ref, par_ref,
                         sta_ref, stg_ref, s0_ref, *rest):
    ya_ref, og_ref, u_ref, s_ref = rest[-4:]
    row8 = lax.broadcasted_iota(jnp.int32, (8, HEAD_DIM), 0)
    first = row8 < SAMPLE_LEN
    row_bd = lax.broadcasted_iota(jnp.int32, (2 * GROUP, 2 * HEAD_DIM), 0) & (GROUP - 1)

    class State:
        @staticmethod
        def read(h, lhs):
            w, qdec = lhs[:GROUP], lhs[GROUP:]
            ws_w, ws_q = [], []
            for p in range(GROUP // 8):
                slab = jnp.concatenate([w[8 * p:8 * p + 8], qdec[8 * p:8 * p + 8]], axis=0)
                r0 = _dot(slab, s0_ref[2 * p, h])
                r1 = _dot(slab, s0_ref[2 * p + 1, h])
                ws_w.append(jnp.where(first, r0[:8], r1[:8]))
                ws_q.append(jnp.where(first, r0[8:], r1[8:]))
            return jnp.concatenate(ws_w, axis=0), jnp.concatenate(ws_q, axis=0)

        @staticmethod
        def update(h, kd_t, vn_bd, e0, e1):
            for s in range(SEQ_PER_GROUP):
                in_seq = _same_block(row_bd, SAMPLE_LEN * s, SAMPLE_LEN)
                kv = _dot(kd_t, jnp.where(in_seq, vn_bd, 0.0))
                r = SAMPLE_LEN * s
                s_ref[s, h] = s0_ref[s, h] * e0[r:r + 1, :] + kv[:, :HEAD_DIM]
                s_ref[s, h + 1] = s0_ref[s, h + 1] * e1[r:r + 1, :] + kv[:, HEAD_DIM:]

        @staticmethod
        def finish():
            pass

    u, _ = _mixer_common(
        x_ref, wl_ref, abc_ref, qkv_ref, z_ref, par_ref,
        lambda u: _causal_conv(u, caw_ref, _sample_prev(u, sta_ref[...])),
        lambda r: _silu(_causal_conv(r, gcw_ref, _sample_prev(r, stg_ref[...]))),
        SAMPLE_LEN, State, ya_ref, og_ref)
    u_ref[...] = u


def _mixer_in_specs(layer, row_block, n_rows=GROUP):
    rows = lambda width, col_block: pl.BlockSpec((n_rows, width), lambda *i: (row_block(*i), col_block))
    return [
        rows(D_MODEL, 0),
        _layer_spec(layer, (LANE, D_MODEL)),
        rows(3 * D_CONV, 0),
        rows(3 * D_GDN, COL_QKV // (3 * D_GDN)),
        rows(D_GDN, COL_Z // D_GDN),
        _layer_spec(layer, (3, D_CONV)),
        _layer_spec(layer, (4, 3 * D_GDN)),
        _layer_spec(layer, (8, LANE)),
    ]


def _prompt_mixer(layer, x_bf, w_logit, proj, caw, gcw, par, batch, seq):
    step = GROUP * PROMPT_SUB
    chunks = seq // step
    rb = lambda b, c: b * chunks + c
    n = batch * seq
    return pl.pallas_call(
        _prompt_mixer_kernel,
        grid=(batch, chunks),
        in_specs=_mixer_in_specs(layer, rb, step),
        out_specs=[
            pl.BlockSpec((step, D_CONV), lambda b, c: (rb(b, c), 0)),
            pl.BlockSpec((step, D_GDN), lambda b, c: (rb(b, c), 0)),
            pl.BlockSpec((1, 8, D_CONV), lambda b, c: (b, 0, 0)),
            pl.BlockSpec((1, 8, 3 * D_GDN), lambda b, c: (b, 0, 0)),
            pl.BlockSpec((1, N_HEADS, HEAD_DIM, HEAD_DIM), lambda b, c: (b, 0, 0, 0)),
        ],
        out_shape=[
            jax.ShapeDtypeStruct((n, D_CONV), BF16),
            jax.ShapeDtypeStruct((n, D_GDN), BF16),
            jax.ShapeDtypeStruct((batch, 8, D_CONV), F32),
            jax.ShapeDtypeStruct((batch, 8, 3 * D_GDN), F32),
            jax.ShapeDtypeStruct((batch, N_HEADS, HEAD_DIM, HEAD_DIM), F32),
        ],
        scratch_shapes=[pltpu.VMEM((D_CONV // LANE, step + 8, LANE), F32),
                        pltpu.VMEM((D_CONV // LANE, step, LANE), F32),
                        pltpu.VMEM((3 * D_GDN // LANE, step + 8, LANE), F32),
                        pltpu.VMEM((3 * D_GDN // LANE, step, LANE), F32)],
        compiler_params=_params("arbitrary", "arbitrary"),
        name="prompt_mixer",
    )(x_bf, w_logit, proj, proj, proj, caw, gcw, par)


def _sample_mixer(layer, x_bf, w_logit, proj, caw, gcw, par, sta, stg, s0, s_prev, first_block):
    n_seq = s0.shape[1]
    rb = lambda g: first_block + g
    state_block = (SEQ_PER_GROUP, N_HEADS, HEAD_DIM, HEAD_DIM)
    in_specs = _mixer_in_specs(layer, rb) + [
        _layer_spec(layer, (GROUP, D_CONV), lambda g: (g, 0)),
        _layer_spec(layer, (GROUP, 3 * D_GDN), lambda g: (g, 0)),
        _layer_spec(layer, state_block, lambda g: (g, 0, 0, 0)),
    ]
    args = [x_bf, w_logit, proj, proj, proj, caw, gcw, par, sta, stg, s0]
    aliases = {}
    if s_prev is not None:
        in_specs.append(pl.BlockSpec(memory_space=pl.ANY))
        aliases = {len(args): 3}
        args.append(s_prev)
    return pl.pallas_call(
        _sample_mixer_kernel,
        grid=(n_seq // SEQ_PER_GROUP,),
        in_specs=in_specs,
        out_specs=[
            pl.BlockSpec((GROUP, D_CONV), lambda g: (g, 0)),
            pl.BlockSpec((GROUP, D_GDN), lambda g: (g, 0)),
            pl.BlockSpec((GROUP, D_CONV), lambda g: (g, 0)),
            _layer_spec(layer, state_block, lambda g: (g, 0, 0, 0)),
        ],
        out_shape=[
            jax.ShapeDtypeStruct((n_seq * SAMPLE_LEN, D_CONV), BF16),
            jax.ShapeDtypeStruct((n_seq * SAMPLE_LEN, D_GDN), BF16),
            jax.ShapeDtypeStruct((n_seq * SAMPLE_LEN, D_CONV), F32),
            jax.ShapeDtypeStruct(s0.shape, F32),
        ],
        input_output_aliases=aliases,
        compiler_params=_params("arbitrary"),
        name="sample_mixer",
    )(*args)


def _out_proj_kernel(ya_lo, ya_hi, og_lo, og_hi, gate_ref, x_lo, x_hi, wa_ref, wb_ref, wo_ref, g_ref, b_ref,
                     o_ref, *, n_lo):
    lo = pl.program_id(0) < n_lo
    half = o_ref.shape[0] // 2
    halves = (slice(0, half), slice(half, 2 * half))
    pick = lambda a, b, r: jnp.where(lo, a[r, :], b[r, :])
    dot = lambda a, w_ref: jnp.dot(a, w_ref[...], preferred_element_type=F32)
    y_a = [dot(pick(ya_lo, ya_hi, r), wa_ref) for r in halves]
    y_b = [dot(pick(og_lo, og_hi, r), wb_ref) for r in halves]
    merged = [gate_ref[r, :D_MODEL].astype(F32) * a + gate_ref[r, D_MODEL:].astype(F32) * b
              for r, a, b in zip(halves, y_a, y_b)]
    z = [dot(m.astype(BF16), wo_ref) for m in merged]
    for r, zz in zip(halves, z):
        o_ref[r, :] = _layer_norm(ALPHA * pick(x_lo, x_hi, r) + zz, g_ref[...], b_ref[...])


def _out_proj(layer, ya_p, ya_s, og_p, og_s, gates, x_lo, x_hi, hi_block0, wa, wb, wo, g, b, tm):
    t = gates.shape[0]
    n_lo = ya_p.shape[0] // tm
    lo = lambda width: pl.BlockSpec((tm, width), lambda i: (jnp.minimum(i, n_lo - 1), 0))
    hi = lambda width, block0=0: pl.BlockSpec((tm, width), lambda i: (jnp.maximum(i - n_lo, 0) + block0, 0))
    const = lambda shape: _layer_spec(layer, shape, pipeline_mode=pl.Buffered(1))
    weight = lambda shape: pl.BlockSpec(shape, lambda i: (0, 0), pipeline_mode=pl.Buffered(1))
    return pl.pallas_call(
        functools.partial(_out_proj_kernel, n_lo=n_lo),
        grid=(t // tm,),
        in_specs=[
            lo(D_CONV), hi(D_CONV), lo(D_GDN), hi(D_GDN),
            pl.BlockSpec((tm, 2 * D_MODEL), lambda i: (i, 0)),
            lo(D_MODEL), hi(D_MODEL, hi_block0),
            weight((D_CONV, D_MODEL)), weight((D_GDN, D_MODEL)), weight((D_MODEL, D_MODEL)),
            const((1, D_MODEL)), const((1, D_MODEL)),
        ],
        out_specs=pl.BlockSpec((tm, D_MODEL), lambda i: (i, 0)),
        out_shape=jax.ShapeDtypeStruct((t, D_MODEL), F32),
        compiler_params=_params("arbitrary"),
        name="out_proj",
    )(ya_p, ya_s, og_p, og_s, gates, x_lo, x_hi, wa, wb, wo, g, b)


def _mlp_kernel(x_ref, wu_ref, wd_ref, g_ref, b_ref, o0_ref, o1_ref, xb_ref, acc_ref, *, n_lo):
    i, j = pl.program_id(0), pl.program_id(1)

    @pl.when(j == 0)
    def _():
        xb_ref[...] = x_ref[...].astype(BF16)
        acc_ref[...] = jnp.zeros_like(acc_ref)

    hid = jnp.maximum(jnp.dot(xb_ref[...], wu_ref[...], preferred_element_type=F32), 0.0)
    acc_ref[...] += jnp.dot((hid * hid).astype(BF16), wd_ref[...], preferred_element_type=F32)

    @pl.when(j == pl.num_programs(1) - 1)
    def _():
        y = _layer_norm(ALPHA * x_ref[...] + acc_ref[...], g_ref[...], b_ref[...])
        if n_lo is None:
            o0_ref[...] = y
            o1_ref[...] = y.astype(BF16)
        else:
            @pl.when(i < n_lo)
            def _():
                o0_ref[...] = y

            @pl.when(i >= n_lo)
            def _():
                o1_ref[...] = y


def _mlp(layer, x, wu, wd, g, b, tm, tf, n_prompt=None):
    t = x.shape[0]
    if n_prompt is None:
        n_lo = None
        out_specs = [pl.BlockSpec((tm, D_MODEL), lambda i, j: (i, 0)),
                     pl.BlockSpec((tm, D_MODEL), lambda i, j: (i, 0))]
        out_shape = [jax.ShapeDtypeStruct((t, D_MODEL), F32), jax.ShapeDtypeStruct((t, D_MODEL), BF16)]
    else:
        n_lo = n_prompt // tm
        out_specs = [pl.BlockSpec((tm, D_MODEL), lambda i, j: (jnp.minimum(i, n_lo - 1), 0)),
                     pl.BlockSpec((tm, D_MODEL), lambda i, j: (jnp.maximum(i - n_lo, 0), 0))]
        out_shape = [jax.ShapeDtypeStruct((n_prompt, D_MODEL), F32),
                     jax.ShapeDtypeStruct((t - n_prompt, D_MODEL), F32)]
    return pl.pallas_call(
        functools.partial(_mlp_kernel, n_lo=n_lo),
        grid=(t // tm, D_FF // tf),
        in_specs=[
            pl.BlockSpec((tm, D_MODEL), lambda i, j: (i, 0)),
            pl.BlockSpec((D_MODEL, tf), lambda i, j: (0, j)),
            pl.BlockSpec((tf, D_MODEL), lambda i, j: (j, 0)),
            _layer_spec(layer, (1, D_MODEL)),
            _layer_spec(layer, (1, D_MODEL)),
        ],
        out_specs=out_specs,
        out_shape=out_shape,
        scratch_shapes=[pltpu.VMEM((tm, D_MODEL), BF16), pltpu.VMEM((tm, D_MODEL), F32)],
        compiler_params=_params("arbitrary", "arbitrary"),
        name="mlp",
    )(x, wu, wd, g, b)


def _front_pad_state(st):
    d, n, k1, c = st.shape
    return jnp.pad(st, ((0, 0), (0, 0), (SAMPLE_LEN - k1, 0), (0, 0))).reshape(d, n * SAMPLE_LEN, c)


def kernel(x_prompt, x_sample, state_conv_a, state_gdn_conv, state_gdn, w_in, conv_a_w, gdn_conv_w, a_log,
           dt_bias, gdn_norm_w, w_a_out, w_b_out, w_o, ln1_g, ln1_b, w_up, w_down, ln2_g, ln2_b):
    batch, seq, _ = x_prompt.shape
    n_seq = x_sample.shape[0]
    n_prompt = batch * seq
    n_sample = n_seq * SAMPLE_LEN
    tm_proj, tm_out, tm_mlp = 1088, 256, 512
    assert x_sample.shape[1] == SAMPLE_LEN and seq % (GROUP * PROMPT_SUB) == 0 and n_seq % SEQ_PER_GROUP == 0
    assert n_prompt % tm_out == 0 and n_sample % tm_out == 0 and n_prompt % tm_mlp == 0 and n_sample == tm_mlp

    xp = x_prompt.reshape(n_prompt, D_MODEL)
    xs = x_sample.reshape(n_sample, D_MODEL)
    x_bf = _bf16_stream(xp, xs, tm_mlp)
    x_lo, x_hi, hi_block0 = xp, xs, 0

    w_t = jnp.swapaxes(w_in, 1, 2)
    w_logit = jnp.pad(w_t[:, COL_LOGIT:COL_GATE], ((0, 0), (0, LANE - 2 * N_HEADS), (0, 0))).astype(BF16)
    par = jnp.zeros((DEPTH, 8, LANE), F32)
    par = par.at[:, 0, LOGIT_A:LOGIT_A + N_HEADS].set(a_log.astype(F32))
    par = par.at[:, 1, LOGIT_A:LOGIT_A + N_HEADS].set(dt_bias.astype(F32))
    par = par.at[:, 2, :].set(gdn_norm_w.astype(F32))
    sta, stg = _front_pad_state(state_conv_a), _front_pad_state(state_gdn_conv)
    row3 = lambda a: a.reshape(DEPTH, 1, D_MODEL)

    ca_p, cg_p, sg_p, ca_s, cg_s = [], [], [], [], []
    sg_s = None
    for l in range(DEPTH):
        proj = _proj(x_bf, w_t, l, 0, N_MIX, F32, False, tm_proj, 1024, "proj")
        gates, wu, wd, wa, wb, wo = _proj(x_bf, w_t, l, COL_GATE, 2 * D_MODEL, BF16, True, tm_proj, 1024, "gates",
                                          cast=(w_up, w_down, w_a_out, w_b_out, w_o))

        ya_p, og_p, utail, gtail, s_p = _prompt_mixer(l, x_bf, w_logit, proj, conv_a_w, gdn_conv_w, par, batch, seq)
        ya_s, og_s, u_s, sg_s = _sample_mixer(l, x_bf, w_logit, proj, conv_a_w, gdn_conv_w, par, sta, stg,
                                              state_gdn, sg_s, n_prompt // GROUP)

        x1 = _out_proj(l, ya_p, ya_s, og_p, og_s, gates, x_lo, x_hi, hi_block0, wa, wb, wo,
                       row3(ln1_g), row3(ln1_b), tm_out)
        if l < DEPTH - 1:
            x, x_bf = _mlp(l, x1, wu, wd, row3(ln2_g), row3(ln2_b), tm_mlp, 1024)
            x_lo, x_hi, hi_block0 = x, x, n_prompt // tm_out
        else:
            y_p, y_s = _mlp(l, x1, wu, wd, row3(ln2_g), row3(ln2_b), tm_mlp, 1024, n_prompt=n_prompt)

        ca_p.append(utail[:, 6:8])
        cg_p.append(gtail[:, 5:8])
        sg_p.append(s_p)
        ca_s.append(u_s.reshape(n_seq, SAMPLE_LEN, D_CONV)[:, 2:4])
        cg_s.append(proj[n_prompt:, COL_QKV:COL_QKV + 3 * D_GDN].reshape(n_seq, SAMPLE_LEN, 3 * D_GDN)[:, 1:4])

    return (y_p.reshape(batch, seq, D_MODEL), y_s.reshape(n_seq, SAMPLE_LEN, D_MODEL),
            jnp.stack(ca_p), jnp.stack(cg_p), jnp.stack(sg_p), jnp.stack(ca_s), jnp.stack(cg_s), sg_s)
```

```python
import functools

import jax
import jax.numpy as jnp
from jax import lax
from jax.experimental import pallas as pl
from jax.experimental.pallas import tpu as pltpu

F32 = jnp.float32
BF16 = jnp.bfloat16

D_MODEL = 2048
DEPTH = 2
D_CONV = D_MODEL // 2
N_HEADS = 8
HEAD_DIM = 128
D_GDN = N_HEADS * HEAD_DIM
D_FF = 4 * D_MODEL
ALPHA = (2 * DEPTH) ** 0.25
LN_EPS = 1e-5
NORM_EPS = 1e-6

COL_GB, COL_GC, COL_H = 0, D_CONV, 2 * D_CONV
COL_QKV = 3 * D_CONV
COL_Z = COL_QKV + 3 * D_GDN
N_MIX = COL_Z + D_GDN
COL_LOGIT = N_MIX
COL_GATE = COL_LOGIT + 2 * N_HEADS
LANE = 128
LOGIT_B, LOGIT_A = 0, N_HEADS

GROUP = 64
PROMPT_SUB = 4
SAMPLE_LEN = 4
SEQ_PER_GROUP = GROUP // SAMPLE_LEN
INV_PASSES = 1

VMEM_LIMIT = 56 * 1024 * 1024


def _dot(a, b):
    return jnp.dot(a.astype(BF16), b.astype(BF16), preferred_element_type=F32)


def _dot_nt(a, b):
    return lax.dot_general(a.astype(BF16), b.astype(BF16), (((1,), (1,)), ((), ())),
                           preferred_element_type=F32)


def _split2(a):
    hi = a.astype(BF16)
    lo = (a - hi.astype(F32)).astype(BF16)
    return hi, lo


def _dot_multi(a, b, passes):
    if passes == 1:
        return _dot(a, b)
    ah, al = _split2(a)
    bh, bl = _split2(b)
    d = lambda x, y: jnp.dot(x, y, preferred_element_type=F32)
    return d(ah, bh) + (d(ah, bl) + d(al, bh))


def _dot_exact_lhs(m01, x):
    x1 = x.astype(BF16)
    r1 = x - x1.astype(F32)
    x2 = r1.astype(BF16)
    x3 = (r1 - x2.astype(F32)).astype(BF16)
    d = lambda y: jnp.dot(m01, y, preferred_element_type=F32)
    return d(x1) + (d(x2) + d(x3))


def _same_block(i, j, size):
    shift = size.bit_length() - 1
    assert size == 1 << shift
    return (i >> shift) == (j >> shift)


def _sigmoid(x):
    return 0.5 * jnp.tanh(0.5 * x) + 0.5


def _softplus(x):
    return jnp.maximum(x, 0.0) + jnp.log1p(jnp.exp(-jnp.abs(x)))


def _layer_norm(y, g, b):
    mu = jnp.mean(y, axis=-1, keepdims=True)
    yc = y - mu
    var = jnp.mean(yc * yc, axis=-1, keepdims=True)
    return yc * lax.rsqrt(var + LN_EPS) * g + b


def _layer_spec(layer, shape, index_map=None, **kw):
    if index_map is None:
        index_map = lambda *g: (0,) * len(shape)
    return pl.BlockSpec((None,) + tuple(shape), lambda *g: (layer,) + tuple(index_map(*g)), **kw)


def _params(*sem):
    return pltpu.CompilerParams(dimension_semantics=sem, vmem_limit_bytes=VMEM_LIMIT)


def _stream_kernel(lo_ref, hi_ref, o_ref, *, n_lo):
    o_ref[...] = jnp.where(pl.program_id(0) < n_lo, lo_ref[...], hi_ref[...]).astype(o_ref.dtype)


def _bf16_stream(x_lo, x_hi, tm):
    n_lo, n_hi = x_lo.shape[0] // tm, x_hi.shape[0] // tm
    return pl.pallas_call(
        functools.partial(_stream_kernel, n_lo=n_lo),
        grid=(n_lo + n_hi,),
        in_specs=[pl.BlockSpec((tm, D_MODEL), lambda i: (jnp.minimum(i, n_lo - 1), 0)),
                  pl.BlockSpec((tm, D_MODEL), lambda i: (jnp.maximum(i - n_lo, 0), 0))],
        out_specs=pl.BlockSpec((tm, D_MODEL), lambda i: (i, 0)),
        out_shape=jax.ShapeDtypeStruct((x_lo.shape[0] + x_hi.shape[0], D_MODEL), BF16),
        compiler_params=_params("arbitrary"),
        name="bf16_stream",
    )(x_lo, x_hi)


def _proj_kernel(x_ref, wt_ref, *rest, gate, n_cast):
    cast_in, o_ref, cast_out, wb_ref = rest[:n_cast], rest[n_cast], rest[n_cast + 1:2 * n_cast + 1], rest[-1]

    @pl.when(pl.program_id(1) == 0)
    def _():
        wb_ref[...] = wt_ref[0].astype(BF16)

    y = lax.dot_general(x_ref[...], wb_ref[...], (((1,), (1,)), ((), ())), preferred_element_type=F32)
    if gate:
        y = _sigmoid(y)
    o_ref[...] = y.astype(o_ref.dtype)
    for src, dst in zip(cast_in, cast_out):
        dst[...] = src[...].astype(dst.dtype)


def _proj(x_bf, w_t, layer, col0, n_cols, out_dtype, gate, tm, tn, name, cast=()):
    t = x_bf.shape[0]
    n_j, n_i = n_cols // tn, t // tm
    step = lambda j, i: j * n_i + i
    rows = [w.shape[1] // (n_j * n_i) for w in cast]
    assert all(r % 16 == 0 and r * n_j * n_i == w.shape[1] for r, w in zip(rows, cast))
    outs = pl.pallas_call(
        functools.partial(_proj_kernel, gate=gate, n_cast=len(cast)),
        grid=(n_j, n_i),
        in_specs=[pl.BlockSpec((tm, D_MODEL), lambda j, i: (i, 0)),
                  pl.BlockSpec((pl.Element(1), pl.Element(tn), pl.Element(D_MODEL)),
                               lambda j, i: (layer, pl.multiple_of(col0 + j * tn, 16), 0))]
        + [_layer_spec(layer, (r, w.shape[2]), lambda j, i: (step(j, i), 0)) for r, w in zip(rows, cast)],
        out_specs=[pl.BlockSpec((tm, tn), lambda j, i: (i, j))]
        + [pl.BlockSpec((r, w.shape[2]), lambda j, i: (step(j, i), 0)) for r, w in zip(rows, cast)],
        out_shape=[jax.ShapeDtypeStruct((t, n_cols), out_dtype)]
        + [jax.ShapeDtypeStruct(w.shape[1:], BF16) for w in cast],
        scratch_shapes=[pltpu.VMEM((tn, D_MODEL), BF16)],
        compiler_params=_params("arbitrary", "arbitrary"),
        name=name,
    )(x_bf, w_t, *cast)
    return outs if cast else outs[0]


def _causal_conv(u, w_ref, prev_fn):
    k = w_ref.shape[0]
    out = u * w_ref[k - 1:k, :]
    for j in range(1, k):
        out = out + prev_fn(j) * w_ref[k - 1 - j:k - j, :]
    return out


CONV_PHASES = 4


def _prompt_conv(u, w_ref, buf_ref, out_ref, act=None):
    k = w_ref.shape[0]
    n, c = u.shape
    slabs = c // LANE
    for s in range(slabs):
        buf_ref[s, 8:, :] = u[:, s * LANE:(s + 1) * LANE]
    for s in range(slabs):
        w = [w_ref[j:j + 1, s * LANE:(s + 1) * LANE] for j in range(k)]
        for b in range(CONV_PHASES):
            acc = None
            for j in range(k):
                tap = buf_ref[s, pl.ds(8 - (k - 1) + j + b, n // CONV_PHASES, stride=CONV_PHASES), :]
                acc = tap * w[j] if acc is None else acc + tap * w[j]
            out_ref[s, pl.ds(b, n // CONV_PHASES, stride=CONV_PHASES), :] = acc if act is None else act(acc)
    for s in range(slabs):
        buf_ref[s, 0:8, :] = buf_ref[s, n:n + 8, :]


def _slabs_value(ref):
    return jnp.concatenate([ref[s] for s in range(ref.shape[0])], axis=1)


def _sample_prev(u, st_ref, fill_ref):
    t = lax.broadcasted_iota(jnp.int32, u.shape, 0) & (SAMPLE_LEN - 1)
    k1 = st_ref.shape[0]
    slabs = u.shape[1] // LANE

    def prev(j):
        for tt in range(j):
            for s in range(slabs):
                fill_ref[j - 1, s, pl.ds(tt, SEQ_PER_GROUP, stride=SAMPLE_LEN), :] = (
                    st_ref[k1 - j + tt, :, s * LANE:(s + 1) * LANE])
        fill = jnp.concatenate([fill_ref[j - 1, s] for s in range(slabs)], axis=1)
        return jnp.where(t >= j, pltpu.roll(u, j, 0), fill)
    return prev


def _pair_iotas():
    row = lax.broadcasted_iota(jnp.int32, (GROUP, 2 * GROUP), 0)
    lane = lax.broadcasted_iota(jnp.int32, (GROUP, 2 * GROUP), 1)
    return row, lane & (GROUP - 1), lane >= GROUP


def _block_diag(y0, y1):
    z = jnp.zeros_like(y0)
    return jnp.concatenate([jnp.concatenate([y0, z], axis=1), jnp.concatenate([z, y1], axis=1)], axis=0)


def _pair_block_diag(y):
    _, _, second = _pair_iotas()
    return jnp.concatenate([jnp.where(second, 0.0, y), jnp.where(second, y, 0.0)], axis=0)


def _tri_inverse_pairs(ms, seq_len):
    row, col, _ = _pair_iotas()
    eye = (row == col).astype(F32)
    base = min(8, seq_len)
    mm = lambda x, y: _dot_multi(x, _pair_block_diag(y), INV_PASSES)
    same = lambda b: _same_block(row, col, b)
    d = [jnp.where(same(base), m, 0.0) for m in ms]
    d2 = [mm(x, x) for x in d]
    xs = [eye - x for x in d]
    if base > 4:
        d4 = [mm(y, y) for y in d2]
    inv = [x + mm(x, y) for x, y in zip(xs, d2)]
    if base > 4:
        inv = [i + mm(i, y) for i, y in zip(inv, d4)]
    b = base * 2
    while b <= seq_len:
        off_diag = same(b) & jnp.logical_not(same(b // 2))
        t = [mm(i, jnp.where(off_diag, m, 0.0)) for i, m in zip(inv, ms)]
        inv = [i - mm(x, i) for i, x in zip(inv, t)]
        b *= 2
    return inv


def _gdn_group(q, k, v, z, logit, a_row, dt_row, nw_row, seq_len, state, og_ref):
    row = lax.broadcasted_iota(jnp.int32, (GROUP, GROUP), 0)
    col = lax.broadcasted_iota(jnp.int32, (GROUP, GROUP), 1)
    same_seq = _same_block(row, col, seq_len)
    lmat = jnp.concatenate([(same_seq & (row >= col)).astype(F32), same_seq.astype(F32)], axis=0).astype(BF16)
    prow, pcol, second = _pair_iotas()
    psame = _same_block(prow, pcol, seq_len)
    causal = psame & (prow >= pcol)
    strict = psame & (prow > pcol)

    n_sub = q.shape[0] // GROUP
    heads = range(N_HEADS)
    pairs = range(0, N_HEADS, 2)
    rows = lambda x, c: x[c * GROUP:(c + 1) * GROUP]
    hs = lambda x, c, h: x[c * GROUP:(c + 1) * GROUP, h * HEAD_DIM:(h + 1) * HEAD_DIM]
    lcol = lambda x, h: x[:, LOGIT_A + h:LOGIT_A + h + 1]
    cat = jnp.concatenate

    beta = _sigmoid(logit)
    g = -jnp.exp(a_row) * _softplus(logit + dt_row)
    _gdn_sub_chunks(range(n_sub), q, k, v, z, beta, g, lmat, causal, strict,
                    second, nw_row, seq_len, state, og_ref, heads, pairs, rows, hs, lcol, cat)
    state.finish()


def _gdn_sub_chunks(subs, q, k, v, z, beta, g, lmat, causal, strict, second, nw_row, seq_len, state, og_ref,
                    heads, pairs, rows, hs, lcol, cat):
    units = [(c, h) for c in subs for h in pairs]
    gc2 = {c: _dot_exact_lhs(lmat, rows(g, c)) for c in subs}
    gcum = {c: x[:GROUP] for c, x in gc2.items()}
    gtot = {c: x[GROUP:] for c, x in gc2.items()}
    gcum_t2 = {c: cat([x, x], axis=0).T for c, x in gcum.items()}
    e_cum = {c: jnp.exp(x) for c, x in gcum.items()}
    e_rem = {c: jnp.exp(gtot[c] - gcum[c]) for c in subs}
    e_tot = {c: jnp.exp(t) for c, t in gtot.items()}

    qn, kn, bcol, kb = {}, {}, {}, {}
    for c in subs:
        for h in heads:
            qh, kh = hs(q, c, h), hs(k, c, h)
            qn[c, h] = qh * (lax.rsqrt(jnp.sum(qh * qh, axis=-1, keepdims=True) + NORM_EPS) * (HEAD_DIM ** -0.5))
            kn[c, h] = kh * lax.rsqrt(jnp.sum(kh * kh, axis=-1, keepdims=True) + NORM_EPS)
            bcol[c, h] = rows(beta, c)[:, LOGIT_B + h:LOGIT_B + h + 1]
            kb[c, h] = kn[c, h] * bcol[c, h]

    a = {(c, h): _dot_nt(cat([cat([kb[c, h], kb[c, h + 1]], axis=1), cat([qn[c, h], qn[c, h + 1]], axis=1)], axis=0),
                         _block_diag(kn[c, h], kn[c, h + 1])) for c, h in units}
    ms, attn = {}, {}
    for c, h in units:
        gcol = jnp.where(second, lcol(gcum[c], h + 1), lcol(gcum[c], h))
        grow = jnp.where(second[0:1], gcum_t2[c][LOGIT_A + h + 1:LOGIT_A + h + 2, :],
                         gcum_t2[c][LOGIT_A + h:LOGIT_A + h + 1, :])
        dec = jnp.exp(jnp.where(causal, gcol - grow, 0.0))
        ms[c, h] = jnp.where(strict, a[c, h][:GROUP] * dec, 0.0)
        attn[c, h] = jnp.where(causal, a[c, h][GROUP:] * dec, 0.0)
    t_inv = dict(zip(units, _tri_inverse_pairs([ms[un] for un in units], seq_len)))

    u, w, qdec, kdec = {}, {}, {}, {}
    for c, h in units:
        rhs = [cat([hs(v, c, hh) * bcol[c, hh], kb[c, hh] * lcol(e_cum[c], hh)], axis=1) for hh in (h, h + 1)]
        uw = _dot(t_inv[c, h], _block_diag(*rhs))
        for i, hh in enumerate((h, h + 1)):
            u[c, hh] = uw[:, 2 * i * HEAD_DIM:(2 * i + 1) * HEAD_DIM]
            w[c, hh] = uw[:, (2 * i + 1) * HEAD_DIM:(2 * i + 2) * HEAD_DIM]
            qdec[c, hh] = qn[c, hh] * lcol(e_cum[c], hh)
            kdec[c, hh] = kn[c, hh] * lcol(e_rem[c], hh)

    for c in subs:
        ws = [state.read(h, cat([w[c, h], qdec[c, h]], axis=0)) for h in heads]
        v_new = [u[c, h] - ws[h][0] for h in heads]
        vn_bd = {h: _block_diag(v_new[h], v_new[h + 1]) for h in pairs}
        o_pair = {h: cat([ws[h][1], ws[h + 1][1]], axis=1) + _dot(attn[c, h], vn_bd[h]) for h in pairs}
        for h in pairs:
            state.update(h, cat([kdec[c, h], kdec[c, h + 1]], axis=0).T, vn_bd[h],
                         lcol(e_tot[c], h), lcol(e_tot[c], h + 1))
        for h in heads:
            o = o_pair[h - h % 2][:, (h % 2) * HEAD_DIM:(h % 2 + 1) * HEAD_DIM]
            o = o * lax.rsqrt(jnp.mean(o * o, axis=-1, keepdims=True) + NORM_EPS) * nw_row
            zh = hs(z, c, h)
            og_ref[c * GROUP:(c + 1) * GROUP, h * HEAD_DIM:(h + 1) * HEAD_DIM] = (
                o * (zh * _sigmoid(zh))).astype(og_ref.dtype)


def _silu(x):
    return x * _sigmoid(x)


def _mixer_common(x_ref, wl_ref, abc_ref, z_ref, par_ref, conv_a, qkv_act, seq_len, state, ya_ref, og_ref):
    gb = abc_ref[:, COL_GB:COL_GB + D_CONV]
    gc = abc_ref[:, COL_GC:COL_GC + D_CONV]
    hv = abc_ref[:, COL_H:COL_H + D_CONV]
    u = gc * hv
    ya_ref[...] = (gb * conv_a(u)).astype(ya_ref.dtype)

    act = qkv_act()
    q, k, v = act[:, :D_GDN], act[:, D_GDN:2 * D_GDN], act[:, 2 * D_GDN:]
    logit = lax.dot_general(x_ref[...], wl_ref[...], (((1,), (1,)), ((), ())), preferred_element_type=F32)
    _gdn_group(q, k, v, z_ref[...], logit, par_ref[0:1, :], par_ref[1:2, :], par_ref[2:3, :],
               seq_len, state, og_ref)
    return u


def _prompt_mixer_kernel(x_ref, wl_ref, abc_ref, qkv_ref, z_ref, caw_ref, gcw_ref, par_ref,
                         ya_ref, og_ref, utail_ref, gtail_ref, s_ref, ubuf_ref, uout_ref, gbuf_ref, gout_ref):
    @pl.when(pl.program_id(1) == 0)
    def _():
        ubuf_ref[:, 0:8, :] = jnp.zeros((D_CONV // LANE, 8, LANE), F32)
        gbuf_ref[:, 0:8, :] = jnp.zeros((3 * D_GDN // LANE, 8, LANE), F32)
        s_ref[...] = jnp.zeros_like(s_ref)

    cur = {}

    class State:
        @staticmethod
        def read(h, lhs):
            if h not in cur:
                cur[h] = s_ref[0, h]
            ws = _dot(lhs, cur[h])
            return ws[:GROUP], ws[GROUP:]

        @staticmethod
        def update(h, kd_t, vn_bd, e0, e1):
            kv = _dot(kd_t, vn_bd)
            cur[h] = cur[h] * e0[0:1, :] + kv[:, :HEAD_DIM]
            cur[h + 1] = cur[h + 1] * e1[0:1, :] + kv[:, HEAD_DIM:]

        @staticmethod
        def finish():
            for h, s in cur.items():
                s_ref[0, h] = s

    def conv_a(u):
        _prompt_conv(u, caw_ref, ubuf_ref, uout_ref)
        return _slabs_value(uout_ref)

    def qkv_act():
        _prompt_conv(qkv_ref[...], gcw_ref, gbuf_ref, gout_ref, _silu)
        return _slabs_value(gout_ref)

    u = _mixer_common(x_ref, wl_ref, abc_ref, z_ref, par_ref, conv_a, qkv_act, GROUP, State, ya_ref, og_ref)
    n = u.shape[0]
    utail_ref[0] = u[n - 8:]
    gtail_ref[0] = qkv_ref[n - 8:, :]


def _sample_mixer_kernel(x_ref, wl_ref, abc_ref, qkv_ref, z_ref, caw_ref, gcw_ref, par_ref,
                         sta_ref, stg_ref, s0_ref, *rest):
    ya_ref, og_ref, u_ref, s_ref, fill_a_ref, fill_g_ref = rest[-6:]

    @pl.when(pl.program_id(0) == 0)
    def _():
        fill_a_ref[...] = jnp.zeros_like(fill_a_ref)
        fill_g_ref[...] = jnp.zeros_like(fill_g_ref)

    row8 = lax.broadcasted_iota(jnp.int32, (8, HEAD_DIM), 0)
    first = row8 < SAMPLE_LEN
    row_bd = lax.broadcasted_iota(jnp.int32, (2 * GROUP, 2 * HEAD_DIM), 0) & (GROUP - 1)

    class State:
        @staticmethod
        def read(h, lhs):
            w, qdec = lhs[:GROUP], lhs[GROUP:]
            ws_w, ws_q = [], []
            for p in range(GROUP // 8):
                slab = jnp.concatenate([w[8 * p:8 * p + 8], qdec[8 * p:8 * p + 8]], axis=0)
                r0 = _dot(slab, s0_ref[2 * p, h])
                r1 = _dot(slab, s0_ref[2 * p + 1, h])
                ws_w.append(jnp.where(first, r0[:8], r1[:8]))
                ws_q.append(jnp.where(first, r0[8:], r1[8:]))
            return jnp.concatenate(ws_w, axis=0), jnp.concatenate(ws_q, axis=0)

        @staticmethod
        def update(h, kd_t, vn_bd, e0, e1):
            for s in range(SEQ_PER_GROUP):
                in_seq = _same_block(row_bd, SAMPLE_LEN * s, SAMPLE_LEN)
                kv = _dot(kd_t, jnp.where(in_seq, vn_bd, 0.0))
                r = SAMPLE_LEN * s
                s_ref[s, h] = s0_ref[s, h] * e0[r:r + 1, :] + kv[:, :HEAD_DIM]
                s_ref[s, h + 1] = s0_ref[s, h + 1] * e1[r:r + 1, :] + kv[:, HEAD_DIM:]

        @staticmethod
        def finish():
            pass

    def qkv_act():
        raw = qkv_ref[...]
        return _silu(_causal_conv(raw, gcw_ref, _sample_prev(raw, stg_ref, fill_g_ref)))

    u_ref[...] = _mixer_common(
        x_ref, wl_ref, abc_ref, z_ref, par_ref,
        lambda u: _causal_conv(u, caw_ref, _sample_prev(u, sta_ref, fill_a_ref)), qkv_act,
        SAMPLE_LEN, State, ya_ref, og_ref)


def _mixer_in_specs(layer, row_block, n_rows=GROUP):
    rows = lambda width, col_block: pl.BlockSpec((n_rows, width), lambda *i: (row_block(*i), col_block))
    return [
        rows(D_MODEL, 0),
        _layer_spec(layer, (LANE, D_MODEL)),
        rows(3 * D_CONV, 0),
        rows(3 * D_GDN, COL_QKV // (3 * D_GDN)),
        rows(D_GDN, COL_Z // D_GDN),
        _layer_spec(layer, (3, D_CONV)),
        _layer_spec(layer, (4, 3 * D_GDN)),
        _layer_spec(layer, (8, LANE)),
    ]


def _prompt_mixer(layer, x_bf, w_logit, proj, caw, gcw, par, batch, seq):
    step = GROUP * PROMPT_SUB
    chunks = seq // step
    rb = lambda b, c: b * chunks + c
    n = batch * seq
    return pl.pallas_call(
        _prompt_mixer_kernel,
        grid=(batch, chunks),
        in_specs=_mixer_in_specs(layer, rb, step),
        out_specs=[
            pl.BlockSpec((step, D_CONV), lambda b, c: (rb(b, c), 0)),
            pl.BlockSpec((step, D_GDN), lambda b, c: (rb(b, c), 0)),
            pl.BlockSpec((1, 8, D_CONV), lambda b, c: (b, 0, 0)),
            pl.BlockSpec((1, 8, 3 * D_GDN), lambda b, c: (b, 0, 0)),
            pl.BlockSpec((1, N_HEADS, HEAD_DIM, HEAD_DIM), lambda b, c: (b, 0, 0, 0)),
        ],
        out_shape=[
            jax.ShapeDtypeStruct((n, D_CONV), BF16),
            jax.ShapeDtypeStruct((n, D_GDN), BF16),
            jax.ShapeDtypeStruct((batch, 8, D_CONV), F32),
            jax.ShapeDtypeStruct((batch, 8, 3 * D_GDN), F32),
            jax.ShapeDtypeStruct((batch, N_HEADS, HEAD_DIM, HEAD_DIM), F32),
        ],
        scratch_shapes=[pltpu.VMEM((D_CONV // LANE, step + 8, LANE), F32),
                        pltpu.VMEM((D_CONV // LANE, step, LANE), F32),
                        pltpu.VMEM((3 * D_GDN // LANE, step + 8, LANE), F32),
                        pltpu.VMEM((3 * D_GDN // LANE, step, LANE), F32)],
        compiler_params=_params("arbitrary", "arbitrary"),
        name="prompt_mixer",
    )(x_bf, w_logit, proj, proj, proj, caw, gcw, par)


def _sample_mixer(layer, x_bf, w_logit, proj, caw, gcw, par, sta, stg, s0, s_prev, first_block):
    n_seq = s0.shape[1]
    rb = lambda g: first_block + g
    state_block = (SEQ_PER_GROUP, N_HEADS, HEAD_DIM, HEAD_DIM)
    in_specs = _mixer_in_specs(layer, rb) + [
        _layer_spec(layer, (sta.shape[1], SEQ_PER_GROUP, D_CONV), lambda g: (0, g, 0)),
        _layer_spec(layer, (stg.shape[1], SEQ_PER_GROUP, 3 * D_GDN), lambda g: (0, g, 0)),
        _layer_spec(layer, state_block, lambda g: (g, 0, 0, 0)),
    ]
    args = [x_bf, w_logit, proj, proj, proj, caw, gcw, par, sta, stg, s0]
    aliases = {}
    if s_prev is not None:
        in_specs.append(pl.BlockSpec(memory_space=pl.ANY))
        aliases = {len(args): 3}
        args.append(s_prev)
    return pl.pallas_call(
        _sample_mixer_kernel,
        grid=(n_seq // SEQ_PER_GROUP,),
        in_specs=in_specs,
        out_specs=[
            pl.BlockSpec((GROUP, D_CONV), lambda g: (g, 0)),
            pl.BlockSpec((GROUP, D_GDN), lambda g: (g, 0)),
            pl.BlockSpec((GROUP, D_CONV), lambda g: (g, 0)),
            _layer_spec(layer, state_block, lambda g: (g, 0, 0, 0)),
        ],
        out_shape=[
            jax.ShapeDtypeStruct((n_seq * SAMPLE_LEN, D_CONV), BF16),
            jax.ShapeDtypeStruct((n_seq * SAMPLE_LEN, D_GDN), BF16),
            jax.ShapeDtypeStruct((n_seq * SAMPLE_LEN, D_CONV), F32),
            jax.ShapeDtypeStruct(s0.shape, F32),
        ],
        input_output_aliases=aliases,
        scratch_shapes=[pltpu.VMEM((sta.shape[1], D_CONV // LANE, GROUP, LANE), F32),
                        pltpu.VMEM((stg.shape[1], 3 * D_GDN // LANE, GROUP, LANE), F32)],
        compiler_params=_params("arbitrary"),
        name="sample_mixer",
    )(*args)


def _out_proj_kernel(ya_lo, ya_hi, og_lo, og_hi, gate_ref, x_lo, x_hi, wa_ref, wb_ref, wo_ref, g_ref, b_ref,
                     o_ref, *, n_lo):
    lo = pl.program_id(0) < n_lo
    half = o_ref.shape[0] // 2
    halves = (slice(0, half), slice(half, 2 * half))
    pick = lambda a, b, r: jnp.where(lo, a[r, :], b[r, :])
    dot = lambda a, w_ref: jnp.dot(a, w_ref[...], preferred_element_type=F32)
    y_a = [dot(pick(ya_lo, ya_hi, r), wa_ref) for r in halves]
    y_b = [dot(pick(og_lo, og_hi, r), wb_ref) for r in halves]
    merged = [gate_ref[r, :D_MODEL].astype(F32) * a + gate_ref[r, D_MODEL:].astype(F32) * b
              for r, a, b in zip(halves, y_a, y_b)]
    z = [dot(m.astype(BF16), wo_ref) for m in merged]
    for r, zz in zip(halves, z):
        o_ref[r, :] = _layer_norm(ALPHA * pick(x_lo, x_hi, r) + zz, g_ref[...], b_ref[...])


def _out_proj(layer, ya_p, ya_s, og_p, og_s, gates, x_lo, x_hi, hi_block0, wa, wb, wo, g, b, tm):
    t = gates.shape[0]
    n_lo = ya_p.shape[0] // tm
    lo = lambda width: pl.BlockSpec((tm, width), lambda i: (jnp.minimum(i, n_lo - 1), 0))
    hi = lambda width, block0=0: pl.BlockSpec((tm, width), lambda i: (jnp.maximum(i - n_lo, 0) + block0, 0))
    const = lambda shape: _layer_spec(layer, shape, pipeline_mode=pl.Buffered(1))
    weight = lambda shape: pl.BlockSpec(shape, lambda i: (0, 0), pipeline_mode=pl.Buffered(1))
    return pl.pallas_call(
        functools.partial(_out_proj_kernel, n_lo=n_lo),
        grid=(t // tm,),
        in_specs=[
            lo(D_CONV), hi(D_CONV), lo(D_GDN), hi(D_GDN),
            pl.BlockSpec((tm, 2 * D_MODEL), lambda i: (i, 0)),
            lo(D_MODEL), hi(D_MODEL, hi_block0),
            weight((D_CONV, D_MODEL)), weight((D_GDN, D_MODEL)), weight((D_MODEL, D_MODEL)),
            const((1, D_MODEL)), const((1, D_MODEL)),
        ],
        out_specs=pl.BlockSpec((tm, D_MODEL), lambda i: (i, 0)),
        out_shape=jax.ShapeDtypeStruct((t, D_MODEL), F32),
        compiler_params=_params("arbitrary"),
        name="out_proj",
    )(ya_p, ya_s, og_p, og_s, gates, x_lo, x_hi, wa, wb, wo, g, b)


def _mlp_kernel(x_ref, wu_ref, wd_ref, g_ref, b_ref, o0_ref, o1_ref, xb_ref, acc_ref, *, n_lo):
    i, j = pl.program_id(0), pl.program_id(1)

    @pl.when(j == 0)
    def _():
        xb_ref[...] = x_ref[...].astype(BF16)
        acc_ref[...] = jnp.zeros_like(acc_ref)

    hid = jnp.maximum(jnp.dot(xb_ref[...], wu_ref[...], preferred_element_type=F32), 0.0)
    acc_ref[...] += jnp.dot((hid * hid).astype(BF16), wd_ref[...], preferred_element_type=F32)

    @pl.when(j == pl.num_programs(1) - 1)
    def _():
        y = _layer_norm(ALPHA * x_ref[...] + acc_ref[...], g_ref[...], b_ref[...])
        if n_lo is None:
            o0_ref[...] = y
            o1_ref[...] = y.astype(BF16)
        else:
            @pl.when(i < n_lo)
            def _():
                o0_ref[...] = y

            @pl.when(i >= n_lo)
            def _():
                o1_ref[...] = y


def _mlp(layer, x, wu, wd, g, b, tm, tf, n_prompt=None):
    t = x.shape[0]
    if n_prompt is None:
        n_lo = None
        out_specs = [pl.BlockSpec((tm, D_MODEL), lambda i, j: (i, 0)),
                     pl.BlockSpec((tm, D_MODEL), lambda i, j: (i, 0))]
        out_shape = [jax.ShapeDtypeStruct((t, D_MODEL), F32), jax.ShapeDtypeStruct((t, D_MODEL), BF16)]
    else:
        n_lo = n_prompt // tm
        out_specs = [pl.BlockSpec((tm, D_MODEL), lambda i, j: (jnp.minimum(i, n_lo - 1), 0)),
                     pl.BlockSpec((tm, D_MODEL), lambda i, j: (jnp.maximum(i - n_lo, 0), 0))]
        out_shape = [jax.ShapeDtypeStruct((n_prompt, D_MODEL), F32),
                     jax.ShapeDtypeStruct((t - n_prompt, D_MODEL), F32)]
    return pl.pallas_call(
        functools.partial(_mlp_kernel, n_lo=n_lo),
        grid=(t // tm, D_FF // tf),
        in_specs=[
            pl.BlockSpec((tm, D_MODEL), lambda i, j: (i, 0)),
            pl.BlockSpec((D_MODEL, tf), lambda i, j: (0, j)),
            pl.BlockSpec((tf, D_MODEL), lambda i, j: (j, 0)),
            _layer_spec(layer, (1, D_MODEL)),
            _layer_spec(layer, (1, D_MODEL)),
        ],
        out_specs=out_specs,
        out_shape=out_shape,
        scratch_shapes=[pltpu.VMEM((tm, D_MODEL), BF16), pltpu.VMEM((tm, D_MODEL), F32)],
        compiler_params=_params("arbitrary", "arbitrary"),
        name="mlp",
    )(x, wu, wd, g, b)


def kernel(x_prompt, x_sample, state_conv_a, state_gdn_conv, state_gdn, w_in, conv_a_w, gdn_conv_w, a_log,
           dt_bias, gdn_norm_w, w_a_out, w_b_out, w_o, ln1_g, ln1_b, w_up, w_down, ln2_g, ln2_b):
    batch, seq, _ = x_prompt.shape
    n_seq = x_sample.shape[0]
    n_prompt = batch * seq
    n_sample = n_seq * SAMPLE_LEN
    tm_proj, tm_out, tm_mlp = 1088, 256, 512
    assert x_sample.shape[1] == SAMPLE_LEN and seq % (GROUP * PROMPT_SUB) == 0 and n_seq % SEQ_PER_GROUP == 0
    assert n_prompt % tm_out == 0 and n_sample % tm_out == 0 and n_prompt % tm_mlp == 0 and n_sample == tm_mlp

    xp = x_prompt.reshape(n_prompt, D_MODEL)
    xs = x_sample.reshape(n_sample, D_MODEL)
    x_bf = _bf16_stream(xp, xs, tm_mlp)
    x_lo, x_hi, hi_block0 = xp, xs, 0

    w_t = jnp.swapaxes(w_in, 1, 2)
    w_logit = jnp.pad(w_t[:, COL_LOGIT:COL_GATE], ((0, 0), (0, LANE - 2 * N_HEADS), (0, 0))).astype(BF16)
    par = jnp.zeros((DEPTH, 8, LANE), F32)
    par = par.at[:, 0, LOGIT_A:LOGIT_A + N_HEADS].set(a_log.astype(F32))
    par = par.at[:, 1, LOGIT_A:LOGIT_A + N_HEADS].set(dt_bias.astype(F32))
    par = par.at[:, 2, :].set(gdn_norm_w.astype(F32))
    sta, stg = jnp.swapaxes(state_conv_a, 1, 2), jnp.swapaxes(state_gdn_conv, 1, 2)
    row3 = lambda a: a.reshape(DEPTH, 1, D_MODEL)

    ca_p, cg_p, sg_p, ca_s, cg_s = [], [], [], [], []
    sg_s = None
    for l in range(DEPTH):
        proj = _proj(x_bf, w_t, l, 0, N_MIX, F32, False, tm_proj, 1024, "proj")
        gates, wu, wd, wa, wb, wo = _proj(x_bf, w_t, l, COL_GATE, 2 * D_MODEL, BF16, True, tm_proj, 1024, "gates",
                                          cast=(w_up, w_down, w_a_out, w_b_out, w_o))

        ya_p, og_p, utail, gtail, s_p = _prompt_mixer(l, x_bf, w_logit, proj, conv_a_w, gdn_conv_w, par, batch, seq)
        ya_s, og_s, u_s, sg_s = _sample_mixer(l, x_bf, w_logit, proj, conv_a_w, gdn_conv_w, par, sta, stg,
                                              state_gdn, sg_s, n_prompt // GROUP)

        x1 = _out_proj(l, ya_p, ya_s, og_p, og_s, gates, x_lo, x_hi, hi_block0, wa, wb, wo,
                       row3(ln1_g), row3(ln1_b), tm_out)
        if l < DEPTH - 1:
            x, x_bf = _mlp(l, x1, wu, wd, row3(ln2_g), row3(ln2_b), tm_mlp, 1024)
            x_lo, x_hi, hi_block0 = x, x, n_prompt // tm_out
        else:
            y_p, y_s = _mlp(l, x1, wu, wd, row3(ln2_g), row3(ln2_b), tm_mlp, 1024, n_prompt=n_prompt)

        ca_p.append(utail[:, 6:8])
        cg_p.append(gtail[:, 5:8])
        sg_p.append(s_p)
        ca_s.append(u_s.reshape(n_seq, SAMPLE_LEN, D_CONV)[:, 2:4])
        cg_s.append(proj[n_prompt:, COL_QKV:COL_QKV + 3 * D_GDN].reshape(n_seq, SAMPLE_LEN, 3 * D_GDN)[:, 1:4])

    return (y_p.reshape(batch, seq, D_MODEL), y_s.reshape(n_seq, SAMPLE_LEN, D_MODEL),
            jnp.stack(ca_p), jnp.stack(cg_p), jnp.stack(sg_p), jnp.stack(ca_s), jnp.stack(cg_s), sg_s)
```

```python
import functools

import jax
import jax.numpy as jnp
from jax import lax
from jax.experimental import pallas as pl
from jax.experimental.pallas import tpu as pltpu

F32 = jnp.float32
BF16 = jnp.bfloat16

D_MODEL = 2048
DEPTH = 2
D_CONV = D_MODEL // 2
N_HEADS = 8
HEAD_DIM = 128
D_GDN = N_HEADS * HEAD_DIM
D_FF = 4 * D_MODEL
ALPHA = (2 * DEPTH) ** 0.25
LN_EPS = 1e-5
NORM_EPS = 1e-6

COL_GB, COL_GC, COL_H = 0, D_CONV, 2 * D_CONV
COL_QKV = 3 * D_CONV
COL_Z = COL_QKV + 3 * D_GDN
N_MIX = COL_Z + D_GDN
COL_LOGIT = N_MIX
COL_GATE = COL_LOGIT + 2 * N_HEADS
LANE = 128
SUBLANE = 8
LOGIT_B, LOGIT_A = 0, N_HEADS

GROUP = 64
PROMPT_SUB = 4
SAMPLE_LEN = 4
SEQ_PER_GROUP = GROUP // SAMPLE_LEN

VMEM_LIMIT = 56 * 1024 * 1024

TM_PROJ, TN_PROJ = 1088, 1024
TM_OUT = 256
TM_MLP, TF_MLP = 512, 1024


def _dot(a, b):
    return jnp.dot(a.astype(BF16), b.astype(BF16), preferred_element_type=F32)


def _dot_nt(a, b):
    return lax.dot_general(a.astype(BF16), b.astype(BF16), (((1,), (1,)), ((), ())),
                           preferred_element_type=F32)


def _dot_exact_lhs(m01, x):
    x1 = x.astype(BF16)
    r1 = x - x1.astype(F32)
    x2 = r1.astype(BF16)
    x3 = (r1 - x2.astype(F32)).astype(BF16)
    d = lambda y: jnp.dot(m01, y, preferred_element_type=F32)
    return d(x1) + (d(x2) + d(x3))


def _same_block(i, j, size):
    shift = size.bit_length() - 1
    assert size == 1 << shift
    return (i >> shift) == (j >> shift)


def _sigmoid(x):
    return 0.5 * jnp.tanh(0.5 * x) + 0.5


def _softplus(x):
    return jnp.maximum(x, 0.0) + jnp.log1p(jnp.exp(-jnp.abs(x)))


def _layer_norm(y, g, b):
    mu = jnp.mean(y, axis=-1, keepdims=True)
    yc = y - mu
    var = jnp.mean(yc * yc, axis=-1, keepdims=True)
    return yc * lax.rsqrt(var + LN_EPS) * g + b


def _layer_spec(layer, shape, index_map=None, **kw):
    if index_map is None:
        index_map = lambda *g: (0,) * len(shape)
    return pl.BlockSpec((None,) + tuple(shape), lambda *g: (layer,) + tuple(index_map(*g)), **kw)


def _params(*sem):
    return pltpu.CompilerParams(dimension_semantics=sem, vmem_limit_bytes=VMEM_LIMIT)


def _stream_kernel(lo_ref, hi_ref, o_ref, *, n_lo):
    o_ref[...] = jnp.where(pl.program_id(0) < n_lo, lo_ref[...], hi_ref[...]).astype(o_ref.dtype)


def _bf16_stream(x_lo, x_hi, tm):
    n_lo, n_hi = x_lo.shape[0] // tm, x_hi.shape[0] // tm
    return pl.pallas_call(
        functools.partial(_stream_kernel, n_lo=n_lo),
        grid=(n_lo + n_hi,),
        in_specs=[pl.BlockSpec((tm, D_MODEL), lambda i: (jnp.minimum(i, n_lo - 1), 0)),
                  pl.BlockSpec((tm, D_MODEL), lambda i: (jnp.maximum(i - n_lo, 0), 0))],
        out_specs=pl.BlockSpec((tm, D_MODEL), lambda i: (i, 0)),
        out_shape=jax.ShapeDtypeStruct((x_lo.shape[0] + x_hi.shape[0], D_MODEL), BF16),
        compiler_params=_params("arbitrary"),
        name="bf16_stream",
    )(x_lo, x_hi)


def _proj_kernel(x_ref, wt_hbm, *rest, gate, n_cast, layer, col0):
    cast_in, o_ref, cast_out = rest[:n_cast], rest[n_cast], rest[n_cast + 1:2 * n_cast + 1]
    wb_ref, wf_ref, sem = rest[-3:]
    j, n_j = pl.program_id(0), pl.num_programs(0)
    tn = wb_ref.shape[0]

    def weight_copy(block, slot):
        rows = pl.ds(pl.multiple_of(col0 + block * tn, 2 * SUBLANE), tn)
        return pltpu.make_async_copy(wt_hbm.at[layer, rows, :], wf_ref.at[slot], sem.at[slot])

    @pl.when(pl.program_id(1) == 0)
    def _():
        slot = j % 2

        @pl.when(j == 0)
        def _():
            weight_copy(0, 0).start()

        weight_copy(j, slot).wait()
        wb_ref[...] = wf_ref[slot].astype(BF16)

        @pl.when(j + 1 < n_j)
        def _():
            weight_copy(j + 1, 1 - slot).start()

    y = lax.dot_general(x_ref[...], wb_ref[...], (((1,), (1,)), ((), ())), preferred_element_type=F32)
    if gate:
        y = _sigmoid(y)
    o_ref[...] = y.astype(o_ref.dtype)
    for src, dst in zip(cast_in, cast_out):
        dst[...] = src[...].astype(dst.dtype)


def _proj(x_bf, w_t, layer, col0, n_cols, out_dtype, gate, tm, tn, name, cast=()):
    t = x_bf.shape[0]
    n_j, n_i = n_cols // tn, t // tm
    step = lambda j, i: j * n_i + i
    rows = [w.shape[1] // (n_j * n_i) for w in cast]
    assert all(r % 16 == 0 and r * n_j * n_i == w.shape[1] for r, w in zip(rows, cast))
    outs = pl.pallas_call(
        functools.partial(_proj_kernel, gate=gate, n_cast=len(cast), layer=layer, col0=col0),
        grid=(n_j, n_i),
        in_specs=[pl.BlockSpec((tm, D_MODEL), lambda j, i: (i, 0)),
                  pl.BlockSpec(memory_space=pl.ANY)]
        + [_layer_spec(layer, (r, w.shape[2]), lambda j, i: (step(j, i), 0)) for r, w in zip(rows, cast)],
        out_specs=[pl.BlockSpec((tm, tn), lambda j, i: (i, j))]
        + [pl.BlockSpec((r, w.shape[2]), lambda j, i: (step(j, i), 0)) for r, w in zip(rows, cast)],
        out_shape=[jax.ShapeDtypeStruct((t, n_cols), out_dtype)]
        + [jax.ShapeDtypeStruct(w.shape[1:], BF16) for w in cast],
        scratch_shapes=[pltpu.VMEM((tn, D_MODEL), BF16), pltpu.VMEM((2, tn, D_MODEL), F32),
                        pltpu.SemaphoreType.DMA((2,))],
        compiler_params=_params("arbitrary", "arbitrary"),
        name=name,
    )(x_bf, w_t, *cast)
    return outs if cast else outs[0]


def _causal_conv(u, w_ref, prev_fn):
    k = w_ref.shape[0]
    out = u * w_ref[k - 1:k, :]
    for j in range(1, k):
        out = out + prev_fn(j) * w_ref[k - 1 - j:k - j, :]
    return out


CONV_PHASES = 4


def _prompt_conv(u, w_ref, buf_ref, out_ref, act=None, scale=None):
    k = w_ref.shape[0]
    n, c = u.shape
    slabs = c // LANE
    for s in range(slabs):
        buf_ref[s, SUBLANE:, :] = u[:, s * LANE:(s + 1) * LANE]
    for s in range(slabs):
        w = [w_ref[j:j + 1, s * LANE:(s + 1) * LANE] for j in range(k)]
        if scale is not None:
            w = [x * scale for x in w]
        for b in range(CONV_PHASES):
            acc = None
            for j in range(k):
                tap = buf_ref[s, pl.ds(SUBLANE - (k - 1) + j + b, n // CONV_PHASES, stride=CONV_PHASES), :]
                acc = tap * w[j] if acc is None else acc + tap * w[j]
            out_ref[s, pl.ds(b, n // CONV_PHASES, stride=CONV_PHASES), :] = acc if act is None else act(acc)
    for s in range(slabs):
        buf_ref[s, 0:SUBLANE, :] = buf_ref[s, n:n + SUBLANE, :]


def _slabs_value(ref):
    return jnp.concatenate([ref[s] for s in range(ref.shape[0])], axis=1)


def _sample_prev(u, st_ref, fill_ref):
    t = lax.broadcasted_iota(jnp.int32, u.shape, 0) & (SAMPLE_LEN - 1)
    k1 = st_ref.shape[0]
    slabs = u.shape[1] // LANE

    def prev(j):
        for tt in range(j):
            for s in range(slabs):
                fill_ref[j - 1, s, pl.ds(tt, SEQ_PER_GROUP, stride=SAMPLE_LEN), :] = (
                    st_ref[k1 - j + tt, :, s * LANE:(s + 1) * LANE])
        fill = jnp.concatenate([fill_ref[j - 1, s] for s in range(slabs)], axis=1)
        return jnp.where(t >= j, pltpu.roll(u, j, 0), fill)
    return prev


def _pair_iotas():
    row = lax.broadcasted_iota(jnp.int32, (GROUP, 2 * GROUP), 0)
    lane = lax.broadcasted_iota(jnp.int32, (GROUP, 2 * GROUP), 1)
    return row, lane & (GROUP - 1), lane >= GROUP


def _block_diag(y0, y1):
    z = jnp.zeros_like(y0)
    return jnp.concatenate([jnp.concatenate([y0, z], axis=1), jnp.concatenate([z, y1], axis=1)], axis=0)


def _pair_block_diag(y):
    _, _, second = _pair_iotas()
    return jnp.concatenate([jnp.where(second, 0.0, y), jnp.where(second, y, 0.0)], axis=0)


def _tri_inverse_pairs(ms, seq_len):
    row, col, _ = _pair_iotas()
    eye = (row == col).astype(F32)
    base = min(8, seq_len)
    mm = lambda x, y: _dot(x, _pair_block_diag(y))
    same = lambda b: _same_block(row, col, b)
    d = [jnp.where(same(base), m, 0.0) for m in ms]
    d2 = [mm(x, x) for x in d]
    xs = [eye - x for x in d]
    if base > 4:
        d4 = [mm(y, y) for y in d2]
    inv = [x + mm(x, y) for x, y in zip(xs, d2)]
    if base > 4:
        inv = [i + mm(i, y) for i, y in zip(inv, d4)]
    b = base * 2
    while b <= seq_len:
        off_diag = same(b) & jnp.logical_not(same(b // 2))
        t = [mm(i, jnp.where(off_diag, m, 0.0)) for i, m in zip(inv, ms)]
        inv = [i - mm(x, i) for i, x in zip(inv, t)]
        b *= 2
    return inv


def _gdn_group(q, k, v, z, logit, a_row, dt_row, nw_row, seq_len, state, og_ref):
    row = lax.broadcasted_iota(jnp.int32, (GROUP, GROUP), 0)
    col = lax.broadcasted_iota(jnp.int32, (GROUP, GROUP), 1)
    same_seq = _same_block(row, col, seq_len)
    lmat = jnp.concatenate([(same_seq & (row >= col)).astype(F32), same_seq.astype(F32)], axis=0).astype(BF16)
    prow, pcol, second = _pair_iotas()
    psame = _same_block(prow, pcol, seq_len)
    causal = psame & (prow >= pcol)
    strict = psame & (prow > pcol)

    subs = range(q.shape[0] // GROUP)
    heads = range(N_HEADS)
    pairs = range(0, N_HEADS, 2)
    rows = lambda x, c: x[c * GROUP:(c + 1) * GROUP]
    hs = lambda x, c, h: x[c * GROUP:(c + 1) * GROUP, h * HEAD_DIM:(h + 1) * HEAD_DIM]
    lcol = lambda x, h: x[:, LOGIT_A + h:LOGIT_A + h + 1]
    cat = jnp.concatenate

    beta = _sigmoid(logit)
    g = -jnp.exp(a_row) * _softplus(logit + dt_row)
    units = [(c, h) for c in subs for h in pairs]
    gc2 = {c: _dot_exact_lhs(lmat, rows(g, c)) for c in subs}
    gcum = {c: x[:GROUP] for c, x in gc2.items()}
    gtot = {c: x[GROUP:] for c, x in gc2.items()}
    gcum_t2 = {c: cat([x, x], axis=0).T for c, x in gcum.items()}
    e_cum = {c: jnp.exp(x) for c, x in gcum.items()}
    e_rem = {c: jnp.exp(gtot[c] - gcum[c]) for c in subs}
    e_tot = {c: jnp.exp(t) for c, t in gtot.items()}

    qn, kn, bcol, kb = {}, {}, {}, {}
    for c in subs:
        for h in heads:
            qh, kh = hs(q, c, h), hs(k, c, h)
            qn[c, h] = qh * (lax.rsqrt(jnp.sum(qh * qh, axis=-1, keepdims=True) + NORM_EPS) * (HEAD_DIM ** -0.5))
            kn[c, h] = kh * lax.rsqrt(jnp.sum(kh * kh, axis=-1, keepdims=True) + NORM_EPS)
            bcol[c, h] = rows(beta, c)[:, LOGIT_B + h:LOGIT_B + h + 1]
            kb[c, h] = kn[c, h] * bcol[c, h]

    a = {(c, h): _dot_nt(cat([cat([kb[c, h], kb[c, h + 1]], axis=1), cat([qn[c, h], qn[c, h + 1]], axis=1)], axis=0),
                         _block_diag(kn[c, h], kn[c, h + 1])) for c, h in units}
    ms, attn = {}, {}
    for c, h in units:
        gcol = jnp.where(second, lcol(gcum[c], h + 1), lcol(gcum[c], h))
        grow = jnp.where(second[0:1], gcum_t2[c][LOGIT_A + h + 1:LOGIT_A + h + 2, :],
                         gcum_t2[c][LOGIT_A + h:LOGIT_A + h + 1, :])
        dec = jnp.exp(jnp.where(causal, gcol - grow, 0.0))
        ms[c, h] = jnp.where(strict, a[c, h][:GROUP] * dec, 0.0)
        attn[c, h] = jnp.where(causal, a[c, h][GROUP:] * dec, 0.0)
    t_inv = dict(zip(units, _tri_inverse_pairs([ms[un] for un in units], seq_len)))

    u, w, qdec, kdec = {}, {}, {}, {}
    for c, h in units:
        rhs = [cat([hs(v, c, hh) * bcol[c, hh], kb[c, hh] * lcol(e_cum[c], hh)], axis=1) for hh in (h, h + 1)]
        uw = _dot(t_inv[c, h], _block_diag(*rhs))
        for i, hh in enumerate((h, h + 1)):
            u[c, hh] = uw[:, 2 * i * HEAD_DIM:(2 * i + 1) * HEAD_DIM]
            w[c, hh] = uw[:, (2 * i + 1) * HEAD_DIM:(2 * i + 2) * HEAD_DIM]
            qdec[c, hh] = qn[c, hh] * lcol(e_cum[c], hh)
            kdec[c, hh] = kn[c, hh] * lcol(e_rem[c], hh)

    for c in subs:
        ws = [state.read(h, cat([w[c, h], qdec[c, h]], axis=0)) for h in heads]
        v_new = [u[c, h] - ws[h][0] for h in heads]
        vn_bd = {h: _block_diag(v_new[h], v_new[h + 1]) for h in pairs}
        o_pair = {h: cat([ws[h][1], ws[h + 1][1]], axis=1) + _dot(attn[c, h], vn_bd[h]) for h in pairs}
        for h in pairs:
            state.update(h, cat([kdec[c, h], kdec[c, h + 1]], axis=0).T, vn_bd[h],
                         lcol(e_tot[c], h), lcol(e_tot[c], h + 1))
        for h in heads:
            o = o_pair[h - h % 2][:, (h % 2) * HEAD_DIM:(h % 2 + 1) * HEAD_DIM]
            o = o * lax.rsqrt(jnp.mean(o * o, axis=-1, keepdims=True) + NORM_EPS) * nw_row
            zh = hs(z, c, h)
            og_ref[c * GROUP:(c + 1) * GROUP, h * HEAD_DIM:(h + 1) * HEAD_DIM] = (
                o * _silu(zh)).astype(og_ref.dtype)
    state.finish()


def _silu_of_half(h):
    return h * (jnp.tanh(h) + 1.0)


def _silu(x):
    return _silu_of_half(0.5 * x)


def _mixer_common(x_ref, wl_ref, abc_ref, z_ref, par_ref, conv_a, qkv_act, seq_len, state, ya_ref, og_ref):
    gb = abc_ref[:, COL_GB:COL_GB + D_CONV]
    gc = abc_ref[:, COL_GC:COL_GC + D_CONV]
    hv = abc_ref[:, COL_H:COL_H + D_CONV]
    u = gc * hv
    ya_ref[...] = (gb * conv_a(u)).astype(ya_ref.dtype)

    act = qkv_act()
    q, k, v = act[:, :D_GDN], act[:, D_GDN:2 * D_GDN], act[:, 2 * D_GDN:]
    logit = lax.dot_general(x_ref[...], wl_ref[...], (((1,), (1,)), ((), ())), preferred_element_type=F32)
    _gdn_group(q, k, v, z_ref[...], logit, par_ref[0:1, :], par_ref[1:2, :], par_ref[2:3, :],
               seq_len, state, og_ref)
    return u


def _prompt_mixer_kernel(x_ref, wl_ref, abc_ref, qkv_ref, z_ref, caw_ref, gcw_ref, par_ref,
                         ya_ref, og_ref, utail_ref, gtail_ref, s_ref, ubuf_ref, uout_ref, gbuf_ref, gout_ref):
    @pl.when(pl.program_id(1) == 0)
    def _():
        ubuf_ref[:, 0:SUBLANE, :] = jnp.zeros((D_CONV // LANE, SUBLANE, LANE), F32)
        gbuf_ref[:, 0:SUBLANE, :] = jnp.zeros((3 * D_GDN // LANE, SUBLANE, LANE), F32)
        s_ref[...] = jnp.zeros_like(s_ref)

    cur = {}

    class State:
        @staticmethod
        def read(h, lhs):
            if h not in cur:
                cur[h] = s_ref[0, h]
            ws = _dot(lhs, cur[h])
            return ws[:GROUP], ws[GROUP:]

        @staticmethod
        def update(h, kd_t, vn_bd, e0, e1):
            kv = _dot(kd_t, vn_bd)
            cur[h] = cur[h] * e0[0:1, :] + kv[:, :HEAD_DIM]
            cur[h + 1] = cur[h + 1] * e1[0:1, :] + kv[:, HEAD_DIM:]

        @staticmethod
        def finish():
            for h, s in cur.items():
                s_ref[0, h] = s

    def conv_a(u):
        _prompt_conv(u, caw_ref, ubuf_ref, uout_ref)
        return _slabs_value(uout_ref)

    def qkv_act():
        _prompt_conv(qkv_ref[...], gcw_ref, gbuf_ref, gout_ref, _silu_of_half, scale=0.5)
        return _slabs_value(gout_ref)

    u = _mixer_common(x_ref, wl_ref, abc_ref, z_ref, par_ref, conv_a, qkv_act, GROUP, State, ya_ref, og_ref)
    n = u.shape[0]
    utail_ref[0] = u[n - SUBLANE:]
    gtail_ref[0] = qkv_ref[n - SUBLANE:, :]


def _sample_mixer_kernel(x_ref, wl_ref, abc_ref, qkv_ref, z_ref, caw_ref, gcw_ref, par_ref,
                         sta_ref, stg_ref, s0_ref, *rest):
    ya_ref, og_ref, u_ref, gstate_ref, s_ref, fill_a_ref, fill_g_ref, raw_slab_ref = rest[-8:]

    @pl.when(pl.program_id(0) == 0)
    def _():
        fill_a_ref[...] = jnp.zeros_like(fill_a_ref)
        fill_g_ref[...] = jnp.zeros_like(fill_g_ref)

    first = lax.broadcasted_iota(jnp.int32, (SUBLANE, HEAD_DIM), 0) < SAMPLE_LEN
    row_bd = lax.broadcasted_iota(jnp.int32, (2 * GROUP, 2 * HEAD_DIM), 0) & (GROUP - 1)

    class State:
        @staticmethod
        def read(h, lhs):
            w, qdec = lhs[:GROUP], lhs[GROUP:]
            ws_w, ws_q = [], []
            for p in range(GROUP // SUBLANE):
                r = slice(SUBLANE * p, SUBLANE * (p + 1))
                slab = jnp.concatenate([w[r], qdec[r]], axis=0)
                r0 = _dot(slab, s0_ref[2 * p, h])
                r1 = _dot(slab, s0_ref[2 * p + 1, h])
                ws_w.append(jnp.where(first, r0[:SUBLANE], r1[:SUBLANE]))
                ws_q.append(jnp.where(first, r0[SUBLANE:], r1[SUBLANE:]))
            return jnp.concatenate(ws_w, axis=0), jnp.concatenate(ws_q, axis=0)

        @staticmethod
        def update(h, kd_t, vn_bd, e0, e1):
            for s in range(SEQ_PER_GROUP):
                in_seq = _same_block(row_bd, SAMPLE_LEN * s, SAMPLE_LEN)
                kv = _dot(kd_t, jnp.where(in_seq, vn_bd, 0.0))
                r = SAMPLE_LEN * s
                s_ref[s, h] = s0_ref[s, h] * e0[r:r + 1, :] + kv[:, :HEAD_DIM]
                s_ref[s, h + 1] = s0_ref[s, h + 1] * e1[r:r + 1, :] + kv[:, HEAD_DIM:]

        @staticmethod
        def finish():
            pass

    def qkv_act():
        raw = qkv_ref[...]
        for s in range(raw_slab_ref.shape[0]):
            raw_slab_ref[s] = raw[:, s * LANE:(s + 1) * LANE]
        for t in range(1, SAMPLE_LEN):
            gstate_ref[t - 1] = jnp.concatenate(
                [raw_slab_ref[s, pl.ds(t, SEQ_PER_GROUP, stride=SAMPLE_LEN), :] for s in range(raw_slab_ref.shape[0])],
                axis=1)
        return _silu(_causal_conv(raw, gcw_ref, _sample_prev(raw, stg_ref, fill_g_ref)))

    u_ref[...] = _mixer_common(
        x_ref, wl_ref, abc_ref, z_ref, par_ref,
        lambda u: _causal_conv(u, caw_ref, _sample_prev(u, sta_ref, fill_a_ref)), qkv_act,
        SAMPLE_LEN, State, ya_ref, og_ref)


def _mixer_in_specs(layer, row_block, n_rows=GROUP):
    rows = lambda width, col_block: pl.BlockSpec((n_rows, width), lambda *i: (row_block(*i), col_block))
    return [
        rows(D_MODEL, 0),
        _layer_spec(layer, (LANE, D_MODEL)),
        rows(3 * D_CONV, 0),
        rows(3 * D_GDN, COL_QKV // (3 * D_GDN)),
        rows(D_GDN, COL_Z // D_GDN),
        _layer_spec(layer, (3, D_CONV)),
        _layer_spec(layer, (4, 3 * D_GDN)),
        _layer_spec(layer, (SUBLANE, LANE)),
    ]


def _prompt_mixer(layer, x_bf, w_logit, proj, caw, gcw, par, batch, seq):
    step = GROUP * PROMPT_SUB
    chunks = seq // step
    rb = lambda b, c: b * chunks + c
    n = batch * seq
    return pl.pallas_call(
        _prompt_mixer_kernel,
        grid=(batch, chunks),
        in_specs=_mixer_in_specs(layer, rb, step),
        out_specs=[
            pl.BlockSpec((step, D_CONV), lambda b, c: (rb(b, c), 0)),
            pl.BlockSpec((step, D_GDN), lambda b, c: (rb(b, c), 0)),
            pl.BlockSpec((1, SUBLANE, D_CONV), lambda b, c: (b, 0, 0)),
            pl.BlockSpec((1, SUBLANE, 3 * D_GDN), lambda b, c: (b, 0, 0)),
            pl.BlockSpec((1, N_HEADS, HEAD_DIM, HEAD_DIM), lambda b, c: (b, 0, 0, 0)),
        ],
        out_shape=[
            jax.ShapeDtypeStruct((n, D_CONV), BF16),
            jax.ShapeDtypeStruct((n, D_GDN), BF16),
            jax.ShapeDtypeStruct((batch, SUBLANE, D_CONV), F32),
            jax.ShapeDtypeStruct((batch, SUBLANE, 3 * D_GDN), F32),
            jax.ShapeDtypeStruct((batch, N_HEADS, HEAD_DIM, HEAD_DIM), F32),
        ],
        scratch_shapes=[pltpu.VMEM((D_CONV // LANE, step + SUBLANE, LANE), F32),
                        pltpu.VMEM((D_CONV // LANE, step, LANE), F32),
                        pltpu.VMEM((3 * D_GDN // LANE, step + SUBLANE, LANE), F32),
                        pltpu.VMEM((3 * D_GDN // LANE, step, LANE), F32)],
        compiler_params=_params("arbitrary", "arbitrary"),
        name="prompt_mixer",
    )(x_bf, w_logit, proj, proj, proj, caw, gcw, par)


def _sample_mixer(layer, x_bf, w_logit, proj, caw, gcw, par, sta, stg, s0, s_prev, first_block):
    n_seq = s0.shape[1]
    rb = lambda g: first_block + g
    state_block = (SEQ_PER_GROUP, N_HEADS, HEAD_DIM, HEAD_DIM)
    in_specs = _mixer_in_specs(layer, rb) + [
        _layer_spec(layer, (sta.shape[1], SEQ_PER_GROUP, D_CONV), lambda g: (0, g, 0)),
        _layer_spec(layer, (stg.shape[1], SEQ_PER_GROUP, 3 * D_GDN), lambda g: (0, g, 0)),
        _layer_spec(layer, state_block, lambda g: (g, 0, 0, 0)),
    ]
    args = [x_bf, w_logit, proj, proj, proj, caw, gcw, par, sta, stg, s0]
    aliases = {}
    if s_prev is not None:
        in_specs.append(pl.BlockSpec(memory_space=pl.ANY))
        aliases = {len(args): 4}
        args.append(s_prev)
    return pl.pallas_call(
        _sample_mixer_kernel,
        grid=(n_seq // SEQ_PER_GROUP,),
        in_specs=in_specs,
        out_specs=[
            pl.BlockSpec((GROUP, D_CONV), lambda g: (g, 0)),
            pl.BlockSpec((GROUP, D_GDN), lambda g: (g, 0)),
            pl.BlockSpec((GROUP, D_CONV), lambda g: (g, 0)),
            pl.BlockSpec((SAMPLE_LEN - 1, SEQ_PER_GROUP, 3 * D_GDN), lambda g: (0, g, 0)),
            _layer_spec(layer, state_block, lambda g: (g, 0, 0, 0)),
        ],
        out_shape=[
            jax.ShapeDtypeStruct((n_seq * SAMPLE_LEN, D_CONV), BF16),
            jax.ShapeDtypeStruct((n_seq * SAMPLE_LEN, D_GDN), BF16),
            jax.ShapeDtypeStruct((n_seq * SAMPLE_LEN, D_CONV), F32),
            jax.ShapeDtypeStruct((SAMPLE_LEN - 1, n_seq, 3 * D_GDN), F32),
            jax.ShapeDtypeStruct(s0.shape, F32),
        ],
        input_output_aliases=aliases,
        scratch_shapes=[pltpu.VMEM((sta.shape[1], D_CONV // LANE, GROUP, LANE), F32),
                        pltpu.VMEM((stg.shape[1], 3 * D_GDN // LANE, GROUP, LANE), F32),
                        pltpu.VMEM((3 * D_GDN // LANE, GROUP, LANE), F32)],
        compiler_params=_params("arbitrary"),
        name="sample_mixer",
    )(*args)


def _out_proj_kernel(ya_lo, ya_hi, og_lo, og_hi, gate_ref, x_lo, x_hi, wa_ref, wb_ref, wo_ref, g_ref, b_ref,
                     *rest, n_lo):
    n_cast = (len(rest) - 1) // 2
    cast_in, o_ref, cast_out = rest[:n_cast], rest[n_cast], rest[n_cast + 1:]
    for src, dst in zip(cast_in, cast_out):
        dst[...] = src[...].astype(dst.dtype)
    lo = pl.program_id(0) < n_lo
    half = o_ref.shape[0] // 2
    halves = (slice(0, half), slice(half, 2 * half))
    pick = lambda a, b, r: jnp.where(lo, a[r, :], b[r, :])
    dot = lambda a, w_ref: jnp.dot(a, w_ref[...], preferred_element_type=F32)
    y_a = [dot(pick(ya_lo, ya_hi, r), wa_ref) for r in halves]
    y_b = [dot(pick(og_lo, og_hi, r), wb_ref) for r in halves]
    merged = [gate_ref[r, :D_MODEL].astype(F32) * a + gate_ref[r, D_MODEL:].astype(F32) * b
              for r, a, b in zip(halves, y_a, y_b)]
    z = [dot(m.astype(BF16), wo_ref) for m in merged]
    for r, zz in zip(halves, z):
        o_ref[r, :] = _layer_norm(ALPHA * pick(x_lo, x_hi, r) + zz, g_ref[...], b_ref[...])


def _out_proj(layer, ya_p, ya_s, og_p, og_s, gates, x_lo, x_hi, hi_block0, wa, wb, wo, g, b, tm, cast=()):
    t = gates.shape[0]
    n, n_lo = t // tm, ya_p.shape[0] // tm
    lo = lambda width: pl.BlockSpec((tm, width), lambda i: (jnp.minimum(i, n_lo - 1), 0))
    hi = lambda width, block0=0: pl.BlockSpec((tm, width), lambda i: (jnp.maximum(i - n_lo, 0) + block0, 0),
                                              pipeline_mode=pl.Buffered(1))
    const = lambda shape: _layer_spec(layer, shape, pipeline_mode=pl.Buffered(1))
    weight = lambda shape: pl.BlockSpec(shape, lambda i: (0, 0), pipeline_mode=pl.Buffered(1))
    n_c = 1 << (n.bit_length() - 1)
    rows = [w.shape[1] // n_c for w in cast]
    assert all(r % (2 * SUBLANE) == 0 and r * n_c == w.shape[1] for r, w in zip(rows, cast))
    cast_block = lambda i: (jnp.minimum(i, n_c - 1), 0)
    outs = pl.pallas_call(
        functools.partial(_out_proj_kernel, n_lo=n_lo),
        grid=(n,),
        in_specs=[
            lo(D_CONV), hi(D_CONV), lo(D_GDN), hi(D_GDN),
            pl.BlockSpec((tm, 2 * D_MODEL), lambda i: (i, 0)),
            lo(D_MODEL), hi(D_MODEL, hi_block0),
            weight((D_CONV, D_MODEL)), weight((D_GDN, D_MODEL)), weight((D_MODEL, D_MODEL)),
            const((1, D_MODEL)), const((1, D_MODEL)),
        ] + [_layer_spec(layer, (r, w.shape[2]), cast_block) for r, w in zip(rows, cast)],
        out_specs=[pl.BlockSpec((tm, D_MODEL), lambda i: (i, 0))]
        + [pl.BlockSpec((r, w.shape[2]), cast_block) for r, w in zip(rows, cast)],
        out_shape=[jax.ShapeDtypeStruct((t, D_MODEL), F32)]
        + [jax.ShapeDtypeStruct(w.shape[1:], BF16) for w in cast],
        compiler_params=_params("arbitrary"),
        name="out_proj",
    )(ya_p, ya_s, og_p, og_s, gates, x_lo, x_hi, wa, wb, wo, g, b, *cast)
    return outs


def _mlp_kernel(x_ref, wu_ref, wd_ref, g_ref, b_ref, o0_ref, o1_ref, xb_ref, acc_ref, *, n_lo):
    i, j = pl.program_id(0), pl.program_id(1)

    @pl.when(j == 0)
    def _():
        xb_ref[...] = x_ref[...].astype(BF16)
        acc_ref[...] = jnp.zeros_like(acc_ref)

    hid = jnp.maximum(jnp.dot(xb_ref[...], wu_ref[...], preferred_element_type=F32), 0.0)
    acc_ref[...] += jnp.dot((hid * hid).astype(BF16), wd_ref[...], preferred_element_type=F32)

    @pl.when(j == pl.num_programs(1) - 1)
    def _():
        y = _layer_norm(ALPHA * x_ref[...] + acc_ref[...], g_ref[...], b_ref[...])
        if n_lo is None:
            o0_ref[...] = y
            o1_ref[...] = y.astype(BF16)
        else:
            @pl.when(i < n_lo)
            def _():
                o0_ref[...] = y

            @pl.when(i >= n_lo)
            def _():
                o1_ref[...] = y


def _mlp(layer, x, wu, wd, g, b, tm, tf, n_prompt=None):
    t = x.shape[0]
    if n_prompt is None:
        n_lo = None
        out_specs = [pl.BlockSpec((tm, D_MODEL), lambda i, j: (i, 0)),
                     pl.BlockSpec((tm, D_MODEL), lambda i, j: (i, 0))]
        out_shape = [jax.ShapeDtypeStruct((t, D_MODEL), F32), jax.ShapeDtypeStruct((t, D_MODEL), BF16)]
    else:
        n_lo = n_prompt // tm
        out_specs = [pl.BlockSpec((tm, D_MODEL), lambda i, j: (jnp.minimum(i, n_lo - 1), 0)),
                     pl.BlockSpec((tm, D_MODEL), lambda i, j: (jnp.maximum(i - n_lo, 0), 0))]
        out_shape = [jax.ShapeDtypeStruct((n_prompt, D_MODEL), F32),
                     jax.ShapeDtypeStruct((t - n_prompt, D_MODEL), F32)]
    return pl.pallas_call(
        functools.partial(_mlp_kernel, n_lo=n_lo),
        grid=(t // tm, D_FF // tf),
        in_specs=[
            pl.BlockSpec((tm, D_MODEL), lambda i, j: (i, 0)),
            pl.BlockSpec((D_MODEL, tf), lambda i, j: (0, j)),
            pl.BlockSpec((tf, D_MODEL), lambda i, j: (j, 0)),
            _layer_spec(layer, (1, D_MODEL)),
            _layer_spec(layer, (1, D_MODEL)),
        ],
        out_specs=out_specs,
        out_shape=out_shape,
        scratch_shapes=[pltpu.VMEM((tm, D_MODEL), BF16), pltpu.VMEM((tm, D_MODEL), F32)],
        compiler_params=_params("arbitrary", "arbitrary"),
        name="mlp",
    )(x, wu, wd, g, b)


def kernel(x_prompt, x_sample, state_conv_a, state_gdn_conv, state_gdn, w_in, conv_a_w, gdn_conv_w, a_log,
           dt_bias, gdn_norm_w, w_a_out, w_b_out, w_o, ln1_g, ln1_b, w_up, w_down, ln2_g, ln2_b):
    batch, seq, _ = x_prompt.shape
    n_seq = x_sample.shape[0]
    n_prompt = batch * seq
    n_sample = n_seq * SAMPLE_LEN
    tm_proj, tm_out, tm_mlp = TM_PROJ, TM_OUT, TM_MLP
    assert x_sample.shape[1] == SAMPLE_LEN and seq % (GROUP * PROMPT_SUB) == 0 and n_seq % SEQ_PER_GROUP == 0
    assert (n_prompt + n_sample) % tm_proj == 0
    assert n_prompt % tm_out == 0 and n_sample % tm_out == 0 and n_prompt % tm_mlp == 0 and n_sample == tm_mlp

    xp = x_prompt.reshape(n_prompt, D_MODEL)
    xs = x_sample.reshape(n_sample, D_MODEL)
    x_bf = _bf16_stream(xp, xs, tm_mlp)
    x_lo, x_hi, hi_block0 = xp, xs, 0

    w_t = jnp.swapaxes(w_in, 1, 2)
    w_logit = jnp.pad(w_t[:, COL_LOGIT:COL_GATE], ((0, 0), (0, LANE - 2 * N_HEADS), (0, 0))).astype(BF16)
    par = jnp.zeros((DEPTH, SUBLANE, LANE), F32)
    par = par.at[:, 0, LOGIT_A:LOGIT_A + N_HEADS].set(a_log.astype(F32))
    par = par.at[:, 1, LOGIT_A:LOGIT_A + N_HEADS].set(dt_bias.astype(F32))
    par = par.at[:, 2, :].set(gdn_norm_w.astype(F32))
    sta, stg = jnp.swapaxes(state_conv_a, 1, 2), jnp.swapaxes(state_gdn_conv, 1, 2)
    row3 = lambda a: a.reshape(DEPTH, 1, D_MODEL)

    ca_p, cg_p, sg_p, ca_s, cg_s = [], [], [], [], []
    sg_s = None
    for l in range(DEPTH):
        proj = _proj(x_bf, w_t, l, 0, N_MIX, F32, False, tm_proj, TN_PROJ, "proj")
        gates, wa, wb, wo = _proj(x_bf, w_t, l, COL_GATE, 2 * D_MODEL, BF16, True, tm_proj, TN_PROJ, "gates",
                                  cast=(w_a_out, w_b_out, w_o))

        ya_p, og_p, utail, gtail, s_p = _prompt_mixer(l, x_bf, w_logit, proj, conv_a_w, gdn_conv_w, par, batch, seq)
        ya_s, og_s, u_s, gstate_s, sg_s = _sample_mixer(
            l, x_bf, w_logit, proj, conv_a_w, gdn_conv_w, par, sta, stg, state_gdn, sg_s, n_prompt // GROUP)

        x1, wu, wd = _out_proj(l, ya_p, ya_s, og_p, og_s, gates, x_lo, x_hi, hi_block0, wa, wb, wo,
                               row3(ln1_g), row3(ln1_b), tm_out, cast=(w_up, w_down))
        if l < DEPTH - 1:
            x, x_bf = _mlp(l, x1, wu, wd, row3(ln2_g), row3(ln2_b), tm_mlp, TF_MLP)
            x_lo, x_hi, hi_block0 = x, x, n_prompt // tm_out
        else:
            y_p, y_s = _mlp(l, x1, wu, wd, row3(ln2_g), row3(ln2_b), tm_mlp, TF_MLP, n_prompt=n_prompt)

        ca_p.append(utail[:, SUBLANE - state_conv_a.shape[2]:])
        cg_p.append(gtail[:, SUBLANE - state_gdn_conv.shape[2]:])
        sg_p.append(s_p)
        ca_s.append(u_s.reshape(n_seq, SAMPLE_LEN, D_CONV)[:, SAMPLE_LEN - state_conv_a.shape[2]:])
        cg_s.append(gstate_s)

    return (y_p.reshape(batch, seq, D_MODEL), y_s.reshape(n_seq, SAMPLE_LEN, D_MODEL),
            jnp.stack(ca_p), jnp.stack(cg_p), jnp.stack(sg_p),
            jnp.stack(ca_s), jnp.swapaxes(jnp.stack(cg_s), 1, 2), sg_s)
```

```python
import functools

import jax
import jax.numpy as jnp
from jax import lax
from jax.experimental import pallas as pl
from jax.experimental.pallas import tpu as pltpu

F32 = jnp.float32
BF16 = jnp.bfloat16

D_MODEL = 2048
DEPTH = 2
D_CONV = D_MODEL // 2
N_HEADS = 8
HEAD_DIM = 128
D_GDN = N_HEADS * HEAD_DIM
D_FF = 4 * D_MODEL
ALPHA = (2 * DEPTH) ** 0.25
LN_EPS = 1e-5
NORM_EPS = 1e-6

COL_GB, COL_GC, COL_H = 0, D_CONV, 2 * D_CONV
COL_QKV = 3 * D_CONV
COL_Z = COL_QKV + 3 * D_GDN
N_MIX = COL_Z + D_GDN
COL_LOGIT = N_MIX
COL_GATE = COL_LOGIT + 2 * N_HEADS
LANE = 128
SUBLANE = 8
LOGIT_B, LOGIT_A = 0, N_HEADS

GROUP = 64
PROMPT_SUB = 4
SAMPLE_LEN = 4
SEQ_PER_GROUP = GROUP // SAMPLE_LEN

VMEM_LIMIT = 56 * 1024 * 1024

TM_PROJ, TN_PROJ = 1088, 1024
TM_OUT = 256
TM_MLP, TF_MLP = 512, 1024


def _dot(a, b):
    return jnp.dot(a.astype(BF16), b.astype(BF16), preferred_element_type=F32)


def _dot_nt(a, b):
    return lax.dot_general(a.astype(BF16), b.astype(BF16), (((1,), (1,)), ((), ())),
                           preferred_element_type=F32)


def _dot_exact_lhs(m01, x):
    x1 = x.astype(BF16)
    r1 = x - x1.astype(F32)
    x2 = r1.astype(BF16)
    x3 = (r1 - x2.astype(F32)).astype(BF16)
    d = lambda y: jnp.dot(m01, y, preferred_element_type=F32)
    return d(x1) + (d(x2) + d(x3))


def _same_block(i, j, size):
    shift = size.bit_length() - 1
    assert size == 1 << shift
    return (i >> shift) == (j >> shift)


def _sigmoid(x):
    return 0.5 * jnp.tanh(0.5 * x) + 0.5


def _softplus(x):
    return jnp.maximum(x, 0.0) + jnp.log1p(jnp.exp(-jnp.abs(x)))


def _layer_norm(y, g, b):
    mu = jnp.mean(y, axis=-1, keepdims=True)
    yc = y - mu
    var = jnp.mean(yc * yc, axis=-1, keepdims=True)
    return yc * lax.rsqrt(var + LN_EPS) * g + b


def _layer_spec(layer, shape, index_map=None, **kw):
    if index_map is None:
        index_map = lambda *g: (0,) * len(shape)
    return pl.BlockSpec((None,) + tuple(shape), lambda *g: (layer,) + tuple(index_map(*g)), **kw)


def _params(*sem):
    return pltpu.CompilerParams(dimension_semantics=sem, vmem_limit_bytes=VMEM_LIMIT)


def _stream_kernel(lo_ref, hi_ref, o_ref, *, n_lo):
    o_ref[...] = jnp.where(pl.program_id(0) < n_lo, lo_ref[...], hi_ref[...]).astype(o_ref.dtype)


def _bf16_stream(x_lo, x_hi, tm):
    n_lo, n_hi = x_lo.shape[0] // tm, x_hi.shape[0] // tm
    return pl.pallas_call(
        functools.partial(_stream_kernel, n_lo=n_lo),
        grid=(n_lo + n_hi,),
        in_specs=[pl.BlockSpec((tm, D_MODEL), lambda i: (jnp.minimum(i, n_lo - 1), 0)),
                  pl.BlockSpec((tm, D_MODEL), lambda i: (jnp.maximum(i - n_lo, 0), 0))],
        out_specs=pl.BlockSpec((tm, D_MODEL), lambda i: (i, 0)),
        out_shape=jax.ShapeDtypeStruct((x_lo.shape[0] + x_hi.shape[0], D_MODEL), BF16),
        compiler_params=_params("arbitrary"),
        name="bf16_stream",
    )(x_lo, x_hi)


def _proj_kernel(x_ref, wt_hbm, *rest, gate, n_cast, layer, col0):
    cast_in, o_ref, cast_out = rest[:n_cast], rest[n_cast], rest[n_cast + 1:2 * n_cast + 1]
    wb_ref, wf_ref, sem = rest[-3:]
    j, n_j = pl.program_id(0), pl.num_programs(0)
    tn = wb_ref.shape[0]

    def weight_copy(block, slot):
        rows = pl.ds(pl.multiple_of(col0 + block * tn, 2 * SUBLANE), tn)
        return pltpu.make_async_copy(wt_hbm.at[layer, rows, :], wf_ref.at[slot], sem.at[slot])

    @pl.when(pl.program_id(1) == 0)
    def _():
        slot = j % 2

        @pl.when(j == 0)
        def _():
            weight_copy(0, 0).start()

        weight_copy(j, slot).wait()
        wb_ref[...] = wf_ref[slot].astype(BF16)

        @pl.when(j + 1 < n_j)
        def _():
            weight_copy(j + 1, 1 - slot).start()

    y = lax.dot_general(x_ref[...], wb_ref[...], (((1,), (1,)), ((), ())), preferred_element_type=F32)
    if gate:
        y = _sigmoid(y)
    o_ref[...] = y.astype(o_ref.dtype)
    for src, dst in zip(cast_in, cast_out):
        dst[...] = src[...].astype(dst.dtype)


def _proj(x_bf, w_t, layer, col0, n_cols, out_dtype, gate, tm, tn, name, cast=()):
    t = x_bf.shape[0]
    n_j, n_i = n_cols // tn, t // tm
    step = lambda j, i: j * n_i + i
    rows = [w.shape[1] // (n_j * n_i) for w in cast]
    assert all(r % 16 == 0 and r * n_j * n_i == w.shape[1] for r, w in zip(rows, cast))
    outs = pl.pallas_call(
        functools.partial(_proj_kernel, gate=gate, n_cast=len(cast), layer=layer, col0=col0),
        grid=(n_j, n_i),
        in_specs=[pl.BlockSpec((tm, D_MODEL), lambda j, i: (i, 0)),
                  pl.BlockSpec(memory_space=pl.ANY)]
        + [_layer_spec(layer, (r, w.shape[2]), lambda j, i: (step(j, i), 0)) for r, w in zip(rows, cast)],
        out_specs=[pl.BlockSpec((tm, tn), lambda j, i: (i, j))]
        + [pl.BlockSpec((r, w.shape[2]), lambda j, i: (step(j, i), 0)) for r, w in zip(rows, cast)],
        out_shape=[jax.ShapeDtypeStruct((t, n_cols), out_dtype)]
        + [jax.ShapeDtypeStruct(w.shape[1:], BF16) for w in cast],
        scratch_shapes=[pltpu.VMEM((tn, D_MODEL), BF16), pltpu.VMEM((2, tn, D_MODEL), F32),
                        pltpu.SemaphoreType.DMA((2,))],
        compiler_params=_params("arbitrary", "arbitrary"),
        name=name,
    )(x_bf, w_t, *cast)
    return outs if cast else outs[0]


def _causal_conv(u, w_ref, prev_fn):
    k = w_ref.shape[0]
    out = u * w_ref[k - 1:k, :]
    for j in range(1, k):
        out = out + prev_fn(j) * w_ref[k - 1 - j:k - j, :]
    return out


CONV_PHASES = 4


def _prompt_conv(u, w_ref, buf_ref, out_ref, act=None, scale=None):
    k = w_ref.shape[0]
    n, c = u.shape
    slabs = c // LANE
    for s in range(slabs):
        buf_ref[s, SUBLANE:, :] = u[:, s * LANE:(s + 1) * LANE]
    for s in range(slabs):
        w = [w_ref[j:j + 1, s * LANE:(s + 1) * LANE] for j in range(k)]
        if scale is not None:
            w = [x * scale for x in w]
        for b in range(CONV_PHASES):
            acc = None
            for j in range(k):
                tap = buf_ref[s, pl.ds(SUBLANE - (k - 1) + j + b, n // CONV_PHASES, stride=CONV_PHASES), :]
                acc = tap * w[j] if acc is None else acc + tap * w[j]
            out_ref[s, pl.ds(b, n // CONV_PHASES, stride=CONV_PHASES), :] = acc if act is None else act(acc)
    for s in range(slabs):
        buf_ref[s, 0:SUBLANE, :] = buf_ref[s, n:n + SUBLANE, :]


def _slabs_value(ref):
    return jnp.concatenate([ref[s] for s in range(ref.shape[0])], axis=1)


def _sample_prev(u, st_ref, fill_ref):
    t = lax.broadcasted_iota(jnp.int32, u.shape, 0) & (SAMPLE_LEN - 1)
    k1 = st_ref.shape[0]
    slabs = u.shape[1] // LANE

    def prev(j):
        for tt in range(j):
            for s in range(slabs):
                fill_ref[j - 1, s, pl.ds(tt, SEQ_PER_GROUP, stride=SAMPLE_LEN), :] = (
                    st_ref[k1 - j + tt, :, s * LANE:(s + 1) * LANE])
        fill = jnp.concatenate([fill_ref[j - 1, s] for s in range(slabs)], axis=1)
        return jnp.where(t >= j, pltpu.roll(u, j, 0), fill)
    return prev


def _pair_iotas():
    row = lax.broadcasted_iota(jnp.int32, (GROUP, 2 * GROUP), 0)
    lane = lax.broadcasted_iota(jnp.int32, (GROUP, 2 * GROUP), 1)
    return row, lane & (GROUP - 1), lane >= GROUP


def _block_diag(y0, y1):
    z = jnp.zeros_like(y0)
    return jnp.concatenate([jnp.concatenate([y0, z], axis=1), jnp.concatenate([z, y1], axis=1)], axis=0)


def _pair_block_diag(y):
    _, _, second = _pair_iotas()
    return jnp.concatenate([jnp.where(second, 0.0, y), jnp.where(second, y, 0.0)], axis=0)


def _tri_inverse_pairs(ms, seq_len):
    row, col, _ = _pair_iotas()
    eye = (row == col).astype(F32)
    base = min(8, seq_len)
    mm = lambda x, y: _dot(x, _pair_block_diag(y))
    same = lambda b: _same_block(row, col, b)
    d = [jnp.where(same(base), m, 0.0) for m in ms]
    d2 = [mm(x, x) for x in d]
    xs = [eye - x for x in d]
    if base > 4:
        d4 = [mm(y, y) for y in d2]
    inv = [x + mm(x, y) for x, y in zip(xs, d2)]
    if base > 4:
        inv = [i + mm(i, y) for i, y in zip(inv, d4)]
    b = base * 2
    while b <= seq_len:
        off_diag = same(b) & jnp.logical_not(same(b // 2))
        t = [mm(i, jnp.where(off_diag, m, 0.0)) for i, m in zip(inv, ms)]
        inv = [i - mm(x, i) for i, x in zip(inv, t)]
        b *= 2
    return inv


def _gdn_group(q, k, v, z, logit, a_row, dt_row, nw_row, seq_len, state, og_ref):
    row = lax.broadcasted_iota(jnp.int32, (GROUP, GROUP), 0)
    col = lax.broadcasted_iota(jnp.int32, (GROUP, GROUP), 1)
    same_seq = _same_block(row, col, seq_len)
    lmat = jnp.concatenate([(same_seq & (row >= col)).astype(F32), same_seq.astype(F32)], axis=0).astype(BF16)
    prow, pcol, second = _pair_iotas()
    psame = _same_block(prow, pcol, seq_len)
    causal = psame & (prow >= pcol)
    strict = psame & (prow > pcol)

    subs = range(q.shape[0] // GROUP)
    heads = range(N_HEADS)
    pairs = range(0, N_HEADS, 2)
    rows = lambda x, c: x[c * GROUP:(c + 1) * GROUP]
    hs = lambda x, c, h: x[c * GROUP:(c + 1) * GROUP, h * HEAD_DIM:(h + 1) * HEAD_DIM]
    lcol = lambda x, h: x[:, LOGIT_A + h:LOGIT_A + h + 1]
    cat = jnp.concatenate

    beta = _sigmoid(logit)
    g = -jnp.exp(a_row) * _softplus(logit + dt_row)
    units = [(c, h) for c in subs for h in pairs]
    gc2 = {c: _dot_exact_lhs(lmat, rows(g, c)) for c in subs}
    gcum = {c: x[:GROUP] for c, x in gc2.items()}
    gtot = {c: x[GROUP:] for c, x in gc2.items()}
    gcum_t2 = {c: cat([x, x], axis=0).T for c, x in gcum.items()}
    e_cum = {c: jnp.exp(x) for c, x in gcum.items()}
    e_rem = {c: jnp.exp(gtot[c] - gcum[c]) for c in subs}
    e_tot = {c: jnp.exp(t) for c, t in gtot.items()}

    qn, kn, bcol, kb = {}, {}, {}, {}
    for c in subs:
        for h in heads:
            qh, kh = hs(q, c, h), hs(k, c, h)
            qn[c, h] = qh * (lax.rsqrt(jnp.sum(qh * qh, axis=-1, keepdims=True) + NORM_EPS) * (HEAD_DIM ** -0.5))
            kn[c, h] = kh * lax.rsqrt(jnp.sum(kh * kh, axis=-1, keepdims=True) + NORM_EPS)
            bcol[c, h] = rows(beta, c)[:, LOGIT_B + h:LOGIT_B + h + 1]
            kb[c, h] = kn[c, h] * bcol[c, h]

    a = {(c, h): _dot_nt(cat([cat([kb[c, h], kb[c, h + 1]], axis=1), cat([qn[c, h], qn[c, h + 1]], axis=1)], axis=0),
                         _block_diag(kn[c, h], kn[c, h + 1])) for c, h in units}
    ms, attn = {}, {}
    for c, h in units:
        gcol = jnp.where(second, lcol(gcum[c], h + 1), lcol(gcum[c], h))
        grow = jnp.where(second[0:1], gcum_t2[c][LOGIT_A + h + 1:LOGIT_A + h + 2, :],
                         gcum_t2[c][LOGIT_A + h:LOGIT_A + h + 1, :])
        dec = jnp.exp(jnp.where(causal, gcol - grow, 0.0))
        ms[c, h] = jnp.where(strict, a[c, h][:GROUP] * dec, 0.0)
        attn[c, h] = jnp.where(causal, a[c, h][GROUP:] * dec, 0.0)
    t_inv = dict(zip(units, _tri_inverse_pairs([ms[un] for un in units], seq_len)))

    u, w, qdec, kdec = {}, {}, {}, {}
    for c, h in units:
        rhs = [cat([hs(v, c, hh) * bcol[c, hh], kb[c, hh] * lcol(e_cum[c], hh)], axis=1) for hh in (h, h + 1)]
        uw = _dot(t_inv[c, h], _block_diag(*rhs))
        for i, hh in enumerate((h, h + 1)):
            u[c, hh] = uw[:, 2 * i * HEAD_DIM:(2 * i + 1) * HEAD_DIM]
            w[c, hh] = uw[:, (2 * i + 1) * HEAD_DIM:(2 * i + 2) * HEAD_DIM]
            qdec[c, hh] = qn[c, hh] * lcol(e_cum[c], hh)
            kdec[c, hh] = kn[c, hh] * lcol(e_rem[c], hh)

    for c in subs:
        ws = [state.read(h, cat([w[c, h], qdec[c, h]], axis=0)) for h in heads]
        v_new = [u[c, h] - ws[h][0] for h in heads]
        vn_bd = {h: _block_diag(v_new[h], v_new[h + 1]) for h in pairs}
        o_pair = {h: cat([ws[h][1], ws[h + 1][1]], axis=1) + _dot(attn[c, h], vn_bd[h]) for h in pairs}
        for h in pairs:
            state.update(h, cat([kdec[c, h], kdec[c, h + 1]], axis=0).T, vn_bd[h],
                         lcol(e_tot[c], h), lcol(e_tot[c], h + 1))
        for h in heads:
            o = o_pair[h - h % 2][:, (h % 2) * HEAD_DIM:(h % 2 + 1) * HEAD_DIM]
            o = o * lax.rsqrt(jnp.mean(o * o, axis=-1, keepdims=True) + NORM_EPS) * nw_row
            zh = hs(z, c, h)
            og_ref[c * GROUP:(c + 1) * GROUP, h * HEAD_DIM:(h + 1) * HEAD_DIM] = (
                o * _silu(zh)).astype(og_ref.dtype)
    state.finish()


def _silu_of_half(h):
    return h * (jnp.tanh(h) + 1.0)


def _silu(x):
    return _silu_of_half(0.5 * x)


def _mixer_common(x_ref, wl_ref, abc_ref, z_ref, par_ref, conv_a, qkv_act, seq_len, state, ya_ref, og_ref):
    gb = abc_ref[:, COL_GB:COL_GB + D_CONV]
    gc = abc_ref[:, COL_GC:COL_GC + D_CONV]
    hv = abc_ref[:, COL_H:COL_H + D_CONV]
    u = gc * hv
    ya_ref[...] = (gb * conv_a(u)).astype(ya_ref.dtype)

    act = qkv_act()
    q, k, v = act[:, :D_GDN], act[:, D_GDN:2 * D_GDN], act[:, 2 * D_GDN:]
    logit = lax.dot_general(x_ref[...], wl_ref[...], (((1,), (1,)), ((), ())), preferred_element_type=F32)
    _gdn_group(q, k, v, z_ref[...], logit, par_ref[0:1, :], par_ref[1:2, :], par_ref[2:3, :],
               seq_len, state, og_ref)
    return u


def _prompt_mixer_kernel(x_ref, wl_ref, abc_ref, qkv_ref, z_ref, caw_ref, gcw_ref, par_ref, cast_ref,
                         cast_out_ref, ya_ref, og_ref, utail_ref, gtail_ref, s_ref,
                         ubuf_ref, uout_ref, gbuf_ref, gout_ref):
    cast_out_ref[...] = cast_ref[...].astype(cast_out_ref.dtype)

    @pl.when(pl.program_id(1) == 0)
    def _():
        ubuf_ref[:, 0:SUBLANE, :] = jnp.zeros((D_CONV // LANE, SUBLANE, LANE), F32)
        gbuf_ref[:, 0:SUBLANE, :] = jnp.zeros((3 * D_GDN // LANE, SUBLANE, LANE), F32)
        s_ref[...] = jnp.zeros_like(s_ref)

    cur = {}

    class State:
        @staticmethod
        def read(h, lhs):
            if h not in cur:
                cur[h] = s_ref[0, h]
            ws = _dot(lhs, cur[h])
            return ws[:GROUP], ws[GROUP:]

        @staticmethod
        def update(h, kd_t, vn_bd, e0, e1):
            kv = _dot(kd_t, vn_bd)
            cur[h] = cur[h] * e0[0:1, :] + kv[:, :HEAD_DIM]
            cur[h + 1] = cur[h + 1] * e1[0:1, :] + kv[:, HEAD_DIM:]

        @staticmethod
        def finish():
            for h, s in cur.items():
                s_ref[0, h] = s

    def conv_a(u):
        _prompt_conv(u, caw_ref, ubuf_ref, uout_ref)
        return _slabs_value(uout_ref)

    def qkv_act():
        _prompt_conv(qkv_ref[...], gcw_ref, gbuf_ref, gout_ref, _silu_of_half, scale=0.5)
        return _slabs_value(gout_ref)

    u = _mixer_common(x_ref, wl_ref, abc_ref, z_ref, par_ref, conv_a, qkv_act, GROUP, State, ya_ref, og_ref)
    n = u.shape[0]
    utail_ref[0] = u[n - SUBLANE:]
    gtail_ref[0] = qkv_ref[n - SUBLANE:, :]


def _sample_mixer_kernel(x_ref, wl_ref, abc_ref, qkv_ref, z_ref, caw_ref, gcw_ref, par_ref,
                         sta_ref, stg_ref, s0_ref, *rest):
    ya_ref, og_ref, u_ref, gstate_ref, s_ref, fill_a_ref, fill_g_ref, raw_slab_ref = rest[-8:]

    @pl.when(pl.program_id(0) == 0)
    def _():
        fill_a_ref[...] = jnp.zeros_like(fill_a_ref)
        fill_g_ref[...] = jnp.zeros_like(fill_g_ref)

    first = lax.broadcasted_iota(jnp.int32, (SUBLANE, HEAD_DIM), 0) < SAMPLE_LEN
    row_bd = lax.broadcasted_iota(jnp.int32, (2 * GROUP, 2 * HEAD_DIM), 0) & (GROUP - 1)

    class State:
        @staticmethod
        def read(h, lhs):
            w, qdec = lhs[:GROUP], lhs[GROUP:]
            ws_w, ws_q = [], []
            for p in range(GROUP // SUBLANE):
                r = slice(SUBLANE * p, SUBLANE * (p + 1))
                slab = jnp.concatenate([w[r], qdec[r]], axis=0)
                r0 = _dot(slab, s0_ref[2 * p, h])
                r1 = _dot(slab, s0_ref[2 * p + 1, h])
                ws_w.append(jnp.where(first, r0[:SUBLANE], r1[:SUBLANE]))
                ws_q.append(jnp.where(first, r0[SUBLANE:], r1[SUBLANE:]))
            return jnp.concatenate(ws_w, axis=0), jnp.concatenate(ws_q, axis=0)

        @staticmethod
        def update(h, kd_t, vn_bd, e0, e1):
            for s in range(SEQ_PER_GROUP):
                in_seq = _same_block(row_bd, SAMPLE_LEN * s, SAMPLE_LEN)
                kv = _dot(kd_t, jnp.where(in_seq, vn_bd, 0.0))
                r = SAMPLE_LEN * s
                s_ref[s, h] = s0_ref[s, h] * e0[r:r + 1, :] + kv[:, :HEAD_DIM]
                s_ref[s, h + 1] = s0_ref[s, h + 1] * e1[r:r + 1, :] + kv[:, HEAD_DIM:]

        @staticmethod
        def finish():
            pass

    def qkv_act():
        raw = qkv_ref[...]
        for s in range(raw_slab_ref.shape[0]):
            raw_slab_ref[s] = raw[:, s * LANE:(s + 1) * LANE]
        for t in range(1, SAMPLE_LEN):
            gstate_ref[t - 1] = jnp.concatenate(
                [raw_slab_ref[s, pl.ds(t, SEQ_PER_GROUP, stride=SAMPLE_LEN), :] for s in range(raw_slab_ref.shape[0])],
                axis=1)
        return _silu(_causal_conv(raw, gcw_ref, _sample_prev(raw, stg_ref, fill_g_ref)))

    u_ref[...] = _mixer_common(
        x_ref, wl_ref, abc_ref, z_ref, par_ref,
        lambda u: _causal_conv(u, caw_ref, _sample_prev(u, sta_ref, fill_a_ref)), qkv_act,
        SAMPLE_LEN, State, ya_ref, og_ref)


def _mixer_in_specs(layer, row_block, n_rows=GROUP):
    rows = lambda width, col_block: pl.BlockSpec((n_rows, width), lambda *i: (row_block(*i), col_block))
    return [
        rows(D_MODEL, 0),
        _layer_spec(layer, (LANE, D_MODEL)),
        rows(3 * D_CONV, 0),
        rows(3 * D_GDN, COL_QKV // (3 * D_GDN)),
        rows(D_GDN, COL_Z // D_GDN),
        _layer_spec(layer, (3, D_CONV)),
        _layer_spec(layer, (4, 3 * D_GDN)),
        _layer_spec(layer, (SUBLANE, LANE)),
    ]


def _prompt_mixer(layer, x_bf, w_logit, proj, caw, gcw, par, batch, seq, cast):
    step = GROUP * PROMPT_SUB
    chunks = seq // step
    rb = lambda b, c: b * chunks + c
    n = batch * seq
    cast_rows = cast.shape[1] // (batch * chunks)
    assert cast_rows % (2 * SUBLANE) == 0 and cast_rows * batch * chunks == cast.shape[1]
    return pl.pallas_call(
        _prompt_mixer_kernel,
        grid=(batch, chunks),
        in_specs=_mixer_in_specs(layer, rb, step)
        + [_layer_spec(layer, (cast_rows, cast.shape[2]), lambda b, c: (rb(b, c), 0))],
        out_specs=[
            pl.BlockSpec((cast_rows, cast.shape[2]), lambda b, c: (rb(b, c), 0)),
            pl.BlockSpec((step, D_CONV), lambda b, c: (rb(b, c), 0)),
            pl.BlockSpec((step, D_GDN), lambda b, c: (rb(b, c), 0)),
            pl.BlockSpec((1, SUBLANE, D_CONV), lambda b, c: (b, 0, 0)),
            pl.BlockSpec((1, SUBLANE, 3 * D_GDN), lambda b, c: (b, 0, 0)),
            pl.BlockSpec((1, N_HEADS, HEAD_DIM, HEAD_DIM), lambda b, c: (b, 0, 0, 0)),
        ],
        out_shape=[
            jax.ShapeDtypeStruct(cast.shape[1:], BF16),
            jax.ShapeDtypeStruct((n, D_CONV), BF16),
            jax.ShapeDtypeStruct((n, D_GDN), BF16),
            jax.ShapeDtypeStruct((batch, SUBLANE, D_CONV), F32),
            jax.ShapeDtypeStruct((batch, SUBLANE, 3 * D_GDN), F32),
            jax.ShapeDtypeStruct((batch, N_HEADS, HEAD_DIM, HEAD_DIM), F32),
        ],
        scratch_shapes=[pltpu.VMEM((D_CONV // LANE, step + SUBLANE, LANE), F32),
                        pltpu.VMEM((D_CONV // LANE, step, LANE), F32),
                        pltpu.VMEM((3 * D_GDN // LANE, step + SUBLANE, LANE), F32),
                        pltpu.VMEM((3 * D_GDN // LANE, step, LANE), F32)],
        compiler_params=_params("arbitrary", "arbitrary"),
        name="prompt_mixer",
    )(x_bf, w_logit, proj, proj, proj, caw, gcw, par, cast)


def _sample_mixer(layer, x_bf, w_logit, proj, caw, gcw, par, sta, stg, s0, s_prev, first_block):
    n_seq = s0.shape[1]
    rb = lambda g: first_block + g
    state_block = (SEQ_PER_GROUP, N_HEADS, HEAD_DIM, HEAD_DIM)
    in_specs = _mixer_in_specs(layer, rb) + [
        _layer_spec(layer, (sta.shape[1], SEQ_PER_GROUP, D_CONV), lambda g: (0, g, 0)),
        _layer_spec(layer, (stg.shape[1], SEQ_PER_GROUP, 3 * D_GDN), lambda g: (0, g, 0)),
        _layer_spec(layer, state_block, lambda g: (g, 0, 0, 0)),
    ]
    args = [x_bf, w_logit, proj, proj, proj, caw, gcw, par, sta, stg, s0]
    aliases = {}
    if s_prev is not None:
        in_specs.append(pl.BlockSpec(memory_space=pl.ANY))
        aliases = {len(args): 4}
        args.append(s_prev)
    return pl.pallas_call(
        _sample_mixer_kernel,
        grid=(n_seq // SEQ_PER_GROUP,),
        in_specs=in_specs,
        out_specs=[
            pl.BlockSpec((GROUP, D_CONV), lambda g: (g, 0)),
            pl.BlockSpec((GROUP, D_GDN), lambda g: (g, 0)),
            pl.BlockSpec((GROUP, D_CONV), lambda g: (g, 0)),
            pl.BlockSpec((SAMPLE_LEN - 1, SEQ_PER_GROUP, 3 * D_GDN), lambda g: (0, g, 0)),
            _layer_spec(layer, state_block, lambda g: (g, 0, 0, 0)),
        ],
        out_shape=[
            jax.ShapeDtypeStruct((n_seq * SAMPLE_LEN, D_CONV), BF16),
            jax.ShapeDtypeStruct((n_seq * SAMPLE_LEN, D_GDN), BF16),
            jax.ShapeDtypeStruct((n_seq * SAMPLE_LEN, D_CONV), F32),
            jax.ShapeDtypeStruct((SAMPLE_LEN - 1, n_seq, 3 * D_GDN), F32),
            jax.ShapeDtypeStruct(s0.shape, F32),
        ],
        input_output_aliases=aliases,
        scratch_shapes=[pltpu.VMEM((sta.shape[1], D_CONV // LANE, GROUP, LANE), F32),
                        pltpu.VMEM((stg.shape[1], 3 * D_GDN // LANE, GROUP, LANE), F32),
                        pltpu.VMEM((3 * D_GDN // LANE, GROUP, LANE), F32)],
        compiler_params=_params("arbitrary"),
        name="sample_mixer",
    )(*args)


def _out_proj_kernel(ya_lo, ya_hi, og_lo, og_hi, gate_ref, x_lo, x_hi, wa_ref, wb_ref, wo_ref, g_ref, b_ref,
                     *rest, n_lo):
    n_cast = (len(rest) - 1) // 2
    cast_in, o_ref, cast_out = rest[:n_cast], rest[n_cast], rest[n_cast + 1:]
    for src, dst in zip(cast_in, cast_out):
        dst[...] = src[...].astype(dst.dtype)
    lo = pl.program_id(0) < n_lo
    half = o_ref.shape[0] // 2
    halves = (slice(0, half), slice(half, 2 * half))
    pick = lambda a, b, r: jnp.where(lo, a[r, :], b[r, :])
    dot = lambda a, w_ref: jnp.dot(a, w_ref[...], preferred_element_type=F32)
    y_a = [dot(pick(ya_lo, ya_hi, r), wa_ref) for r in halves]
    y_b = [dot(pick(og_lo, og_hi, r), wb_ref) for r in halves]
    merged = [gate_ref[r, :D_MODEL].astype(F32) * a + gate_ref[r, D_MODEL:].astype(F32) * b
              for r, a, b in zip(halves, y_a, y_b)]
    z = [dot(m.astype(BF16), wo_ref) for m in merged]
    for r, zz in zip(halves, z):
        o_ref[r, :] = _layer_norm(ALPHA * pick(x_lo, x_hi, r) + zz, g_ref[...], b_ref[...])


def _out_proj(layer, ya_p, ya_s, og_p, og_s, gates, x_lo, x_hi, hi_block0, wa, wb, wo, g, b, tm, cast=()):
    t = gates.shape[0]
    n, n_lo = t // tm, ya_p.shape[0] // tm
    lo = lambda width: pl.BlockSpec((tm, width), lambda i: (jnp.minimum(i, n_lo - 1), 0))
    hi = lambda width, block0=0: pl.BlockSpec((tm, width), lambda i: (jnp.maximum(i - n_lo, 0) + block0, 0),
                                              pipeline_mode=pl.Buffered(1))
    const = lambda shape: _layer_spec(layer, shape, pipeline_mode=pl.Buffered(1))
    weight = lambda shape: pl.BlockSpec(shape, lambda i: (0, 0), pipeline_mode=pl.Buffered(1))
    n_c = 1 << (n.bit_length() - 1)
    rows = [w.shape[1] // n_c for w in cast]
    assert all(r % (2 * SUBLANE) == 0 and r * n_c == w.shape[1] for r, w in zip(rows, cast))
    cast_block = lambda i: (jnp.minimum(i, n_c - 1), 0)
    outs = pl.pallas_call(
        functools.partial(_out_proj_kernel, n_lo=n_lo),
        grid=(n,),
        in_specs=[
            lo(D_CONV), hi(D_CONV), lo(D_GDN), hi(D_GDN),
            pl.BlockSpec((tm, 2 * D_MODEL), lambda i: (i, 0)),
            lo(D_MODEL), hi(D_MODEL, hi_block0),
            weight((D_CONV, D_MODEL)), weight((D_GDN, D_MODEL)), weight((D_MODEL, D_MODEL)),
            const((1, D_MODEL)), const((1, D_MODEL)),
        ] + [_layer_spec(layer, (r, w.shape[2]), cast_block) for r, w in zip(rows, cast)],
        out_specs=[pl.BlockSpec((tm, D_MODEL), lambda i: (i, 0))]
        + [pl.BlockSpec((r, w.shape[2]), cast_block) for r, w in zip(rows, cast)],
        out_shape=[jax.ShapeDtypeStruct((t, D_MODEL), F32)]
        + [jax.ShapeDtypeStruct(w.shape[1:], BF16) for w in cast],
        compiler_params=_params("arbitrary"),
        name="out_proj",
    )(ya_p, ya_s, og_p, og_s, gates, x_lo, x_hi, wa, wb, wo, g, b, *cast)
    return outs


def _mlp_kernel(x_ref, wu_ref, wd_ref, g_ref, b_ref, o0_ref, o1_ref, xb_ref, acc_ref, *, n_lo):
    i, j = pl.program_id(0), pl.program_id(1)

    @pl.when(j == 0)
    def _():
        xb_ref[...] = x_ref[...].astype(BF16)
        acc_ref[...] = jnp.zeros_like(acc_ref)

    hid = jnp.maximum(jnp.dot(xb_ref[...], wu_ref[...], preferred_element_type=F32), 0.0)
    acc_ref[...] += jnp.dot((hid * hid).astype(BF16), wd_ref[...], preferred_element_type=F32)

    @pl.when(j == pl.num_programs(1) - 1)
    def _():
        y = _layer_norm(ALPHA * x_ref[...] + acc_ref[...], g_ref[...], b_ref[...])
        if n_lo is None:
            o0_ref[...] = y
            o1_ref[...] = y.astype(BF16)
        else:
            @pl.when(i < n_lo)
            def _():
                o0_ref[...] = y

            @pl.when(i >= n_lo)
            def _():
                o1_ref[...] = y


def _mlp(layer, x, wu, wd, g, b, tm, tf, n_prompt=None):
    t = x.shape[0]
    if n_prompt is None:
        n_lo = None
        out_specs = [pl.BlockSpec((tm, D_MODEL), lambda i, j: (i, 0)),
                     pl.BlockSpec((tm, D_MODEL), lambda i, j: (i, 0))]
        out_shape = [jax.ShapeDtypeStruct((t, D_MODEL), F32), jax.ShapeDtypeStruct((t, D_MODEL), BF16)]
    else:
        n_lo = n_prompt // tm
        out_specs = [pl.BlockSpec((tm, D_MODEL), lambda i, j: (jnp.minimum(i, n_lo - 1), 0)),
                     pl.BlockSpec((tm, D_MODEL), lambda i, j: (jnp.maximum(i - n_lo, 0), 0))]
        out_shape = [jax.ShapeDtypeStruct((n_prompt, D_MODEL), F32),
                     jax.ShapeDtypeStruct((t - n_prompt, D_MODEL), F32)]
    return pl.pallas_call(
        functools.partial(_mlp_kernel, n_lo=n_lo),
        grid=(t // tm, D_FF // tf),
        in_specs=[
            pl.BlockSpec((tm, D_MODEL), lambda i, j: (i, 0)),
            pl.BlockSpec((D_MODEL, tf), lambda i, j: (0, j)),
            pl.BlockSpec((tf, D_MODEL), lambda i, j: (j, 0)),
            _layer_spec(layer, (1, D_MODEL)),
            _layer_spec(layer, (1, D_MODEL)),
        ],
        out_specs=out_specs,
        out_shape=out_shape,
        scratch_shapes=[pltpu.VMEM((tm, D_MODEL), BF16), pltpu.VMEM((tm, D_MODEL), F32)],
        compiler_params=_params("arbitrary", "arbitrary"),
        name="mlp",
    )(x, wu, wd, g, b)


def kernel(x_prompt, x_sample, state_conv_a, state_gdn_conv, state_gdn, w_in, conv_a_w, gdn_conv_w, a_log,
           dt_bias, gdn_norm_w, w_a_out, w_b_out, w_o, ln1_g, ln1_b, w_up, w_down, ln2_g, ln2_b):
    batch, seq, _ = x_prompt.shape
    n_seq = x_sample.shape[0]
    n_prompt = batch * seq
    n_sample = n_seq * SAMPLE_LEN
    tm_proj, tm_out, tm_mlp = TM_PROJ, TM_OUT, TM_MLP
    assert x_sample.shape[1] == SAMPLE_LEN and seq % (GROUP * PROMPT_SUB) == 0 and n_seq % SEQ_PER_GROUP == 0
    assert (n_prompt + n_sample) % tm_proj == 0
    assert n_prompt % tm_out == 0 and n_sample % tm_out == 0 and n_prompt % tm_mlp == 0 and n_sample == tm_mlp

    xp = x_prompt.reshape(n_prompt, D_MODEL)
    xs = x_sample.reshape(n_sample, D_MODEL)
    x_bf = _bf16_stream(xp, xs, tm_mlp)
    x_lo, x_hi, hi_block0 = xp, xs, 0

    w_t = jnp.swapaxes(w_in, 1, 2)
    w_logit = jnp.pad(w_t[:, COL_LOGIT:COL_GATE], ((0, 0), (0, LANE - 2 * N_HEADS), (0, 0))).astype(BF16)
    par = jnp.zeros((DEPTH, SUBLANE, LANE), F32)
    par = par.at[:, 0, LOGIT_A:LOGIT_A + N_HEADS].set(a_log.astype(F32))
    par = par.at[:, 1, LOGIT_A:LOGIT_A + N_HEADS].set(dt_bias.astype(F32))
    par = par.at[:, 2, :].set(gdn_norm_w.astype(F32))
    sta, stg = jnp.swapaxes(state_conv_a, 1, 2), jnp.swapaxes(state_gdn_conv, 1, 2)
    row3 = lambda a: a.reshape(DEPTH, 1, D_MODEL)

    ca_p, cg_p, sg_p, ca_s, cg_s = [], [], [], [], []
    sg_s = None
    for l in range(DEPTH):
        proj = _proj(x_bf, w_t, l, 0, N_MIX, F32, False, tm_proj, TN_PROJ, "proj")
        gates, wa, wb, wo = _proj(x_bf, w_t, l, COL_GATE, 2 * D_MODEL, BF16, True, tm_proj, TN_PROJ, "gates",
                                  cast=(w_a_out, w_b_out, w_o))

        wd, ya_p, og_p, utail, gtail, s_p = _prompt_mixer(
            l, x_bf, w_logit, proj, conv_a_w, gdn_conv_w, par, batch, seq, cast=w_down)
        ya_s, og_s, u_s, gstate_s, sg_s = _sample_mixer(
            l, x_bf, w_logit, proj, conv_a_w, gdn_conv_w, par, sta, stg, state_gdn, sg_s, n_prompt // GROUP)

        x1, wu = _out_proj(l, ya_p, ya_s, og_p, og_s, gates, x_lo, x_hi, hi_block0, wa, wb, wo,
                           row3(ln1_g), row3(ln1_b), tm_out, cast=(w_up,))
        if l < DEPTH - 1:
            x, x_bf = _mlp(l, x1, wu, wd, row3(ln2_g), row3(ln2_b), tm_mlp, TF_MLP)
            x_lo, x_hi, hi_block0 = x, x, n_prompt // tm_out
        else:
            y_p, y_s = _mlp(l, x1, wu, wd, row3(ln2_g), row3(ln2_b), tm_mlp, TF_MLP, n_prompt=n_prompt)

        ca_p.append(utail[:, SUBLANE - state_conv_a.shape[2]:])
        cg_p.append(gtail[:, SUBLANE - state_gdn_conv.shape[2]:])
        sg_p.append(s_p)
        ca_s.append(u_s.reshape(n_seq, SAMPLE_LEN, D_CONV)[:, SAMPLE_LEN - state_conv_a.shape[2]:])
        cg_s.append(gstate_s)

    return (y_p.reshape(batch, seq, D_MODEL), y_s.reshape(n_seq, SAMPLE_LEN, D_MODEL),
            jnp.stack(ca_p), jnp.stack(cg_p), jnp.stack(sg_p),
            jnp.stack(ca_s), jnp.swapaxes(jnp.stack(cg_s), 1, 2), sg_s)
```

```python
import functools

import jax
import jax.numpy as jnp
from jax import lax
from jax.experimental import pallas as pl
from jax.experimental.pallas import tpu as pltpu

F32 = jnp.float32
BF16 = jnp.bfloat16

D_MODEL = 2048
DEPTH = 2
D_CONV = D_MODEL // 2
N_HEADS = 8
HEAD_DIM = 128
D_GDN = N_HEADS * HEAD_DIM
D_FF = 4 * D_MODEL
ALPHA = (2 * DEPTH) ** 0.25
LN_EPS = 1e-5
NORM_EPS = 1e-6

COL_GB, COL_GC, COL_H = 0, D_CONV, 2 * D_CONV
COL_QKV = 3 * D_CONV
COL_Z = COL_QKV + 3 * D_GDN
N_MIX = COL_Z + D_GDN
COL_LOGIT = N_MIX
COL_GATE = COL_LOGIT + 2 * N_HEADS
LANE = 128
SUBLANE = 8
LOGIT_B, LOGIT_A = 0, N_HEADS

GROUP = 64
PROMPT_SUB = 4
SAMPLE_LEN = 4
SEQ_PER_GROUP = GROUP // SAMPLE_LEN

VMEM_LIMIT = 56 * 1024 * 1024

TM_PROJ, TN_PROJ = 1088, 1024
TM_OUT = 256
TM_MLP, TF_MLP = 512, 1024


def _dot(a, b):
    return jnp.dot(a.astype(BF16), b.astype(BF16), preferred_element_type=F32)


def _dot_nt(a, b):
    return lax.dot_general(a.astype(BF16), b.astype(BF16), (((1,), (1,)), ((), ())),
                           preferred_element_type=F32)


def _dot_exact_lhs(m01, x):
    x1 = x.astype(BF16)
    r1 = x - x1.astype(F32)
    x2 = r1.astype(BF16)
    x3 = (r1 - x2.astype(F32)).astype(BF16)
    d = lambda y: jnp.dot(m01, y, preferred_element_type=F32)
    return d(x1) + (d(x2) + d(x3))


def _same_block(i, j, size):
    shift = size.bit_length() - 1
    assert size == 1 << shift
    return (i >> shift) == (j >> shift)


def _sigmoid(x):
    return 0.5 * jnp.tanh(0.5 * x) + 0.5


def _softplus(x):
    return jnp.maximum(x, 0.0) + jnp.log1p(jnp.exp(-jnp.abs(x)))


def _layer_norm(y, g, b):
    mu = jnp.mean(y, axis=-1, keepdims=True)
    yc = y - mu
    var = jnp.mean(yc * yc, axis=-1, keepdims=True)
    return yc * lax.rsqrt(var + LN_EPS) * g + b


def _layer_spec(layer, shape, index_map=None, **kw):
    if index_map is None:
        index_map = lambda *g: (0,) * len(shape)
    return pl.BlockSpec((None,) + tuple(shape), lambda *g: (layer,) + tuple(index_map(*g)), **kw)


def _params(*sem):
    return pltpu.CompilerParams(dimension_semantics=sem, vmem_limit_bytes=VMEM_LIMIT)


def _stream_kernel(lo_ref, hi_ref, o_ref, *, n_lo):
    o_ref[...] = jnp.where(pl.program_id(0) < n_lo, lo_ref[...], hi_ref[...]).astype(o_ref.dtype)


def _bf16_stream(x_lo, x_hi, tm):
    n_lo, n_hi = x_lo.shape[0] // tm, x_hi.shape[0] // tm
    return pl.pallas_call(
        functools.partial(_stream_kernel, n_lo=n_lo),
        grid=(n_lo + n_hi,),
        in_specs=[pl.BlockSpec((tm, D_MODEL), lambda i: (jnp.minimum(i, n_lo - 1), 0)),
                  pl.BlockSpec((tm, D_MODEL), lambda i: (jnp.maximum(i - n_lo, 0), 0))],
        out_specs=pl.BlockSpec((tm, D_MODEL), lambda i: (i, 0)),
        out_shape=jax.ShapeDtypeStruct((x_lo.shape[0] + x_hi.shape[0], D_MODEL), BF16),
        compiler_params=_params("arbitrary"),
        name="bf16_stream",
    )(x_lo, x_hi)


def _proj_kernel(x_ref, wt_hbm, *rest, gate, n_cast, layer, col0):
    cast_in, o_ref, cast_out = rest[:n_cast], rest[n_cast], rest[n_cast + 1:2 * n_cast + 1]
    wb_ref, wf_ref, sem = rest[-3:]
    j, n_j = pl.program_id(0), pl.num_programs(0)
    tn = wb_ref.shape[0]

    def weight_copy(block, slot):
        rows = pl.ds(pl.multiple_of(col0 + block * tn, 2 * SUBLANE), tn)
        return pltpu.make_async_copy(wt_hbm.at[layer, rows, :], wf_ref.at[slot], sem.at[slot])

    @pl.when(pl.program_id(1) == 0)
    def _():
        slot = j % 2

        @pl.when(j == 0)
        def _():
            weight_copy(0, 0).start()

        weight_copy(j, slot).wait()
        wb_ref[...] = wf_ref[slot].astype(BF16)

        @pl.when(j + 1 < n_j)
        def _():
            weight_copy(j + 1, 1 - slot).start()

    y = lax.dot_general(x_ref[...], wb_ref[...], (((1,), (1,)), ((), ())), preferred_element_type=F32)
    if gate:
        y = _sigmoid(y)
    o_ref[...] = y.astype(o_ref.dtype)
    for src, dst in zip(cast_in, cast_out):
        dst[...] = src[...].astype(dst.dtype)


def _proj(x_bf, w_t, layer, col0, n_cols, out_dtype, gate, tm, tn, name, cast=()):
    t = x_bf.shape[0]
    n_j, n_i = n_cols // tn, t // tm
    step = lambda j, i: j * n_i + i
    rows = [w.shape[1] // (n_j * n_i) for w in cast]
    assert all(r % 16 == 0 and r * n_j * n_i == w.shape[1] for r, w in zip(rows, cast))
    outs = pl.pallas_call(
        functools.partial(_proj_kernel, gate=gate, n_cast=len(cast), layer=layer, col0=col0),
        grid=(n_j, n_i),
        in_specs=[pl.BlockSpec((tm, D_MODEL), lambda j, i: (i, 0)),
                  pl.BlockSpec(memory_space=pl.ANY)]
        + [_layer_spec(layer, (r, w.shape[2]), lambda j, i: (step(j, i), 0)) for r, w in zip(rows, cast)],
        out_specs=[pl.BlockSpec((tm, tn), lambda j, i: (i, j))]
        + [pl.BlockSpec((r, w.shape[2]), lambda j, i: (step(j, i), 0)) for r, w in zip(rows, cast)],
        out_shape=[jax.ShapeDtypeStruct((t, n_cols), out_dtype)]
        + [jax.ShapeDtypeStruct(w.shape[1:], BF16) for w in cast],
        scratch_shapes=[pltpu.VMEM((tn, D_MODEL), BF16), pltpu.VMEM((2, tn, D_MODEL), F32),
                        pltpu.SemaphoreType.DMA((2,))],
        compiler_params=_params("arbitrary", "arbitrary"),
        name=name,
    )(x_bf, w_t, *cast)
    return outs if cast else outs[0]


def _causal_conv(u, w_ref, prev_fn):
    k = w_ref.shape[0]
    out = u * w_ref[k - 1:k, :]
    for j in range(1, k):
        out = out + prev_fn(j) * w_ref[k - 1 - j:k - j, :]
    return out


CONV_PHASES = 4


def _prompt_conv(u, w_ref, buf_ref, out_ref, act=None, scale=None):
    k = w_ref.shape[0]
    n, c = u.shape
    slabs = c // LANE
    for s in range(slabs):
        buf_ref[s, SUBLANE:, :] = u[:, s * LANE:(s + 1) * LANE]
    for s in range(slabs):
        w = [w_ref[j:j + 1, s * LANE:(s + 1) * LANE] for j in range(k)]
        if scale is not None:
            w = [x * scale for x in w]
        for b in range(CONV_PHASES):
            acc = None
            for j in range(k):
                tap = buf_ref[s, pl.ds(SUBLANE - (k - 1) + j + b, n // CONV_PHASES, stride=CONV_PHASES), :]
                acc = tap * w[j] if acc is None else acc + tap * w[j]
            out_ref[s, pl.ds(b, n // CONV_PHASES, stride=CONV_PHASES), :] = acc if act is None else act(acc)
    for s in range(slabs):
        buf_ref[s, 0:SUBLANE, :] = buf_ref[s, n:n + SUBLANE, :]


def _slabs_value(ref):
    return jnp.concatenate([ref[s] for s in range(ref.shape[0])], axis=1)


def _sample_prev(u, st_ref, fill_ref):
    t = lax.broadcasted_iota(jnp.int32, u.shape, 0) & (SAMPLE_LEN - 1)
    k1 = st_ref.shape[0]
    slabs = u.shape[1] // LANE

    def prev(j):
        for tt in range(j):
            for s in range(slabs):
                fill_ref[j - 1, s, pl.ds(tt, SEQ_PER_GROUP, stride=SAMPLE_LEN), :] = (
                    st_ref[k1 - j + tt, :, s * LANE:(s + 1) * LANE])
        fill = jnp.concatenate([fill_ref[j - 1, s] for s in range(slabs)], axis=1)
        return jnp.where(t >= j, pltpu.roll(u, j, 0), fill)
    return prev


def _pair_iotas():
    row = lax.broadcasted_iota(jnp.int32, (GROUP, 2 * GROUP), 0)
    lane = lax.broadcasted_iota(jnp.int32, (GROUP, 2 * GROUP), 1)
    return row, lane & (GROUP - 1), lane >= GROUP


def _block_diag(y0, y1):
    z = jnp.zeros_like(y0)
    return jnp.concatenate([jnp.concatenate([y0, z], axis=1), jnp.concatenate([z, y1], axis=1)], axis=0)


def _pair_block_diag(y):
    _, _, second = _pair_iotas()
    return jnp.concatenate([jnp.where(second, 0.0, y), jnp.where(second, y, 0.0)], axis=0)


def _tri_inverse_pairs(ms, seq_len):
    row, col, _ = _pair_iotas()
    eye = (row == col).astype(F32)
    base = min(8, seq_len)
    mm = lambda x, y: _dot(x, _pair_block_diag(y))
    same = lambda b: _same_block(row, col, b)
    d = [jnp.where(same(base), m, 0.0) for m in ms]
    d2 = [mm(x, x) for x in d]
    xs = [eye - x for x in d]
    if base > 4:
        d4 = [mm(y, y) for y in d2]
    inv = [x + mm(x, y) for x, y in zip(xs, d2)]
    if base > 4:
        inv = [i + mm(i, y) for i, y in zip(inv, d4)]
    b = base * 2
    while b <= seq_len:
        off_diag = same(b) & jnp.logical_not(same(b // 2))
        t = [mm(i, jnp.where(off_diag, m, 0.0)) for i, m in zip(inv, ms)]
        inv = [i - mm(x, i) for i, x in zip(inv, t)]
        b *= 2
    return inv


def _gdn_group(q, k, v, z, logit, a_row, dt_row, nw_row, seq_len, state, og_ref):
    row = lax.broadcasted_iota(jnp.int32, (GROUP, GROUP), 0)
    col = lax.broadcasted_iota(jnp.int32, (GROUP, GROUP), 1)
    same_seq = _same_block(row, col, seq_len)
    lmat = jnp.concatenate([(same_seq & (row >= col)).astype(F32), same_seq.astype(F32)], axis=0).astype(BF16)
    prow, pcol, second = _pair_iotas()
    psame = _same_block(prow, pcol, seq_len)
    causal = psame & (prow >= pcol)
    strict = psame & (prow > pcol)

    subs = range(q.shape[0] // GROUP)
    heads = range(N_HEADS)
    pairs = range(0, N_HEADS, 2)
    rows = lambda x, c: x[c * GROUP:(c + 1) * GROUP]
    hs = lambda x, c, h: x[c * GROUP:(c + 1) * GROUP, h * HEAD_DIM:(h + 1) * HEAD_DIM]
    lcol = lambda x, h: x[:, LOGIT_A + h:LOGIT_A + h + 1]
    cat = jnp.concatenate

    beta = _sigmoid(logit)
    g = -jnp.exp(a_row) * _softplus(logit + dt_row)
    units = [(c, h) for c in subs for h in pairs]
    gc2 = {c: _dot_exact_lhs(lmat, rows(g, c)) for c in subs}
    gcum = {c: x[:GROUP] for c, x in gc2.items()}
    gtot = {c: x[GROUP:] for c, x in gc2.items()}
    gcum_t2 = {c: cat([x, x], axis=0).T for c, x in gcum.items()}
    e_cum = {c: jnp.exp(x) for c, x in gcum.items()}
    e_rem = {c: jnp.exp(gtot[c] - gcum[c]) for c in subs}
    e_tot = {c: jnp.exp(t) for c, t in gtot.items()}

    qn, kn, bcol, kb = {}, {}, {}, {}
    for c in subs:
        for h in heads:
            qh, kh = hs(q, c, h), hs(k, c, h)
            qn[c, h] = qh * (lax.rsqrt(jnp.sum(qh * qh, axis=-1, keepdims=True) + NORM_EPS) * (HEAD_DIM ** -0.5))
            kn[c, h] = kh * lax.rsqrt(jnp.sum(kh * kh, axis=-1, keepdims=True) + NORM_EPS)
            bcol[c, h] = rows(beta, c)[:, LOGIT_B + h:LOGIT_B + h + 1]
            kb[c, h] = kn[c, h] * bcol[c, h]

    a = {(c, h): _dot_nt(cat([cat([kb[c, h], kb[c, h + 1]], axis=1), cat([qn[c, h], qn[c, h + 1]], axis=1)], axis=0),
                         _block_diag(kn[c, h], kn[c, h + 1])) for c, h in units}
    ms, attn = {}, {}
    for c, h in units:
        gcol = jnp.where(second, lcol(gcum[c], h + 1), lcol(gcum[c], h))
        grow = jnp.where(second[0:1], gcum_t2[c][LOGIT_A + h + 1:LOGIT_A + h + 2, :],
                         gcum_t2[c][LOGIT_A + h:LOGIT_A + h + 1, :])
        dec = jnp.exp(jnp.where(causal, gcol - grow, 0.0))
        ms[c, h] = jnp.where(strict, a[c, h][:GROUP] * dec, 0.0)
        attn[c, h] = jnp.where(causal, a[c, h][GROUP:] * dec, 0.0)
    t_inv = dict(zip(units, _tri_inverse_pairs([ms[un] for un in units], seq_len)))

    u, w, qdec, kdec = {}, {}, {}, {}
    for c, h in units:
        rhs = [cat([hs(v, c, hh) * bcol[c, hh], kb[c, hh] * lcol(e_cum[c], hh)], axis=1) for hh in (h, h + 1)]
        uw = _dot(t_inv[c, h], _block_diag(*rhs))
        for i, hh in enumerate((h, h + 1)):
            u[c, hh] = uw[:, 2 * i * HEAD_DIM:(2 * i + 1) * HEAD_DIM]
            w[c, hh] = uw[:, (2 * i + 1) * HEAD_DIM:(2 * i + 2) * HEAD_DIM]
            qdec[c, hh] = qn[c, hh] * lcol(e_cum[c], hh)
            kdec[c, hh] = kn[c, hh] * lcol(e_rem[c], hh)

    for c in subs:
        ws = [state.read(h, cat([w[c, h], qdec[c, h]], axis=0)) for h in heads]
        v_new = [u[c, h] - ws[h][0] for h in heads]
        vn_bd = {h: _block_diag(v_new[h], v_new[h + 1]) for h in pairs}
        o_pair = {h: cat([ws[h][1], ws[h + 1][1]], axis=1) + _dot(attn[c, h], vn_bd[h]) for h in pairs}
        for h in pairs:
            state.update(h, cat([kdec[c, h], kdec[c, h + 1]], axis=0).T, vn_bd[h],
                         lcol(e_tot[c], h), lcol(e_tot[c], h + 1))
        for h in heads:
            o = o_pair[h - h % 2][:, (h % 2) * HEAD_DIM:(h % 2 + 1) * HEAD_DIM]
            o = o * lax.rsqrt(jnp.mean(o * o, axis=-1, keepdims=True) + NORM_EPS) * nw_row
            zh = hs(z, c, h)
            og_ref[c * GROUP:(c + 1) * GROUP, h * HEAD_DIM:(h + 1) * HEAD_DIM] = (
                o * _silu(zh)).astype(og_ref.dtype)
    state.finish()


def _silu_of_half(h):
    return h * (jnp.tanh(h) + 1.0)


def _silu(x):
    return _silu_of_half(0.5 * x)


def _mixer_common(x_ref, wl_ref, abc_ref, z_ref, par_ref, conv_a, qkv_act, seq_len, state, ya_ref, og_ref):
    gb = abc_ref[:, COL_GB:COL_GB + D_CONV]
    gc = abc_ref[:, COL_GC:COL_GC + D_CONV]
    hv = abc_ref[:, COL_H:COL_H + D_CONV]
    u = gc * hv
    ya_ref[...] = (gb * conv_a(u)).astype(ya_ref.dtype)

    act = qkv_act()
    q, k, v = act[:, :D_GDN], act[:, D_GDN:2 * D_GDN], act[:, 2 * D_GDN:]
    logit = lax.dot_general(x_ref[...], wl_ref[...], (((1,), (1,)), ((), ())), preferred_element_type=F32)
    _gdn_group(q, k, v, z_ref[...], logit, par_ref[0:1, :], par_ref[1:2, :], par_ref[2:3, :],
               seq_len, state, og_ref)
    return u


def _prompt_mixer_kernel(x_ref, wl_ref, abc_ref, qkv_ref, z_ref, caw_ref, gcw_ref, par_ref,
                         ya_ref, og_ref, utail_ref, gtail_ref, s_ref, ubuf_ref, uout_ref, gbuf_ref, gout_ref):
    @pl.when(pl.program_id(1) == 0)
    def _():
        ubuf_ref[:, 0:SUBLANE, :] = jnp.zeros((D_CONV // LANE, SUBLANE, LANE), F32)
        gbuf_ref[:, 0:SUBLANE, :] = jnp.zeros((3 * D_GDN // LANE, SUBLANE, LANE), F32)
        s_ref[...] = jnp.zeros_like(s_ref)

    cur = {}

    class State:
        @staticmethod
        def read(h, lhs):
            if h not in cur:
                cur[h] = s_ref[0, h]
            ws = _dot(lhs, cur[h])
            return ws[:GROUP], ws[GROUP:]

        @staticmethod
        def update(h, kd_t, vn_bd, e0, e1):
            kv = _dot(kd_t, vn_bd)
            cur[h] = cur[h] * e0[0:1, :] + kv[:, :HEAD_DIM]
            cur[h + 1] = cur[h + 1] * e1[0:1, :] + kv[:, HEAD_DIM:]

        @staticmethod
        def finish():
            for h, s in cur.items():
                s_ref[0, h] = s

    def conv_a(u):
        _prompt_conv(u, caw_ref, ubuf_ref, uout_ref)
        return _slabs_value(uout_ref)

    def qkv_act():
        _prompt_conv(qkv_ref[...], gcw_ref, gbuf_ref, gout_ref, _silu_of_half, scale=0.5)
        return _slabs_value(gout_ref)

    u = _mixer_common(x_ref, wl_ref, abc_ref, z_ref, par_ref, conv_a, qkv_act, GROUP, State, ya_ref, og_ref)
    n = u.shape[0]
    utail_ref[0] = u[n - SUBLANE:]
    gtail_ref[0] = qkv_ref[n - SUBLANE:, :]


def _sample_mixer_kernel(x_ref, wl_ref, abc_ref, qkv_ref, z_ref, caw_ref, gcw_ref, par_ref,
                         sta_ref, stg_ref, s0_ref, *rest):
    ya_ref, og_ref, u_ref, gstate_ref, s_ref, fill_a_ref, fill_g_ref, raw_slab_ref = rest[-8:]

    @pl.when(pl.program_id(0) == 0)
    def _():
        fill_a_ref[...] = jnp.zeros_like(fill_a_ref)
        fill_g_ref[...] = jnp.zeros_like(fill_g_ref)

    first = lax.broadcasted_iota(jnp.int32, (SUBLANE, HEAD_DIM), 0) < SAMPLE_LEN
    row_bd = lax.broadcasted_iota(jnp.int32, (2 * GROUP, 2 * HEAD_DIM), 0) & (GROUP - 1)

    class State:
        @staticmethod
        def read(h, lhs):
            w, qdec = lhs[:GROUP], lhs[GROUP:]
            ws_w, ws_q = [], []
            for p in range(GROUP // SUBLANE):
                r = slice(SUBLANE * p, SUBLANE * (p + 1))
                slab = jnp.concatenate([w[r], qdec[r]], axis=0)
                r0 = _dot(slab, s0_ref[2 * p, h])
                r1 = _dot(slab, s0_ref[2 * p + 1, h])
                ws_w.append(jnp.where(first, r0[:SUBLANE], r1[:SUBLANE]))
                ws_q.append(jnp.where(first, r0[SUBLANE:], r1[SUBLANE:]))
            return jnp.concatenate(ws_w, axis=0), jnp.concatenate(ws_q, axis=0)

        @staticmethod
        def update(h, kd_t, vn_bd, e0, e1):
            for s in range(SEQ_PER_GROUP):
                in_seq = _same_block(row_bd, SAMPLE_LEN * s, SAMPLE_LEN)
                kv = _dot(kd_t, jnp.where(in_seq, vn_bd, 0.0))
                r = SAMPLE_LEN * s
                s_ref[s, h] = s0_ref[s, h] * e0[r:r + 1, :] + kv[:, :HEAD_DIM]
                s_ref[s, h + 1] = s0_ref[s, h + 1] * e1[r:r + 1, :] + kv[:, HEAD_DIM:]

        @staticmethod
        def finish():
            pass

    def qkv_act():
        raw = qkv_ref[...]
        for s in range(raw_slab_ref.shape[0]):
            raw_slab_ref[s] = raw[:, s * LANE:(s + 1) * LANE]
        for t in range(1, SAMPLE_LEN):
            gstate_ref[t - 1] = jnp.concatenate(
                [raw_slab_ref[s, pl.ds(t, SEQ_PER_GROUP, stride=SAMPLE_LEN), :] for s in range(raw_slab_ref.shape[0])],
                axis=1)
        return _silu(_causal_conv(raw, gcw_ref, _sample_prev(raw, stg_ref, fill_g_ref)))

    u_ref[...] = _mixer_common(
        x_ref, wl_ref, abc_ref, z_ref, par_ref,
        lambda u: _causal_conv(u, caw_ref, _sample_prev(u, sta_ref, fill_a_ref)), qkv_act,
        SAMPLE_LEN, State, ya_ref, og_ref)


def _mixer_in_specs(layer, row_block, n_rows=GROUP):
    rows = lambda width, col_block: pl.BlockSpec((n_rows, width), lambda *i: (row_block(*i), col_block))
    return [
        rows(D_MODEL, 0),
        _layer_spec(layer, (LANE, D_MODEL)),
        rows(3 * D_CONV, 0),
        rows(3 * D_GDN, COL_QKV // (3 * D_GDN)),
        rows(D_GDN, COL_Z // D_GDN),
        _layer_spec(layer, (3, D_CONV)),
        _layer_spec(layer, (4, 3 * D_GDN)),
        _layer_spec(layer, (SUBLANE, LANE)),
    ]


def _prompt_mixer(layer, x_bf, w_logit, proj, caw, gcw, par, batch, seq):
    step = GROUP * PROMPT_SUB
    chunks = seq // step
    rb = lambda b, c: b * chunks + c
    n = batch * seq
    return pl.pallas_call(
        _prompt_mixer_kernel,
        grid=(batch, chunks),
        in_specs=_mixer_in_specs(layer, rb, step),
        out_specs=[
            pl.BlockSpec((step, D_CONV), lambda b, c: (rb(b, c), 0)),
            pl.BlockSpec((step, D_GDN), lambda b, c: (rb(b, c), 0)),
            pl.BlockSpec((1, SUBLANE, D_CONV), lambda b, c: (b, 0, 0)),
            pl.BlockSpec((1, SUBLANE, 3 * D_GDN), lambda b, c: (b, 0, 0)),
            pl.BlockSpec((1, N_HEADS, HEAD_DIM, HEAD_DIM), lambda b, c: (b, 0, 0, 0)),
        ],
        out_shape=[
            jax.ShapeDtypeStruct((n, D_CONV), BF16),
            jax.ShapeDtypeStruct((n, D_GDN), BF16),
            jax.ShapeDtypeStruct((batch, SUBLANE, D_CONV), F32),
            jax.ShapeDtypeStruct((batch, SUBLANE, 3 * D_GDN), F32),
            jax.ShapeDtypeStruct((batch, N_HEADS, HEAD_DIM, HEAD_DIM), F32),
        ],
        scratch_shapes=[pltpu.VMEM((D_CONV // LANE, step + SUBLANE, LANE), F32),
                        pltpu.VMEM((D_CONV // LANE, step, LANE), F32),
                        pltpu.VMEM((3 * D_GDN // LANE, step + SUBLANE, LANE), F32),
                        pltpu.VMEM((3 * D_GDN // LANE, step, LANE), F32)],
        compiler_params=_params("arbitrary", "arbitrary"),
        name="prompt_mixer",
    )(x_bf, w_logit, proj, proj, proj, caw, gcw, par)


def _sample_mixer(layer, x_bf, w_logit, proj, caw, gcw, par, sta, stg, s0, s_prev, first_block):
    n_seq = s0.shape[1]
    rb = lambda g: first_block + g
    state_block = (SEQ_PER_GROUP, N_HEADS, HEAD_DIM, HEAD_DIM)
    in_specs = _mixer_in_specs(layer, rb) + [
        _layer_spec(layer, (sta.shape[1], SEQ_PER_GROUP, D_CONV), lambda g: (0, g, 0)),
        _layer_spec(layer, (stg.shape[1], SEQ_PER_GROUP, 3 * D_GDN), lambda g: (0, g, 0)),
        _layer_spec(layer, state_block, lambda g: (g, 0, 0, 0)),
    ]
    args = [x_bf, w_logit, proj, proj, proj, caw, gcw, par, sta, stg, s0]
    aliases = {}
    if s_prev is not None:
        in_specs.append(pl.BlockSpec(memory_space=pl.ANY))
        aliases = {len(args): 4}
        args.append(s_prev)
    return pl.pallas_call(
        _sample_mixer_kernel,
        grid=(n_seq // SEQ_PER_GROUP,),
        in_specs=in_specs,
        out_specs=[
            pl.BlockSpec((GROUP, D_CONV), lambda g: (g, 0)),
            pl.BlockSpec((GROUP, D_GDN), lambda g: (g, 0)),
            pl.BlockSpec((GROUP, D_CONV), lambda g: (g, 0)),
            pl.BlockSpec((SAMPLE_LEN - 1, SEQ_PER_GROUP, 3 * D_GDN), lambda g: (0, g, 0)),
            _layer_spec(layer, state_block, lambda g: (g, 0, 0, 0)),
        ],
        out_shape=[
            jax.ShapeDtypeStruct((n_seq * SAMPLE_LEN, D_CONV), BF16),
            jax.ShapeDtypeStruct((n_seq * SAMPLE_LEN, D_GDN), BF16),
            jax.ShapeDtypeStruct((n_seq * SAMPLE_LEN, D_CONV), F32),
            jax.ShapeDtypeStruct((SAMPLE_LEN - 1, n_seq, 3 * D_GDN), F32),
            jax.ShapeDtypeStruct(s0.shape, F32),
        ],
        input_output_aliases=aliases,
        scratch_shapes=[pltpu.VMEM((sta.shape[1], D_CONV // LANE, GROUP, LANE), F32),
                        pltpu.VMEM((stg.shape[1], 3 * D_GDN // LANE, GROUP, LANE), F32),
                        pltpu.VMEM((3 * D_GDN // LANE, GROUP, LANE), F32)],
        compiler_params=_params("arbitrary"),
        name="sample_mixer",
    )(*args)


def _out_proj_kernel(ya_lo, ya_hi, og_lo, og_hi, gate_ref, x_lo, x_hi, wa_ref, wb_ref, wo_ref, g_ref, b_ref,
                     *rest, n_lo):
    n_cast = (len(rest) - 1) // 2
    cast_in, o_ref, cast_out = rest[:n_cast], rest[n_cast], rest[n_cast + 1:]
    for src, dst in zip(cast_in, cast_out):
        dst[...] = src[...].astype(dst.dtype)
    lo = pl.program_id(0) < n_lo
    half = o_ref.shape[0] // 2
    halves = (slice(0, half), slice(half, 2 * half))
    pick = lambda a, b, r: jnp.where(lo, a[r, :], b[r, :])
    dot = lambda a, w_ref: jnp.dot(a, w_ref[...], preferred_element_type=F32)
    y_a = [dot(pick(ya_lo, ya_hi, r), wa_ref) for r in halves]
    y_b = [dot(pick(og_lo, og_hi, r), wb_ref) for r in halves]
    merged = [gate_ref[r, :D_MODEL].astype(F32) * a + gate_ref[r, D_MODEL:].astype(F32) * b
              for r, a, b in zip(halves, y_a, y_b)]
    z = [dot(m.astype(BF16), wo_ref) for m in merged]
    for r, zz in zip(halves, z):
        o_ref[r, :] = _layer_norm(ALPHA * pick(x_lo, x_hi, r) + zz, g_ref[...], b_ref[...])


def _out_proj(layer, ya_p, ya_s, og_p, og_s, gates, x_lo, x_hi, hi_block0, wa, wb, wo, g, b, tm, cast=()):
    t = gates.shape[0]
    n, n_lo = t // tm, ya_p.shape[0] // tm
    lo = lambda width: pl.BlockSpec((tm, width), lambda i: (jnp.minimum(i, n_lo - 1), 0))
    hi = lambda width, block0=0: pl.BlockSpec((tm, width), lambda i: (jnp.maximum(i - n_lo, 0) + block0, 0),
                                              pipeline_mode=pl.Buffered(1))
    const = lambda shape: _layer_spec(layer, shape, pipeline_mode=pl.Buffered(1))
    weight = lambda shape: pl.BlockSpec(shape, lambda i: (0, 0), pipeline_mode=pl.Buffered(1))
    n_c = 1 << (n.bit_length() - 1)
    rows = [w.shape[1] // n_c for w in cast]
    assert all(r % (2 * SUBLANE) == 0 and r * n_c == w.shape[1] for r, w in zip(rows, cast))
    cast_block = lambda i: (jnp.minimum(i, n_c - 1), 0)
    outs = pl.pallas_call(
        functools.partial(_out_proj_kernel, n_lo=n_lo),
        grid=(n,),
        in_specs=[
            lo(D_CONV), hi(D_CONV), lo(D_GDN), hi(D_GDN),
            pl.BlockSpec((tm, 2 * D_MODEL), lambda i: (i, 0)),
            lo(D_MODEL), hi(D_MODEL, hi_block0),
            weight((D_CONV, D_MODEL)), weight((D_GDN, D_MODEL)), weight((D_MODEL, D_MODEL)),
            const((1, D_MODEL)), const((1, D_MODEL)),
        ] + [_layer_spec(layer, (r, w.shape[2]), cast_block) for r, w in zip(rows, cast)],
        out_specs=[pl.BlockSpec((tm, D_MODEL), lambda i: (i, 0))]
        + [pl.BlockSpec((r, w.shape[2]), cast_block) for r, w in zip(rows, cast)],
        out_shape=[jax.ShapeDtypeStruct((t, D_MODEL), F32)]
        + [jax.ShapeDtypeStruct(w.shape[1:], BF16) for w in cast],
        compiler_params=_params("arbitrary"),
        name="out_proj",
    )(ya_p, ya_s, og_p, og_s, gates, x_lo, x_hi, wa, wb, wo, g, b, *cast)
    return outs


def _mlp_kernel(x_hbm, wu_ref, wd_ref, g_ref, b_ref, o0_ref, o1_ref, xb_ref, acc_ref, xf_ref, sem, *, n_lo):
    i, j = pl.program_id(0), pl.program_id(1)
    tm = xb_ref.shape[0]
    slot = i % 2
    x_ref = xf_ref.at[slot]

    def x_copy(tile, dst_slot):
        rows = pl.ds(pl.multiple_of(tile * tm, 2 * SUBLANE), tm)
        return pltpu.make_async_copy(x_hbm.at[rows, :], xf_ref.at[dst_slot], sem.at[dst_slot])

    @pl.when(j == 0)
    def _():
        @pl.when(i == 0)
        def _():
            x_copy(0, 0).start()

        x_copy(i, slot).wait()
        xb_ref[...] = x_ref[...].astype(BF16)
        acc_ref[...] = jnp.zeros_like(acc_ref)

        @pl.when(i + 1 < pl.num_programs(0))
        def _():
            x_copy(i + 1, 1 - slot).start()

    hid = jnp.maximum(jnp.dot(xb_ref[...], wu_ref[...], preferred_element_type=F32), 0.0)
    acc_ref[...] += jnp.dot((hid * hid).astype(BF16), wd_ref[...], preferred_element_type=F32)

    @pl.when(j == pl.num_programs(1) - 1)
    def _():
        y = _layer_norm(ALPHA * x_ref[...] + acc_ref[...], g_ref[...], b_ref[...])
        if n_lo is None:
            o0_ref[...] = y
            o1_ref[...] = y.astype(BF16)
        else:
            @pl.when(i < n_lo)
            def _():
                o0_ref[...] = y

            @pl.when(i >= n_lo)
            def _():
                o1_ref[...] = y


def _mlp(layer, x, wu, wd, g, b, tm, tf, n_prompt=None):
    t = x.shape[0]
    if n_prompt is None:
        n_lo = None
        out_specs = [pl.BlockSpec((tm, D_MODEL), lambda i, j: (i, 0)),
                     pl.BlockSpec((tm, D_MODEL), lambda i, j: (i, 0))]
        out_shape = [jax.ShapeDtypeStruct((t, D_MODEL), F32), jax.ShapeDtypeStruct((t, D_MODEL), BF16)]
    else:
        n_lo = n_prompt // tm
        out_specs = [pl.BlockSpec((tm, D_MODEL), lambda i, j: (jnp.minimum(i, n_lo - 1), 0)),
                     pl.BlockSpec((tm, D_MODEL), lambda i, j: (jnp.maximum(i - n_lo, 0), 0))]
        out_shape = [jax.ShapeDtypeStruct((n_prompt, D_MODEL), F32),
                     jax.ShapeDtypeStruct((t - n_prompt, D_MODEL), F32)]
    return pl.pallas_call(
        functools.partial(_mlp_kernel, n_lo=n_lo),
        grid=(t // tm, D_FF // tf),
        in_specs=[
            pl.BlockSpec(memory_space=pl.ANY),
            pl.BlockSpec((D_MODEL, tf), lambda i, j: (0, j)),
            pl.BlockSpec((tf, D_MODEL), lambda i, j: (j, 0)),
            _layer_spec(layer, (1, D_MODEL)),
            _layer_spec(layer, (1, D_MODEL)),
        ],
        out_specs=out_specs,
        out_shape=out_shape,
        scratch_shapes=[pltpu.VMEM((tm, D_MODEL), BF16), pltpu.VMEM((tm, D_MODEL), F32),
                        pltpu.VMEM((2, tm, D_MODEL), F32), pltpu.SemaphoreType.DMA((2,))],
        compiler_params=_params("arbitrary", "arbitrary"),
        name="mlp",
    )(x, wu, wd, g, b)


def kernel(x_prompt, x_sample, state_conv_a, state_gdn_conv, state_gdn, w_in, conv_a_w, gdn_conv_w, a_log,
           dt_bias, gdn_norm_w, w_a_out, w_b_out, w_o, ln1_g, ln1_b, w_up, w_down, ln2_g, ln2_b):
    batch, seq, _ = x_prompt.shape
    n_seq = x_sample.shape[0]
    n_prompt = batch * seq
    n_sample = n_seq * SAMPLE_LEN
    tm_proj, tm_out, tm_mlp = TM_PROJ, TM_OUT, TM_MLP
    assert x_sample.shape[1] == SAMPLE_LEN and seq % (GROUP * PROMPT_SUB) == 0 and n_seq % SEQ_PER_GROUP == 0
    assert (n_prompt + n_sample) % tm_proj == 0
    assert n_prompt % tm_out == 0 and n_sample % tm_out == 0 and n_prompt % tm_mlp == 0 and n_sample == tm_mlp

    xp = x_prompt.reshape(n_prompt, D_MODEL)
    xs = x_sample.reshape(n_sample, D_MODEL)
    x_bf = _bf16_stream(xp, xs, tm_mlp)
    x_lo, x_hi, hi_block0 = xp, xs, 0

    w_t = jnp.swapaxes(w_in, 1, 2)
    w_logit = jnp.pad(w_t[:, COL_LOGIT:COL_GATE], ((0, 0), (0, LANE - 2 * N_HEADS), (0, 0))).astype(BF16)
    par = jnp.zeros((DEPTH, SUBLANE, LANE), F32)
    par = par.at[:, 0, LOGIT_A:LOGIT_A + N_HEADS].set(a_log.astype(F32))
    par = par.at[:, 1, LOGIT_A:LOGIT_A + N_HEADS].set(dt_bias.astype(F32))
    par = par.at[:, 2, :].set(gdn_norm_w.astype(F32))
    sta, stg = jnp.swapaxes(state_conv_a, 1, 2), jnp.swapaxes(state_gdn_conv, 1, 2)
    row3 = lambda a: a.reshape(DEPTH, 1, D_MODEL)

    ca_p, cg_p, sg_p, ca_s, cg_s = [], [], [], [], []
    sg_s = None
    for l in range(DEPTH):
        proj = _proj(x_bf, w_t, l, 0, N_MIX, F32, False, tm_proj, TN_PROJ, "proj")
        gates, wa, wb, wo = _proj(x_bf, w_t, l, COL_GATE, 2 * D_MODEL, BF16, True, tm_proj, TN_PROJ, "gates",
                                  cast=(w_a_out, w_b_out, w_o))

        ya_p, og_p, utail, gtail, s_p = _prompt_mixer(l, x_bf, w_logit, proj, conv_a_w, gdn_conv_w, par, batch, seq)
        ya_s, og_s, u_s, gstate_s, sg_s = _sample_mixer(
            l, x_bf, w_logit, proj, conv_a_w, gdn_conv_w, par, sta, stg, state_gdn, sg_s, n_prompt // GROUP)

        x1, wu, wd = _out_proj(l, ya_p, ya_s, og_p, og_s, gates, x_lo, x_hi, hi_block0, wa, wb, wo,
                               row3(ln1_g), row3(ln1_b), tm_out, cast=(w_up, w_down))
        if l < DEPTH - 1:
            x, x_bf = _mlp(l, x1, wu, wd, row3(ln2_g), row3(ln2_b), tm_mlp, TF_MLP)
            x_lo, x_hi, hi_block0 = x, x, n_prompt // tm_out
        else:
            y_p, y_s = _mlp(l, x1, wu, wd, row3(ln2_g), row3(ln2_b), tm_mlp, TF_MLP, n_prompt=n_prompt)

        ca_p.append(utail[:, SUBLANE - state_conv_a.shape[2]:])
        cg_p.append(gtail[:, SUBLANE - state_gdn_conv.shape[2]:])
        sg_p.append(s_p)
        ca_s.append(u_s.reshape(n_seq, SAMPLE_LEN, D_CONV)[:, SAMPLE_LEN - state_conv_a.shape[2]:])
        cg_s.append(gstate_s)

    return (y_p.reshape(batch, seq, D_MODEL), y_s.reshape(n_seq, SAMPLE_LEN, D_MODEL),
            jnp.stack(ca_p), jnp.stack(cg_p), jnp.stack(sg_p),
            jnp.stack(ca_s), jnp.swapaxes(jnp.stack(cg_s), 1, 2), sg_s)
```

```python
import functools

import jax
import jax.numpy as jnp
from jax import lax
from jax.experimental import pallas as pl
from jax.experimental.pallas import tpu as pltpu

F32 = jnp.float32
BF16 = jnp.bfloat16

D_MODEL = 2048
DEPTH = 2
D_CONV = D_MODEL // 2
N_HEADS = 8
HEAD_DIM = 128
D_GDN = N_HEADS * HEAD_DIM
D_FF = 4 * D_MODEL
ALPHA = (2 * DEPTH) ** 0.25
LN_EPS = 1e-5
NORM_EPS = 1e-6

COL_GB, COL_GC, COL_H = 0, D_CONV, 2 * D_CONV
COL_QKV = 3 * D_CONV
COL_Z = COL_QKV + 3 * D_GDN
N_MIX = COL_Z + D_GDN
COL_LOGIT = N_MIX
COL_GATE = COL_LOGIT + 2 * N_HEADS
LANE = 128
SUBLANE = 8
LOGIT_B, LOGIT_A = 0, N_HEADS

GROUP = 64
PROMPT_SUB = 4
SAMPLE_LEN = 4
SEQ_PER_GROUP = GROUP // SAMPLE_LEN

VMEM_LIMIT = 56 * 1024 * 1024

TM_PROJ, TN_PROJ = 1088, 1024
TM_OUT = 256
TM_MLP, TF_MLP = 512, 1024


def _dot(a, b):
    return jnp.dot(a.astype(BF16), b.astype(BF16), preferred_element_type=F32)


def _dot_nt(a, b):
    return lax.dot_general(a.astype(BF16), b.astype(BF16), (((1,), (1,)), ((), ())),
                           preferred_element_type=F32)


def _dot_exact_lhs(m01, x):
    x1 = x.astype(BF16)
    r1 = x - x1.astype(F32)
    x2 = r1.astype(BF16)
    x3 = (r1 - x2.astype(F32)).astype(BF16)
    d = lambda y: jnp.dot(m01, y, preferred_element_type=F32)
    return d(x1) + (d(x2) + d(x3))


def _same_block(i, j, size):
    shift = size.bit_length() - 1
    assert size == 1 << shift
    return (i >> shift) == (j >> shift)


def _sigmoid(x):
    return 0.5 * jnp.tanh(0.5 * x) + 0.5


def _softplus(x):
    return jnp.maximum(x, 0.0) + jnp.log1p(jnp.exp(-jnp.abs(x)))


def _layer_norm(y, g, b):
    mu = jnp.mean(y, axis=-1, keepdims=True)
    yc = y - mu
    var = jnp.mean(yc * yc, axis=-1, keepdims=True)
    return yc * lax.rsqrt(var + LN_EPS) * g + b


def _layer_spec(layer, shape, index_map=None, **kw):
    if index_map is None:
        index_map = lambda *g: (0,) * len(shape)
    return pl.BlockSpec((None,) + tuple(shape), lambda *g: (layer,) + tuple(index_map(*g)), **kw)


def _params(*sem):
    return pltpu.CompilerParams(dimension_semantics=sem, vmem_limit_bytes=VMEM_LIMIT)


def _stream_kernel(lo_ref, hi_ref, o_ref, *, n_lo):
    o_ref[...] = jnp.where(pl.program_id(0) < n_lo, lo_ref[...], hi_ref[...]).astype(o_ref.dtype)


def _bf16_stream(x_lo, x_hi, tm):
    n_lo, n_hi = x_lo.shape[0] // tm, x_hi.shape[0] // tm
    return pl.pallas_call(
        functools.partial(_stream_kernel, n_lo=n_lo),
        grid=(n_lo + n_hi,),
        in_specs=[pl.BlockSpec((tm, D_MODEL), lambda i: (jnp.minimum(i, n_lo - 1), 0)),
                  pl.BlockSpec((tm, D_MODEL), lambda i: (jnp.maximum(i - n_lo, 0), 0))],
        out_specs=pl.BlockSpec((tm, D_MODEL), lambda i: (i, 0)),
        out_shape=jax.ShapeDtypeStruct((x_lo.shape[0] + x_hi.shape[0], D_MODEL), BF16),
        compiler_params=_params("arbitrary"),
        name="bf16_stream",
    )(x_lo, x_hi)


def _proj_kernel(x_ref, wt_hbm, *rest, gate, n_cast, layer, col0):
    cast_in, o_ref, cast_out = rest[:n_cast], rest[n_cast], rest[n_cast + 1:2 * n_cast + 1]
    wb_ref, wf_ref, sem = rest[-3:]
    j, n_j = pl.program_id(0), pl.num_programs(0)
    tn = wb_ref.shape[0]

    def weight_copy(block, slot):
        rows = pl.ds(pl.multiple_of(col0 + block * tn, 2 * SUBLANE), tn)
        return pltpu.make_async_copy(wt_hbm.at[layer, rows, :], wf_ref.at[slot], sem.at[slot])

    @pl.when(pl.program_id(1) == 0)
    def _():
        slot = j % 2

        @pl.when(j == 0)
        def _():
            weight_copy(0, 0).start()

        weight_copy(j, slot).wait()
        wb_ref[...] = wf_ref[slot].astype(BF16)

        @pl.when(j + 1 < n_j)
        def _():
            weight_copy(j + 1, 1 - slot).start(priority=1)

    y = lax.dot_general(x_ref[...], wb_ref[...], (((1,), (1,)), ((), ())), preferred_element_type=F32)
    if gate:
        y = _sigmoid(y)
    o_ref[...] = y.astype(o_ref.dtype)
    for src, dst in zip(cast_in, cast_out):
        dst[...] = src[...].astype(dst.dtype)


def _proj(x_bf, w_t, layer, col0, n_cols, out_dtype, gate, tm, tn, name, cast=()):
    t = x_bf.shape[0]
    n_j, n_i = n_cols // tn, t // tm
    step = lambda j, i: j * n_i + i
    rows = [w.shape[1] // (n_j * n_i) for w in cast]
    assert all(r % 16 == 0 and r * n_j * n_i == w.shape[1] for r, w in zip(rows, cast))
    outs = pl.pallas_call(
        functools.partial(_proj_kernel, gate=gate, n_cast=len(cast), layer=layer, col0=col0),
        grid=(n_j, n_i),
        in_specs=[pl.BlockSpec((tm, D_MODEL), lambda j, i: (i, 0)),
                  pl.BlockSpec(memory_space=pl.ANY)]
        + [_layer_spec(layer, (r, w.shape[2]), lambda j, i: (step(j, i), 0)) for r, w in zip(rows, cast)],
        out_specs=[pl.BlockSpec((tm, tn), lambda j, i: (i, j))]
        + [pl.BlockSpec((r, w.shape[2]), lambda j, i: (step(j, i), 0)) for r, w in zip(rows, cast)],
        out_shape=[jax.ShapeDtypeStruct((t, n_cols), out_dtype)]
        + [jax.ShapeDtypeStruct(w.shape[1:], BF16) for w in cast],
        scratch_shapes=[pltpu.VMEM((tn, D_MODEL), BF16), pltpu.VMEM((2, tn, D_MODEL), F32),
                        pltpu.SemaphoreType.DMA((2,))],
        compiler_params=_params("arbitrary", "arbitrary"),
        name=name,
    )(x_bf, w_t, *cast)
    return outs if cast else outs[0]


def _causal_conv(u, w_ref, prev_fn):
    k = w_ref.shape[0]
    out = u * w_ref[k - 1:k, :]
    for j in range(1, k):
        out = out + prev_fn(j) * w_ref[k - 1 - j:k - j, :]
    return out


CONV_PHASES = 4


def _prompt_conv(u, w_ref, buf_ref, out_ref, act=None, scale=None):
    k = w_ref.shape[0]
    n, c = u.shape
    slabs = c // LANE
    for s in range(slabs):
        buf_ref[s, SUBLANE:, :] = u[:, s * LANE:(s + 1) * LANE]
    for s in range(slabs):
        w = [w_ref[j:j + 1, s * LANE:(s + 1) * LANE] for j in range(k)]
        if scale is not None:
            w = [x * scale for x in w]
        for b in range(CONV_PHASES):
            acc = None
            for j in range(k):
                tap = buf_ref[s, pl.ds(SUBLANE - (k - 1) + j + b, n // CONV_PHASES, stride=CONV_PHASES), :]
                acc = tap * w[j] if acc is None else acc + tap * w[j]
            out_ref[s, pl.ds(b, n // CONV_PHASES, stride=CONV_PHASES), :] = acc if act is None else act(acc)
    for s in range(slabs):
        buf_ref[s, 0:SUBLANE, :] = buf_ref[s, n:n + SUBLANE, :]


def _slabs_value(ref):
    return jnp.concatenate([ref[s] for s in range(ref.shape[0])], axis=1)


def _sample_prev(u, st_ref, fill_ref):
    t = lax.broadcasted_iota(jnp.int32, u.shape, 0) & (SAMPLE_LEN - 1)
    k1 = st_ref.shape[0]
    slabs = u.shape[1] // LANE

    def prev(j):
        for tt in range(j):
            for s in range(slabs):
                fill_ref[j - 1, s, pl.ds(tt, SEQ_PER_GROUP, stride=SAMPLE_LEN), :] = (
                    st_ref[k1 - j + tt, :, s * LANE:(s + 1) * LANE])
        fill = jnp.concatenate([fill_ref[j - 1, s] for s in range(slabs)], axis=1)
        return jnp.where(t >= j, pltpu.roll(u, j, 0), fill)
    return prev


def _pair_iotas():
    row = lax.broadcasted_iota(jnp.int32, (GROUP, 2 * GROUP), 0)
    lane = lax.broadcasted_iota(jnp.int32, (GROUP, 2 * GROUP), 1)
    return row, lane & (GROUP - 1), lane >= GROUP


def _block_diag(y0, y1):
    z = jnp.zeros_like(y0)
    return jnp.concatenate([jnp.concatenate([y0, z], axis=1), jnp.concatenate([z, y1], axis=1)], axis=0)


def _pair_block_diag(y):
    _, _, second = _pair_iotas()
    return jnp.concatenate([jnp.where(second, 0.0, y), jnp.where(second, y, 0.0)], axis=0)


def _tri_inverse_pairs(ms, seq_len):
    row, col, _ = _pair_iotas()
    eye = (row == col).astype(F32)
    base = min(8, seq_len)
    mm = lambda x, y: _dot(x, _pair_block_diag(y))
    same = lambda b: _same_block(row, col, b)
    d = [jnp.where(same(base), m, 0.0) for m in ms]
    d2 = [mm(x, x) for x in d]
    xs = [eye - x for x in d]
    if base > 4:
        d4 = [mm(y, y) for y in d2]
    inv = [x + mm(x, y) for x, y in zip(xs, d2)]
    if base > 4:
        inv = [i + mm(i, y) for i, y in zip(inv, d4)]
    b = base * 2
    while b <= seq_len:
        off_diag = same(b) & jnp.logical_not(same(b // 2))
        t = [mm(i, jnp.where(off_diag, m, 0.0)) for i, m in zip(inv, ms)]
        inv = [i - mm(x, i) for i, x in zip(inv, t)]
        b *= 2
    return inv


def _gdn_group(q, k, v, z, logit, a_row, dt_row, nw_row, seq_len, state, og_ref):
    row = lax.broadcasted_iota(jnp.int32, (GROUP, GROUP), 0)
    col = lax.broadcasted_iota(jnp.int32, (GROUP, GROUP), 1)
    same_seq = _same_block(row, col, seq_len)
    lmat = jnp.concatenate([(same_seq & (row >= col)).astype(F32), same_seq.astype(F32)], axis=0).astype(BF16)
    prow, pcol, second = _pair_iotas()
    psame = _same_block(prow, pcol, seq_len)
    causal = psame & (prow >= pcol)
    strict = psame & (prow > pcol)

    subs = range(q.shape[0] // GROUP)
    heads = range(N_HEADS)
    pairs = range(0, N_HEADS, 2)
    rows = lambda x, c: x[c * GROUP:(c + 1) * GROUP]
    hs = lambda x, c, h: x[c * GROUP:(c + 1) * GROUP, h * HEAD_DIM:(h + 1) * HEAD_DIM]
    lcol = lambda x, h: x[:, LOGIT_A + h:LOGIT_A + h + 1]
    cat = jnp.concatenate

    beta = _sigmoid(logit)
    g = -jnp.exp(a_row) * _softplus(logit + dt_row)
    units = [(c, h) for c in subs for h in pairs]
    gc2 = {c: _dot_exact_lhs(lmat, rows(g, c)) for c in subs}
    gcum = {c: x[:GROUP] for c, x in gc2.items()}
    gtot = {c: x[GROUP:] for c, x in gc2.items()}
    gcum_t2 = {c: cat([x, x], axis=0).T for c, x in gcum.items()}
    e_cum = {c: jnp.exp(x) for c, x in gcum.items()}
    e_rem = {c: jnp.exp(gtot[c] - gcum[c]) for c in subs}
    e_tot = {c: jnp.exp(t) for c, t in gtot.items()}

    qn, kn, bcol, kb = {}, {}, {}, {}
    for c in subs:
        for h in heads:
            qh, kh = hs(q, c, h), hs(k, c, h)
            qn[c, h] = qh * (lax.rsqrt(jnp.sum(qh * qh, axis=-1, keepdims=True) + NORM_EPS) * (HEAD_DIM ** -0.5))
            kn[c, h] = kh * lax.rsqrt(jnp.sum(kh * kh, axis=-1, keepdims=True) + NORM_EPS)
            bcol[c, h] = rows(beta, c)[:, LOGIT_B + h:LOGIT_B + h + 1]
            kb[c, h] = kn[c, h] * bcol[c, h]

    a = {(c, h): _dot_nt(cat([cat([kb[c, h], kb[c, h + 1]], axis=1), cat([qn[c, h], qn[c, h + 1]], axis=1)], axis=0),
                         _block_diag(kn[c, h], kn[c, h + 1])) for c, h in units}
    ms, attn = {}, {}
    for c, h in units:
        gcol = jnp.where(second, lcol(gcum[c], h + 1), lcol(gcum[c], h))
        grow = jnp.where(second[0:1], gcum_t2[c][LOGIT_A + h + 1:LOGIT_A + h + 2, :],
                         gcum_t2[c][LOGIT_A + h:LOGIT_A + h + 1, :])
        dec = jnp.exp(jnp.where(causal, gcol - grow, 0.0))
        ms[c, h] = jnp.where(strict, a[c, h][:GROUP] * dec, 0.0)
        attn[c, h] = jnp.where(causal, a[c, h][GROUP:] * dec, 0.0)
    t_inv = dict(zip(units, _tri_inverse_pairs([ms[un] for un in units], seq_len)))

    u, w, qdec, kdec = {}, {}, {}, {}
    for c, h in units:
        rhs = [cat([hs(v, c, hh) * bcol[c, hh], kb[c, hh] * lcol(e_cum[c], hh)], axis=1) for hh in (h, h + 1)]
        uw = _dot(t_inv[c, h], _block_diag(*rhs))
        for i, hh in enumerate((h, h + 1)):
            u[c, hh] = uw[:, 2 * i * HEAD_DIM:(2 * i + 1) * HEAD_DIM]
            w[c, hh] = uw[:, (2 * i + 1) * HEAD_DIM:(2 * i + 2) * HEAD_DIM]
            qdec[c, hh] = qn[c, hh] * lcol(e_cum[c], hh)
            kdec[c, hh] = kn[c, hh] * lcol(e_rem[c], hh)

    for c in subs:
        ws = [state.read(h, cat([w[c, h], qdec[c, h]], axis=0)) for h in heads]
        v_new = [u[c, h] - ws[h][0] for h in heads]
        vn_bd = {h: _block_diag(v_new[h], v_new[h + 1]) for h in pairs}
        o_pair = {h: cat([ws[h][1], ws[h + 1][1]], axis=1) + _dot(attn[c, h], vn_bd[h]) for h in pairs}
        for h in pairs:
            state.update(h, cat([kdec[c, h], kdec[c, h + 1]], axis=0).T, vn_bd[h],
                         lcol(e_tot[c], h), lcol(e_tot[c], h + 1))
        for h in heads:
            o = o_pair[h - h % 2][:, (h % 2) * HEAD_DIM:(h % 2 + 1) * HEAD_DIM]
            o = o * lax.rsqrt(jnp.mean(o * o, axis=-1, keepdims=True) + NORM_EPS) * nw_row
            zh = hs(z, c, h)
            og_ref[c * GROUP:(c + 1) * GROUP, h * HEAD_DIM:(h + 1) * HEAD_DIM] = (
                o * _silu(zh)).astype(og_ref.dtype)
    state.finish()


def _silu_of_half(h):
    return h * (jnp.tanh(h) + 1.0)


def _silu(x):
    return _silu_of_half(0.5 * x)


def _mixer_common(x_ref, wl_ref, abc_ref, z_ref, par_ref, conv_a, qkv_act, seq_len, state, ya_ref, og_ref):
    gb = abc_ref[:, COL_GB:COL_GB + D_CONV]
    gc = abc_ref[:, COL_GC:COL_GC + D_CONV]
    hv = abc_ref[:, COL_H:COL_H + D_CONV]
    u = gc * hv
    ya_ref[...] = (gb * conv_a(u)).astype(ya_ref.dtype)

    act = qkv_act()
    q, k, v = act[:, :D_GDN], act[:, D_GDN:2 * D_GDN], act[:, 2 * D_GDN:]
    logit = lax.dot_general(x_ref[...], wl_ref[...], (((1,), (1,)), ((), ())), preferred_element_type=F32)
    _gdn_group(q, k, v, z_ref[...], logit, par_ref[0:1, :], par_ref[1:2, :], par_ref[2:3, :],
               seq_len, state, og_ref)
    return u


def _prompt_mixer_kernel(x_ref, wl_ref, abc_ref, qkv_ref, z_ref, caw_ref, gcw_ref, par_ref,
                         ya_ref, og_ref, utail_ref, gtail_ref, s_ref, ubuf_ref, uout_ref, gbuf_ref, gout_ref):
    @pl.when(pl.program_id(1) == 0)
    def _():
        ubuf_ref[:, 0:SUBLANE, :] = jnp.zeros((D_CONV // LANE, SUBLANE, LANE), F32)
        gbuf_ref[:, 0:SUBLANE, :] = jnp.zeros((3 * D_GDN // LANE, SUBLANE, LANE), F32)
        s_ref[...] = jnp.zeros_like(s_ref)

    cur = {}

    class State:
        @staticmethod
        def read(h, lhs):
            if h not in cur:
                cur[h] = s_ref[0, h]
            ws = _dot(lhs, cur[h])
            return ws[:GROUP], ws[GROUP:]

        @staticmethod
        def update(h, kd_t, vn_bd, e0, e1):
            kv = _dot(kd_t, vn_bd)
            cur[h] = cur[h] * e0[0:1, :] + kv[:, :HEAD_DIM]
            cur[h + 1] = cur[h + 1] * e1[0:1, :] + kv[:, HEAD_DIM:]

        @staticmethod
        def finish():
            for h, s in cur.items():
                s_ref[0, h] = s

    def conv_a(u):
        _prompt_conv(u, caw_ref, ubuf_ref, uout_ref)
        return _slabs_value(uout_ref)

    def qkv_act():
        _prompt_conv(qkv_ref[...], gcw_ref, gbuf_ref, gout_ref, _silu_of_half, scale=0.5)
        return _slabs_value(gout_ref)

    u = _mixer_common(x_ref, wl_ref, abc_ref, z_ref, par_ref, conv_a, qkv_act, GROUP, State, ya_ref, og_ref)
    n = u.shape[0]
    utail_ref[0] = u[n - SUBLANE:]
    gtail_ref[0] = qkv_ref[n - SUBLANE:, :]


def _sample_mixer_kernel(x_ref, wl_ref, abc_ref, qkv_ref, z_ref, caw_ref, gcw_ref, par_ref,
                         sta_ref, stg_ref, s0_ref, *rest):
    ya_ref, og_ref, u_ref, gstate_ref, s_ref, fill_a_ref, fill_g_ref, raw_slab_ref = rest[-8:]

    @pl.when(pl.program_id(0) == 0)
    def _():
        fill_a_ref[...] = jnp.zeros_like(fill_a_ref)
        fill_g_ref[...] = jnp.zeros_like(fill_g_ref)

    first = lax.broadcasted_iota(jnp.int32, (SUBLANE, HEAD_DIM), 0) < SAMPLE_LEN
    row_bd = lax.broadcasted_iota(jnp.int32, (2 * GROUP, 2 * HEAD_DIM), 0) & (GROUP - 1)

    class State:
        @staticmethod
        def read(h, lhs):
            w, qdec = lhs[:GROUP], lhs[GROUP:]
            ws_w, ws_q = [], []
            for p in range(GROUP // SUBLANE):
                r = slice(SUBLANE * p, SUBLANE * (p + 1))
                slab = jnp.concatenate([w[r], qdec[r]], axis=0)
                r0 = _dot(slab, s0_ref[2 * p, h])
                r1 = _dot(slab, s0_ref[2 * p + 1, h])
                ws_w.append(jnp.where(first, r0[:SUBLANE], r1[:SUBLANE]))
                ws_q.append(jnp.where(first, r0[SUBLANE:], r1[SUBLANE:]))
            return jnp.concatenate(ws_w, axis=0), jnp.concatenate(ws_q, axis=0)

        @staticmethod
        def update(h, kd_t, vn_bd, e0, e1):
            for s in range(SEQ_PER_GROUP):
                in_seq = _same_block(row_bd, SAMPLE_LEN * s, SAMPLE_LEN)
                kv = _dot(kd_t, jnp.where(in_seq, vn_bd, 0.0))
                r = SAMPLE_LEN * s
                s_ref[s, h] = s0_ref[s, h] * e0[r:r + 1, :] + kv[:, :HEAD_DIM]
                s_ref[s, h + 1] = s0_ref[s, h + 1] * e1[r:r + 1, :] + kv[:, HEAD_DIM:]

        @staticmethod
        def finish():
            pass

    def qkv_act():
        raw = qkv_ref[...]
        for s in range(raw_slab_ref.shape[0]):
            raw_slab_ref[s] = raw[:, s * LANE:(s + 1) * LANE]
        for t in range(1, SAMPLE_LEN):
            gstate_ref[t - 1] = jnp.concatenate(
                [raw_slab_ref[s, pl.ds(t, SEQ_PER_GROUP, stride=SAMPLE_LEN), :] for s in range(raw_slab_ref.shape[0])],
                axis=1)
        return _silu(_causal_conv(raw, gcw_ref, _sample_prev(raw, stg_ref, fill_g_ref)))

    u_ref[...] = _mixer_common(
        x_ref, wl_ref, abc_ref, z_ref, par_ref,
        lambda u: _causal_conv(u, caw_ref, _sample_prev(u, sta_ref, fill_a_ref)), qkv_act,
        SAMPLE_LEN, State, ya_ref, og_ref)


def _mixer_in_specs(layer, row_block, n_rows=GROUP):
    rows = lambda width, col_block: pl.BlockSpec((n_rows, width), lambda *i: (row_block(*i), col_block))
    return [
        rows(D_MODEL, 0),
        _layer_spec(layer, (LANE, D_MODEL)),
        rows(3 * D_CONV, 0),
        rows(3 * D_GDN, COL_QKV // (3 * D_GDN)),
        rows(D_GDN, COL_Z // D_GDN),
        _layer_spec(layer, (3, D_CONV)),
        _layer_spec(layer, (4, 3 * D_GDN)),
        _layer_spec(layer, (SUBLANE, LANE)),
    ]


def _prompt_mixer(layer, x_bf, w_logit, proj, caw, gcw, par, batch, seq):
    step = GROUP * PROMPT_SUB
    chunks = seq // step
    rb = lambda b, c: b * chunks + c
    n = batch * seq
    return pl.pallas_call(
        _prompt_mixer_kernel,
        grid=(batch, chunks),
        in_specs=_mixer_in_specs(layer, rb, step),
        out_specs=[
            pl.BlockSpec((step, D_CONV), lambda b, c: (rb(b, c), 0)),
            pl.BlockSpec((step, D_GDN), lambda b, c: (rb(b, c), 0)),
            pl.BlockSpec((1, SUBLANE, D_CONV), lambda b, c: (b, 0, 0)),
            pl.BlockSpec((1, SUBLANE, 3 * D_GDN), lambda b, c: (b, 0, 0)),
            pl.BlockSpec((1, N_HEADS, HEAD_DIM, HEAD_DIM), lambda b, c: (b, 0, 0, 0)),
        ],
        out_shape=[
            jax.ShapeDtypeStruct((n, D_CONV), BF16),
            jax.ShapeDtypeStruct((n, D_GDN), BF16),
            jax.ShapeDtypeStruct((batch, SUBLANE, D_CONV), F32),
            jax.ShapeDtypeStruct((batch, SUBLANE, 3 * D_GDN), F32),
            jax.ShapeDtypeStruct((batch, N_HEADS, HEAD_DIM, HEAD_DIM), F32),
        ],
        scratch_shapes=[pltpu.VMEM((D_CONV // LANE, step + SUBLANE, LANE), F32),
                        pltpu.VMEM((D_CONV // LANE, step, LANE), F32),
                        pltpu.VMEM((3 * D_GDN // LANE, step + SUBLANE, LANE), F32),
                        pltpu.VMEM((3 * D_GDN // LANE, step, LANE), F32)],
        compiler_params=_params("arbitrary", "arbitrary"),
        name="prompt_mixer",
    )(x_bf, w_logit, proj, proj, proj, caw, gcw, par)


def _sample_mixer(layer, x_bf, w_logit, proj, caw, gcw, par, sta, stg, s0, s_prev, first_block):
    n_seq = s0.shape[1]
    rb = lambda g: first_block + g
    state_block = (SEQ_PER_GROUP, N_HEADS, HEAD_DIM, HEAD_DIM)
    in_specs = _mixer_in_specs(layer, rb) + [
        _layer_spec(layer, (sta.shape[1], SEQ_PER_GROUP, D_CONV), lambda g: (0, g, 0)),
        _layer_spec(layer, (stg.shape[1], SEQ_PER_GROUP, 3 * D_GDN), lambda g: (0, g, 0)),
        _layer_spec(layer, state_block, lambda g: (g, 0, 0, 0)),
    ]
    args = [x_bf, w_logit, proj, proj, proj, caw, gcw, par, sta, stg, s0]
    aliases = {}
    if s_prev is not None:
        in_specs.append(pl.BlockSpec(memory_space=pl.ANY))
        aliases = {len(args): 4}
        args.append(s_prev)
    return pl.pallas_call(
        _sample_mixer_kernel,
        grid=(n_seq // SEQ_PER_GROUP,),
        in_specs=in_specs,
        out_specs=[
            pl.BlockSpec((GROUP, D_CONV), lambda g: (g, 0)),
            pl.BlockSpec((GROUP, D_GDN), lambda g: (g, 0)),
            pl.BlockSpec((GROUP, D_CONV), lambda g: (g, 0)),
            pl.BlockSpec((SAMPLE_LEN - 1, SEQ_PER_GROUP, 3 * D_GDN), lambda g: (0, g, 0)),
            _layer_spec(layer, state_block, lambda g: (g, 0, 0, 0)),
        ],
        out_shape=[
            jax.ShapeDtypeStruct((n_seq * SAMPLE_LEN, D_CONV), BF16),
            jax.ShapeDtypeStruct((n_seq * SAMPLE_LEN, D_GDN), BF16),
            jax.ShapeDtypeStruct((n_seq * SAMPLE_LEN, D_CONV), F32),
            jax.ShapeDtypeStruct((SAMPLE_LEN - 1, n_seq, 3 * D_GDN), F32),
            jax.ShapeDtypeStruct(s0.shape, F32),
        ],
        input_output_aliases=aliases,
        scratch_shapes=[pltpu.VMEM((sta.shape[1], D_CONV // LANE, GROUP, LANE), F32),
                        pltpu.VMEM((stg.shape[1], 3 * D_GDN // LANE, GROUP, LANE), F32),
                        pltpu.VMEM((3 * D_GDN // LANE, GROUP, LANE), F32)],
        compiler_params=_params("arbitrary"),
        name="sample_mixer",
    )(*args)


def _out_proj_kernel(ya_lo, ya_hi, og_lo, og_hi, gate_ref, x_lo, x_hi, wa_ref, wb_ref, wo_ref, g_ref, b_ref,
                     *rest, n_lo):
    n_cast = (len(rest) - 1) // 2
    cast_in, o_ref, cast_out = rest[:n_cast], rest[n_cast], rest[n_cast + 1:]
    for src, dst in zip(cast_in, cast_out):
        dst[...] = src[...].astype(dst.dtype)
    lo = pl.program_id(0) < n_lo
    half = o_ref.shape[0] // 2
    halves = (slice(0, half), slice(half, 2 * half))
    pick = lambda a, b, r: jnp.where(lo, a[r, :], b[r, :])
    dot = lambda a, w_ref: jnp.dot(a, w_ref[...], preferred_element_type=F32)
    y_a = [dot(pick(ya_lo, ya_hi, r), wa_ref) for r in halves]
    y_b = [dot(pick(og_lo, og_hi, r), wb_ref) for r in halves]
    merged = [gate_ref[r, :D_MODEL].astype(F32) * a + gate_ref[r, D_MODEL:].astype(F32) * b
              for r, a, b in zip(halves, y_a, y_b)]
    z = [dot(m.astype(BF16), wo_ref) for m in merged]
    for r, zz in zip(halves, z):
        o_ref[r, :] = _layer_norm(ALPHA * pick(x_lo, x_hi, r) + zz, g_ref[...], b_ref[...])


def _out_proj(layer, ya_p, ya_s, og_p, og_s, gates, x_lo, x_hi, hi_block0, wa, wb, wo, g, b, tm, cast=()):
    t = gates.shape[0]
    n, n_lo = t // tm, ya_p.shape[0] // tm
    lo = lambda width: pl.BlockSpec((tm, width), lambda i: (jnp.minimum(i, n_lo - 1), 0))
    hi = lambda width, block0=0: pl.BlockSpec((tm, width), lambda i: (jnp.maximum(i - n_lo, 0) + block0, 0),
                                              pipeline_mode=pl.Buffered(1))
    const = lambda shape: _layer_spec(layer, shape, pipeline_mode=pl.Buffered(1))
    weight = lambda shape: pl.BlockSpec(shape, lambda i: (0, 0), pipeline_mode=pl.Buffered(1))
    n_c = 1 << (n.bit_length() - 1)
    rows = [w.shape[1] // n_c for w in cast]
    assert all(r % (2 * SUBLANE) == 0 and r * n_c == w.shape[1] for r, w in zip(rows, cast))
    cast_block = lambda i: (jnp.minimum(i, n_c - 1), 0)
    outs = pl.pallas_call(
        functools.partial(_out_proj_kernel, n_lo=n_lo),
        grid=(n,),
        in_specs=[
            lo(D_CONV), hi(D_CONV), lo(D_GDN), hi(D_GDN),
            pl.BlockSpec((tm, 2 * D_MODEL), lambda i: (i, 0)),
            lo(D_MODEL), hi(D_MODEL, hi_block0),
            weight((D_CONV, D_MODEL)), weight((D_GDN, D_MODEL)), weight((D_MODEL, D_MODEL)),
            const((1, D_MODEL)), const((1, D_MODEL)),
        ] + [_layer_spec(layer, (r, w.shape[2]), cast_block) for r, w in zip(rows, cast)],
        out_specs=[pl.BlockSpec((tm, D_MODEL), lambda i: (i, 0))]
        + [pl.BlockSpec((r, w.shape[2]), cast_block) for r, w in zip(rows, cast)],
        out_shape=[jax.ShapeDtypeStruct((t, D_MODEL), F32)]
        + [jax.ShapeDtypeStruct(w.shape[1:], BF16) for w in cast],
        compiler_params=_params("arbitrary"),
        name="out_proj",
    )(ya_p, ya_s, og_p, og_s, gates, x_lo, x_hi, wa, wb, wo, g, b, *cast)
    return outs


def _mlp_kernel(x_ref, wu_ref, wd_ref, g_ref, b_ref, o0_ref, o1_ref, xb_ref, acc_ref, *, n_lo):
    i, j = pl.program_id(0), pl.program_id(1)

    @pl.when(j == 0)
    def _():
        xb_ref[...] = x_ref[...].astype(BF16)
        acc_ref[...] = jnp.zeros_like(acc_ref)

    hid = jnp.maximum(jnp.dot(xb_ref[...], wu_ref[...], preferred_element_type=F32), 0.0)
    acc_ref[...] += jnp.dot((hid * hid).astype(BF16), wd_ref[...], preferred_element_type=F32)

    @pl.when(j == pl.num_programs(1) - 1)
    def _():
        y = _layer_norm(ALPHA * x_ref[...] + acc_ref[...], g_ref[...], b_ref[...])
        if n_lo is None:
            o0_ref[...] = y
            o1_ref[...] = y.astype(BF16)
        else:
            @pl.when(i < n_lo)
            def _():
                o0_ref[...] = y

            @pl.when(i >= n_lo)
            def _():
                o1_ref[...] = y


def _mlp(layer, x, wu, wd, g, b, tm, tf, n_prompt=None):
    t = x.shape[0]
    if n_prompt is None:
        n_lo = None
        out_specs = [pl.BlockSpec((tm, D_MODEL), lambda i, j: (i, 0)),
                     pl.BlockSpec((tm, D_MODEL), lambda i, j: (i, 0))]
        out_shape = [jax.ShapeDtypeStruct((t, D_MODEL), F32), jax.ShapeDtypeStruct((t, D_MODEL), BF16)]
    else:
        n_lo = n_prompt // tm
        out_specs = [pl.BlockSpec((tm, D_MODEL), lambda i, j: (jnp.minimum(i, n_lo - 1), 0)),
                     pl.BlockSpec((tm, D_MODEL), lambda i, j: (jnp.maximum(i - n_lo, 0), 0))]
        out_shape = [jax.ShapeDtypeStruct((n_prompt, D_MODEL), F32),
                     jax.ShapeDtypeStruct((t - n_prompt, D_MODEL), F32)]
    return pl.pallas_call(
        functools.partial(_mlp_kernel, n_lo=n_lo),
        grid=(t // tm, D_FF // tf),
        in_specs=[
            pl.BlockSpec((tm, D_MODEL), lambda i, j: (i, 0)),
            pl.BlockSpec((D_MODEL, tf), lambda i, j: (0, j)),
            pl.BlockSpec((tf, D_MODEL), lambda i, j: (j, 0)),
            _layer_spec(layer, (1, D_MODEL)),
            _layer_spec(layer, (1, D_MODEL)),
        ],
        out_specs=out_specs,
        out_shape=out_shape,
        scratch_shapes=[pltpu.VMEM((tm, D_MODEL), BF16), pltpu.VMEM((tm, D_MODEL), F32)],
        compiler_params=_params("arbitrary", "arbitrary"),
        name="mlp",
    )(x, wu, wd, g, b)


def kernel(x_prompt, x_sample, state_conv_a, state_gdn_conv, state_gdn, w_in, conv_a_w, gdn_conv_w, a_log,
           dt_bias, gdn_norm_w, w_a_out, w_b_out, w_o, ln1_g, ln1_b, w_up, w_down, ln2_g, ln2_b):
    batch, seq, _ = x_prompt.shape
    n_seq = x_sample.shape[0]
    n_prompt = batch * seq
    n_sample = n_seq * SAMPLE_LEN
    tm_proj, tm_out, tm_mlp = TM_PROJ, TM_OUT, TM_MLP
    assert x_sample.shape[1] == SAMPLE_LEN and seq % (GROUP * PROMPT_SUB) == 0 and n_seq % SEQ_PER_GROUP == 0
    assert (n_prompt + n_sample) % tm_proj == 0
    assert n_prompt % tm_out == 0 and n_sample % tm_out == 0 and n_prompt % tm_mlp == 0 and n_sample == tm_mlp

    xp = x_prompt.reshape(n_prompt, D_MODEL)
    xs = x_sample.reshape(n_sample, D_MODEL)
    x_bf = _bf16_stream(xp, xs, tm_mlp)
    x_lo, x_hi, hi_block0 = xp, xs, 0

    w_t = jnp.swapaxes(w_in, 1, 2)
    w_logit = jnp.pad(w_t[:, COL_LOGIT:COL_GATE], ((0, 0), (0, LANE - 2 * N_HEADS), (0, 0))).astype(BF16)
    par = jnp.zeros((DEPTH, SUBLANE, LANE), F32)
    par = par.at[:, 0, LOGIT_A:LOGIT_A + N_HEADS].set(a_log.astype(F32))
    par = par.at[:, 1, LOGIT_A:LOGIT_A + N_HEADS].set(dt_bias.astype(F32))
    par = par.at[:, 2, :].set(gdn_norm_w.astype(F32))
    sta, stg = jnp.swapaxes(state_conv_a, 1, 2), jnp.swapaxes(state_gdn_conv, 1, 2)
    row3 = lambda a: a.reshape(DEPTH, 1, D_MODEL)

    ca_p, cg_p, sg_p, ca_s, cg_s = [], [], [], [], []
    sg_s = None
    for l in range(DEPTH):
        proj = _proj(x_bf, w_t, l, 0, N_MIX, F32, False, tm_proj, TN_PROJ, "proj")
        gates, wa, wb, wo = _proj(x_bf, w_t, l, COL_GATE, 2 * D_MODEL, BF16, True, tm_proj, TN_PROJ, "gates",
                                  cast=(w_a_out, w_b_out, w_o))

        ya_p, og_p, utail, gtail, s_p = _prompt_mixer(l, x_bf, w_logit, proj, conv_a_w, gdn_conv_w, par, batch, seq)
        ya_s, og_s, u_s, gstate_s, sg_s = _sample_mixer(
            l, x_bf, w_logit, proj, conv_a_w, gdn_conv_w, par, sta, stg, state_gdn, sg_s, n_prompt // GROUP)

        x1, wu, wd = _out_proj(l, ya_p, ya_s, og_p, og_s, gates, x_lo, x_hi, hi_block0, wa, wb, wo,
                               row3(ln1_g), row3(ln1_b), tm_out, cast=(w_up, w_down))
        if l < DEPTH - 1:
            x, x_bf = _mlp(l, x1, wu, wd, row3(ln2_g), row3(ln2_b), tm_mlp, TF_MLP)
            x_lo, x_hi, hi_block0 = x, x, n_prompt // tm_out
        else:
            y_p, y_s = _mlp(l, x1, wu, wd, row3(ln2_g), row3(ln2_b), tm_mlp, TF_MLP, n_prompt=n_prompt)

        ca_p.append(utail[:, SUBLANE - state_conv_a.shape[2]:])
        cg_p.append(gtail[:, SUBLANE - state_gdn_conv.shape[2]:])
        sg_p.append(s_p)
        ca_s.append(u_s.reshape(n_seq, SAMPLE_LEN, D_CONV)[:, SAMPLE_LEN - state_conv_a.shape[2]:])
        cg_s.append(gstate_s)

    return (y_p.reshape(batch, seq, D_MODEL), y_s.reshape(n_seq, SAMPLE_LEN, D_MODEL),
            jnp.stack(ca_p), jnp.stack(cg_p), jnp.stack(sg_p),
            jnp.stack(ca_s), jnp.swapaxes(jnp.stack(cg_s), 1, 2), sg_s)
```
